```python
import math
import jax, jax.numpy as jnp
from jax import lax
import numpy as np

D_MODEL = 2048
BATCH = 2
SEQ = 8192
DEPTH = 1
DEC_BATCH = 8
DEC_SEQ = 64
PAST_LEN = 4096

CHUNK = 64
Q_BLOCK = 128
EPS = 1e-6
NEG = -1e30
D_CONV = D_MODEL // 2
CONV_W = 31
N_HEADS = 16
HEAD_DIM = 64
D_ATTN = N_HEADS * 2 * HEAD_DIM
MEM_TOKENS = 256
MEM_HEADS = 4
MEM_HEAD_DIM = 256
D_MEM = MEM_HEADS * MEM_HEAD_DIM
N_BRANCH = 3
N_IN = 2 * D_CONV + 3 * D_ATTN + D_MEM + N_BRANCH * D_MODEL
N_GROUPS = 4
EXPERTS_PER_GROUP = 8
N_EXPERTS = N_GROUPS * EXPERTS_PER_GROUP
TOP_K = 2
D_EXPERT = 256

kernel_name = 'hybrid_streaming_encoder_step'


def rmsnorm(x, g, eps=EPS):
    xf = x.astype(jnp.float32)
    y = xf * lax.rsqrt(jnp.mean(xf * xf, axis=-1, keepdims=True) + eps)
    return (y * g.astype(jnp.float32)).astype(x.dtype)


def layernorm(x, g, b, eps=EPS):
    xf = x.astype(jnp.float32)
    mu = jnp.mean(xf, axis=-1, keepdims=True)
    var = jnp.mean(jnp.square(xf - mu), axis=-1, keepdims=True)
    y = (xf - mu) * lax.rsqrt(var + eps) * g.astype(jnp.float32) + b.astype(jnp.float32)
    return y.astype(x.dtype)


def conv_branch(u_in, conv_buf, w_dw, b_dw, ln_g, ln_b, w_conv_out):
    a, gt = jnp.split(u_in, 2, axis=-1)
    u = a * jax.nn.sigmoid(gt)
    ext = jnp.concatenate([conv_buf.astype(u.dtype), u], axis=1)
    y = lax.conv_general_dilated(ext, w_dw[:, None, :].astype(u.dtype), window_strides=(1,), padding='VALID',
                                 dimension_numbers=('NWC', 'WIO', 'NWC'), feature_group_count=D_CONV)
    y = jax.nn.silu(layernorm(y + b_dw, ln_g, ln_b))
    return y @ w_conv_out, ext[:, -(CONV_W - 1):]


def diff_attention(q, k, v, past_len, lam):
    B, T = q.shape[0], q.shape[1]
    bs = min(Q_BLOCK, T)
    nb = T // bs
    k_pos = jnp.arange(k.shape[1])
    kf = k.astype(jnp.float32)
    qb = jnp.moveaxis(q.reshape(B, nb, bs, N_HEADS, 2, HEAD_DIM), 1, 0)
    scale = HEAD_DIM ** -0.5

    def one_block(args):
        q_blk, blk = args
        q_pos = past_len + blk * bs + jnp.arange(bs)
        s = jnp.einsum('bqhcd,bkhcd->bhcqk', q_blk.astype(jnp.float32), kf) * scale
        mask = (k_pos[None, :] // CHUNK) <= (q_pos[:, None] // CHUNK)
        p = jax.nn.softmax(jnp.where(mask, s, NEG), axis=-1)
        a = p[:, :, 0] - lam * p[:, :, 1]
        return jnp.einsum('bhqk,bkhe->bqhe', a.astype(v.dtype), v)

    out = lax.map(one_block, (qb, jnp.arange(nb)))
    return jnp.moveaxis(out, 0, 1).reshape(B, T, N_HEADS, 2 * HEAD_DIM)


def memory_kv(mem, norm_mem, w_mem_k, w_mem_v):
    B, M = mem.shape[0], mem.shape[1]
    hm = rmsnorm(mem, norm_mem)
    mk = (hm @ w_mem_k).reshape(B, M, MEM_HEADS, MEM_HEAD_DIM)
    mv = (hm @ w_mem_v).reshape(B, M, MEM_HEADS, MEM_HEAD_DIM)
    return mk, mv


def mem_attention(mq, mk, mv):
    B, T = mq.shape[0], mq.shape[1]
    s = jnp.einsum('bqhd,bmhd->bhqm', mq.astype(jnp.float32), mk.astype(jnp.float32)) * (MEM_HEAD_DIM ** -0.5)
    p = jax.nn.softmax(s, axis=-1)
    return jnp.einsum('bhqm,bmhd->bqhd', p.astype(mv.dtype), mv).reshape(B, T, D_MEM)


def mixer_sublayer(x, conv_buf, k_past, v_past, mk, mv, norm_mix, w_in, w_dw, b_dw, conv_ln_g, conv_ln_b,
                   w_conv_out, lam, lam_init, subln_g, w_attn_out, w_mem_out, w_out):
    B, T = x.shape[0], x.shape[1]
    h = rmsnorm(x, norm_mix)
    proj = h @ w_in
    c0 = 2 * D_CONV
    c1 = c0 + D_ATTN
    c2 = c1 + D_ATTN
    c3 = c2 + D_ATTN
    c4 = c3 + D_MEM
    conv_y, new_conv_buf = conv_branch(proj[..., :c0], conv_buf, w_dw, b_dw, conv_ln_g, conv_ln_b, w_conv_out)
    q = proj[..., c0:c1].reshape(B, T, N_HEADS, 2, HEAD_DIM)
    k_rows = proj[..., c1:c2].reshape(B, T, N_HEADS, 2 * HEAD_DIM)
    v_rows = proj[..., c2:c3].reshape(B, T, N_HEADS, 2 * HEAD_DIM)
    mq = proj[..., c3:c4].reshape(B, T, MEM_HEADS, MEM_HEAD_DIM)
    gates = jax.nn.sigmoid(proj[..., c4:].reshape(B, T, N_BRANCH, D_MODEL))
    past_len = k_past.shape[1]
    k_all = jnp.concatenate([k_past.astype(k_rows.dtype), k_rows], axis=1).reshape(B, past_len + T, N_HEADS, 2, HEAD_DIM)
    v_all = jnp.concatenate([v_past.astype(v_rows.dtype), v_rows], axis=1)
    o = diff_attention(q, k_all, v_all, past_len, lam)
    o = rmsnorm(o, subln_g, 1e-5) * (1.0 - lam_init)
    attn_y = o.reshape(B, T, D_ATTN) @ w_attn_out
    mem_y = mem_attention(mq, mk, mv) @ w_mem_out
    merged = gates[:, :, 0] * conv_y + gates[:, :, 1] * attn_y + gates[:, :, 2] * mem_y
    return x + merged @ w_out, new_conv_buf, k_rows, v_rows


def hier_moe(h, w_rg, b_rg, w_re, b_re, w_g, w_u, w_d):
    shp = h.shape
    hf = h.reshape(-1, D_MODEL)
    ntok = hf.shape[0]
    grp_logits = (hf @ w_rg + b_rg).astype(jnp.float32)
    p_grp = jax.nn.softmax(grp_logits, axis=-1)
    g_idx = jnp.argmax(grp_logits, axis=-1)
    p_g = jnp.take_along_axis(p_grp, g_idx[:, None], axis=1)[:, 0]
    exp_logits = (hf @ w_re + b_re).astype(jnp.float32).reshape(ntok, N_GROUPS, EXPERTS_PER_GROUP)
    sel = jnp.take_along_axis(exp_logits, g_idx[:, None, None], axis=1)[:, 0]
    top_v, top_i = lax.top_k(jax.nn.softmax(sel, axis=-1), TOP_K)
    top_v = top_v / jnp.sum(top_v, axis=-1, keepdims=True)
    expert_id = g_idx[:, None] * EXPERTS_PER_GROUP + top_i
    combine = jnp.einsum('nk,nke->ne', p_g[:, None] * top_v,
                         jax.nn.one_hot(expert_id, N_EXPERTS, dtype=jnp.float32)).astype(h.dtype)
    y = jnp.zeros_like(hf)
    for e in range(N_EXPERTS):
        ye = (jax.nn.silu(hf @ w_g[e]) * (hf @ w_u[e])) @ w_d[e]
        y = y + combine[:, e:e + 1] * ye
    return y.reshape(shp)


def setup_inputs(seed: int = 0) -> dict:
    key = jax.random.key(seed)
    ks = iter(jax.random.split(key, 48))

    def nrm(shape, scale):
        return jax.random.normal(next(ks), shape, jnp.float32) * scale

    def gain(shape):
        return 1.0 + nrm(shape, 0.01)

    L = DEPTH
    return {
        'x_prompt': nrm((BATCH, SEQ, D_MODEL), 1.0),
        'x_sample': nrm((DEC_BATCH, DEC_SEQ, D_MODEL), 1.0),
        'mem_prompt': nrm((BATCH, MEM_TOKENS, D_MODEL), 1.0),
        'cache_conv': nrm((L, DEC_BATCH, CONV_W - 1, D_CONV), 1.0),
        'cache_diff_k': nrm((L, DEC_BATCH, PAST_LEN, N_HEADS, 2 * HEAD_DIM), 1.0),
        'cache_diff_v': nrm((L, DEC_BATCH, PAST_LEN, N_HEADS, 2 * HEAD_DIM), 1.0),
        'cache_mem_k': nrm((L, DEC_BATCH, MEM_TOKENS, MEM_HEADS, MEM_HEAD_DIM), 1.0),
        'cache_mem_v': nrm((L, DEC_BATCH, MEM_TOKENS, MEM_HEADS, MEM_HEAD_DIM), 1.0),
        'norm_mix': gain((L, D_MODEL)),
        'w_in': nrm((L, D_MODEL, N_IN), D_MODEL ** -0.5),
        'w_dw': nrm((L, CONV_W, D_CONV), CONV_W ** -0.5),
        'b_dw': nrm((L, D_CONV), 0.01),
        'conv_ln_g': gain((L, D_CONV)),
        'conv_ln_b': nrm((L, D_CONV), 0.01),
        'w_conv_out': nrm((L, D_CONV, D_MODEL), D_CONV ** -0.5),
        'lambda_q1': nrm((L, HEAD_DIM), 0.1),
        'lambda_k1': nrm((L, HEAD_DIM), 0.1),
        'lambda_q2': nrm((L, HEAD_DIM), 0.1),
        'lambda_k2': nrm((L, HEAD_DIM), 0.1),
        'subln_g': gain((L, 2 * HEAD_DIM)),
        'w_attn_out': nrm((L, D_ATTN, D_MODEL), D_ATTN ** -0.5),
        'norm_mem': gain((L, D_MODEL)),
        'w_mem_k': nrm((L, D_MODEL, D_MEM), D_MODEL ** -0.5),
        'w_mem_v': nrm((L, D_MODEL, D_MEM), D_MODEL ** -0.5),
        'w_mem_out': nrm((L, D_MEM, D_MODEL), D_MEM ** -0.5),
        'w_out': nrm((L, D_MODEL, D_MODEL), D_MODEL ** -0.5),
        'norm_ffn': gain((L, D_MODEL)),
        'w_router_grp': nrm((L, D_MODEL, N_GROUPS), D_MODEL ** -0.5),
        'b_router_grp': nrm((L, N_GROUPS), 0.01),
        'w_router_exp': nrm((L, D_MODEL, N_EXPERTS), D_MODEL ** -0.5),
        'b_router_exp': nrm((L, N_EXPERTS), 0.01),
        'w_exp_gate': nrm((L, N_EXPERTS, D_MODEL, D_EXPERT), D_MODEL ** -0.5),
        'w_exp_up': nrm((L, N_EXPERTS, D_MODEL, D_EXPERT), D_MODEL ** -0.5),
        'w_exp_down': nrm((L, N_EXPERTS, D_EXPERT, D_MODEL), D_EXPERT ** -0.5),
        'norm_final': gain((D_MODEL,)),
    }


def reference(x_prompt, x_sample, mem_prompt, cache_conv, cache_diff_k, cache_diff_v, cache_mem_k, cache_mem_v,
              norm_mix, w_in, w_dw, b_dw, conv_ln_g, conv_ln_b, w_conv_out, lambda_q1, lambda_k1, lambda_q2,
              lambda_k2, subln_g, w_attn_out, norm_mem, w_mem_k, w_mem_v, w_mem_out, w_out, norm_ffn,
              w_router_grp, b_router_grp, w_router_exp, b_router_exp, w_exp_gate, w_exp_up, w_exp_down, norm_final):
    xp, xs = x_prompt, x_sample
    bp = x_prompt.shape[0]
    conv_p, k_p, v_p, mk_p, mv_p, conv_s, k_s, v_s = [], [], [], [], [], [], [], []
    for l in range(DEPTH):
        lam_init = 0.8 - 0.6 * math.exp(-0.3 * l)
        lam = (jnp.exp(jnp.sum(lambda_q1[l].astype(jnp.float32) * lambda_k1[l].astype(jnp.float32)))
               - jnp.exp(jnp.sum(lambda_q2[l].astype(jnp.float32) * lambda_k2[l].astype(jnp.float32))) + lam_init)
        shared = (norm_mix[l], w_in[l], w_dw[l], b_dw[l], conv_ln_g[l], conv_ln_b[l], w_conv_out[l], lam, lam_init,
                  subln_g[l], w_attn_out[l], w_mem_out[l], w_out[l])
        moe_w = (w_router_grp[l], b_router_grp[l], w_router_exp[l], b_router_exp[l], w_exp_gate[l], w_exp_up[l], w_exp_down[l])
        mk, mv = memory_kv(mem_prompt, norm_mem[l], w_mem_k[l], w_mem_v[l])
        conv0 = jnp.zeros((bp, CONV_W - 1, D_CONV), xp.dtype)
        kv0 = jnp.zeros((bp, 0, N_HEADS, 2 * HEAD_DIM), xp.dtype)
        xp, cb_p, kr_p, vr_p = mixer_sublayer(xp, conv0, kv0, kv0, mk, mv, *shared)
        xp = xp + hier_moe(rmsnorm(xp, norm_ffn[l]), *moe_w)
        xs, cb_s, kr_s, vr_s = mixer_sublayer(xs, cache_conv[l], cache_diff_k[l], cache_diff_v[l],
                                              cache_mem_k[l], cache_mem_v[l], *shared)
        xs = xs + hier_moe(rmsnorm(xs, norm_ffn[l]), *moe_w)
        conv_p.append(cb_p); k_p.append(kr_p); v_p.append(vr_p); mk_p.append(mk); mv_p.append(mv)
        conv_s.append(cb_s); k_s.append(kr_s); v_s.append(vr_s)
    y_prompt = rmsnorm(xp, norm_final)
    y_sample = rmsnorm(xs, norm_final)
    return (y_prompt, y_sample, jnp.stack(conv_p), jnp.stack(k_p), jnp.stack(v_p), jnp.stack(mk_p), jnp.stack(mv_p),
            jnp.stack(conv_s), jnp.stack(k_s), jnp.stack(v_s))
```

```python
import functools
import math

import jax
import jax.numpy as jnp
from jax import lax
from jax.experimental import pallas as pl
from jax.experimental.pallas import tpu as pltpu

F32 = jnp.float32
BF16 = jnp.bfloat16

CHUNK = 64
EPS = 1e-6
SUBLN_EPS = 1e-5
NEG = -1e30
CONV_W = 31
N_HEADS = 16
HEAD_DIM = 64
MEM_HEADS = 4
MEM_HEAD_DIM = 256
N_BRANCH = 3
N_GROUPS = 4
EXPERTS_PER_GROUP = 8
N_EXPERTS = N_GROUPS * EXPERTS_PER_GROUP
TOP_K = 2

LANES = 128
SUBLANES = 8
VMEM_LIMIT_BYTES = 56 * 1024 * 1024
HALO = 32


def _params(*semantics):
    return pltpu.CompilerParams(dimension_semantics=semantics, vmem_limit_bytes=VMEM_LIMIT_BYTES)


def _tile(n, pref):
    if n <= pref:
        return n
    t = pref
    while n % t:
        t //= 2
    return t


def _rmsnorm_kernel(x_ref, g_ref, o_ref, *, eps):
    x = x_ref[...].astype(F32)
    ms = jnp.mean(x * x, axis=-1, keepdims=True)
    o_ref[...] = (x * lax.rsqrt(ms + eps) * g_ref[...]).astype(o_ref.dtype)


def _rmsnorm(x, g, out_dtype, eps=EPS):
    n, d = x.shape
    tr = _tile(n, 512)
    return pl.pallas_call(
        functools.partial(_rmsnorm_kernel, eps=eps),
        grid=(n // tr,),
        in_specs=[pl.BlockSpec((tr, d), lambda i: (i, 0)), pl.BlockSpec((1, d), lambda i: (0, 0))],
        out_specs=pl.BlockSpec((tr, d), lambda i: (i, 0)),
        out_shape=jax.ShapeDtypeStruct((n, d), out_dtype),
        compiler_params=_params("parallel"),
        name="rmsnorm",
    )(x, g.reshape(1, d).astype(F32))


def _mm_kernel(*refs, has_gate, has_add):
    a_ref, b_ref = refs[0], refs[1]
    o_ref = refs[-1]
    acc = jnp.dot(a_ref[...], b_ref[...], preferred_element_type=F32)
    pos = 2
    if has_gate:
        acc = acc * jax.nn.sigmoid(refs[pos][...].astype(F32))
        pos += 1
    if has_add:
        acc = acc + refs[pos][...].astype(F32)
    o_ref[...] = acc.astype(o_ref.dtype)


def _matmul(a, b, out_dtype, *, b_col0=0, n_out=None, gate=None, gate_col0=0, add=None):
    m, k = a.shape
    n_out = b.shape[1] if n_out is None else n_out
    tm = _tile(m, 512)
    tn = _tile(n_out, 1024)
    assert b_col0 % tn == 0 and gate_col0 % tn == 0
    bj0, gj0 = b_col0 // tn, gate_col0 // tn
    in_specs = [pl.BlockSpec((tm, k), lambda j, i: (i, 0)),
                pl.BlockSpec((k, tn), lambda j, i: (0, bj0 + j))]
    args = [a, b]
    if gate is not None:
        in_specs.append(pl.BlockSpec((tm, tn), lambda j, i: (i, gj0 + j)))
        args.append(gate)
    if add is not None:
        in_specs.append(pl.BlockSpec((tm, tn), lambda j, i: (i, j)))
        args.append(add)
    return pl.pallas_call(
        functools.partial(_mm_kernel, has_gate=gate is not None, has_add=add is not None),
        grid=(n_out // tn, m // tm),
        in_specs=in_specs,
        out_specs=pl.BlockSpec((tm, tn), lambda j, i: (i, j)),
        out_shape=jax.ShapeDtypeStruct((m, n_out), out_dtype),
        compiler_params=_params("parallel", "parallel"),
        name="matmul",
    )(*args)


def _conv_kernel(u_ref, buf_ref, w_ref, bdw_ref, g_ref, b_ref, act_ref, newbuf_ref, ext_ref, *, tt, dc):
    t = pl.program_id(1)
    hist = CONV_W - 1

    @pl.when(t == 0)
    def _():
        ext_ref[pl.ds(HALO - hist, hist), :] = buf_ref[0].astype(F32)

    @pl.when(t > 0)
    def _():
        ext_ref[pl.ds(0, HALO), :] = ext_ref[pl.ds(tt, HALO), :]

    a = u_ref[0, :, :dc].astype(F32)
    gt = u_ref[0, :, dc:].astype(F32)
    ext_ref[pl.ds(HALO, tt), :] = a * jax.nn.sigmoid(gt)

    y = jnp.zeros((tt, dc), F32) + bdw_ref[...]
    for j in range(CONV_W):
        y = y + ext_ref[pl.ds(HALO - hist + j, tt), :] * w_ref[pl.ds(j, 1), :]
    mu = jnp.mean(y, axis=-1, keepdims=True)
    var = jnp.mean(jnp.square(y - mu), axis=-1, keepdims=True)
    z = (y - mu) * lax.rsqrt(var + EPS) * g_ref[...] + b_ref[...]
    act_ref[0] = (z * jax.nn.sigmoid(z)).astype(act_ref.dtype)
    newbuf_ref[0] = ext_ref[pl.ds(HALO + tt - hist, hist), :]


def _conv_branch(u_in, conv_buf, w_dw, b_dw, ln_g, ln_b):
    bsz, t_len, two_dc = u_in.shape
    dc = two_dc // 2
    tt = _tile(t_len, 256)
    assert tt >= HALO and tt % SUBLANES == 0
    vec = lambda v: v.reshape(1, dc).astype(F32)
    return pl.pallas_call(
        functools.partial(_conv_kernel, tt=tt, dc=dc),
        grid=(bsz, t_len // tt),
        in_specs=[pl.BlockSpec((1, tt, two_dc), lambda b, t: (b, t, 0)),
                  pl.BlockSpec((1, CONV_W - 1, dc), lambda b, t: (b, 0, 0)),
                  pl.BlockSpec((CONV_W, dc), lambda b, t: (0, 0)),
                  pl.BlockSpec((1, dc), lambda b, t: (0, 0)),
                  pl.BlockSpec((1, dc), lambda b, t: (0, 0)),
                  pl.BlockSpec((1, dc), lambda b, t: (0, 0))],
        out_specs=[pl.BlockSpec((1, tt, dc), lambda b, t: (b, t, 0)),
                   pl.BlockSpec((1, CONV_W - 1, dc), lambda b, t: (b, 0, 0))],
        out_shape=[jax.ShapeDtypeStruct((bsz, t_len, dc), BF16),
                   jax.ShapeDtypeStruct((bsz, CONV_W - 1, dc), F32)],
        scratch_shapes=[pltpu.VMEM((HALO + tt, dc), F32)],
        compiler_params=_params("parallel", "arbitrary"),
        name="conv_branch",
    )(u_in, conv_buf, w_dw.astype(F32), vec(b_dw), vec(ln_g), vec(ln_b))


def _diff_attn_kernel(*refs, has_past, past_len, tq, tkp, lam_init):
    if has_past:
        (q_ref, kc_ref, vc_ref, kp_ref, vp_ref, lq1_ref, lk1_ref, lq2_ref, lk2_ref, g_ref,
         o_ref, acc_ref, m_ref, l_ref) = refs
    else:
        (q_ref, kc_ref, vc_ref, lq1_ref, lk1_ref, lq2_ref, lk2_ref, g_ref,
         o_ref, acc_ref, m_ref, l_ref) = refs
    qi = pl.program_id(2)
    hd2 = 2 * HEAD_DIM

    q = q_ref[0] * jnp.asarray(HEAD_DIM ** -0.5, q_ref.dtype)
    lane = lax.broadcasted_iota(jnp.int32, (tq, hd2), 1)
    zero = jnp.zeros_like(q)
    qq = jnp.concatenate([jnp.where(lane < HEAD_DIM, q, zero), jnp.where(lane >= HEAD_DIM, q, zero)], axis=0)

    m_ref[...] = jnp.full(m_ref.shape, NEG, F32)
    l_ref[...] = jnp.zeros(l_ref.shape, F32)
    acc_ref[...] = jnp.zeros(acc_ref.shape, F32)

    def step(k_blk, v_blk, mask):
        s = lax.dot_general(qq, k_blk.astype(BF16), (((1,), (1,)), ((), ())), preferred_element_type=F32)
        if mask is not None:
            s = jnp.where(mask, s, NEG)
        m_prev = m_ref[...]
        m_new = jnp.maximum(m_prev, jnp.max(s, axis=1, keepdims=True))
        alpha = jnp.exp(m_prev - m_new)
        p = jnp.exp(s - m_new)
        l_ref[...] = alpha * l_ref[...] + jnp.sum(p, axis=1, keepdims=True)
        acc_ref[...] = alpha * acc_ref[...] + jnp.dot(p.astype(BF16), v_blk.astype(BF16), preferred_element_type=F32)
        m_ref[...] = m_new

    if has_past:
        def past_body(i, carry):
            off = pl.multiple_of(i * tkp, tkp)
            step(kp_ref[0, pl.ds(off, tkp), :], vp_ref[0, pl.ds(off, tkp), :], None)
            return carry
        lax.fori_loop(0, past_len // tkp, past_body, 0)

    def cur_body(kb, carry):
        off = pl.multiple_of(kb * tq, tq)
        step(kc_ref[0, pl.ds(off, tq), :], vc_ref[0, pl.ds(off, tq), :], None)
        return carry
    lax.fori_loop(0, qi, cur_body, 0)

    row = lax.broadcasted_iota(jnp.int32, (2 * tq, tq), 0)
    col = lax.broadcasted_iota(jnp.int32, (2 * tq, tq), 1)
    q_chunk = lax.rem(row, tq) // CHUNK
    mask = (col // CHUNK) <= q_chunk
    off = pl.multiple_of(qi * tq, tq)
    step(kc_ref[0, pl.ds(off, tq), :], vc_ref[0, pl.ds(off, tq), :], mask)

    lam = (jnp.exp(jnp.sum(lq1_ref[...] * lk1_ref[...], axis=1, keepdims=True))
           - jnp.exp(jnp.sum(lq2_ref[...] * lk2_ref[...], axis=1, keepdims=True)) + lam_init)
    acc = acc_ref[...]
    l = l_ref[...]
    o = acc[:tq] / l[:tq] - lam * (acc[tq:] / l[tq:])
    o = o * lax.rsqrt(jnp.mean(o * o, axis=-1, keepdims=True) + SUBLN_EPS) * g_ref[...]
    o_ref[0] = (o * (1.0 - lam_init)).astype(o_ref.dtype)


def _diff_attention(q, k_cur, v_cur, k_past, v_past, lq1, lk1, lq2, lk2, subln_g, lam_init):
    bsz, t_len, d_attn = q.shape
    hd2 = 2 * HEAD_DIM
    n_heads = d_attn // hd2
    has_past = k_past is not None
    past_len = k_past.shape[1] if has_past else 0
    tq = _tile(t_len, 256)
    tkp = _tile(past_len, 512) if has_past else 0
    assert tq % CHUNK == 0 and past_len % CHUNK == 0
    vec = lambda v: v.reshape(1, -1).astype(F32)
    in_specs = [pl.BlockSpec((1, tq, hd2), lambda b, h, i: (b, i, h)),
                pl.BlockSpec((1, t_len, hd2), lambda b, h, i: (b, 0, h)),
                pl.BlockSpec((1, t_len, hd2), lambda b, h, i: (b, 0, h))]
    args = [q, k_cur, v_cur]
    if has_past:
        in_specs += [pl.BlockSpec((1, past_len, hd2), lambda b, h, i: (b, 0, h)),
                     pl.BlockSpec((1, past_len, hd2), lambda b, h, i: (b, 0, h))]
        args += [k_past, v_past]
    in_specs += [pl.BlockSpec((1, HEAD_DIM), lambda b, h, i: (0, 0))] * 4
    in_specs += [pl.BlockSpec((1, hd2), lambda b, h, i: (0, 0))]
    args += [vec(lq1), vec(lk1), vec(lq2), vec(lk2), vec(subln_g)]
    return pl.pallas_call(
        functools.partial(_diff_attn_kernel, has_past=has_past, past_len=past_len, tq=tq, tkp=tkp, lam_init=lam_init),
        grid=(bsz, n_heads, t_len // tq),
        in_specs=in_specs,
        out_specs=pl.BlockSpec((1, tq, hd2), lambda b, h, i: (b, i, h)),
        out_shape=jax.ShapeDtypeStruct((bsz, t_len, d_attn), BF16),
        scratch_shapes=[pltpu.VMEM((2 * tq, hd2), F32), pltpu.VMEM((2 * tq, 1), F32), pltpu.VMEM((2 * tq, 1), F32)],
        compiler_params=_params("parallel", "parallel", "arbitrary"),
        name="diff_attention",
    )(*args)


def _mem_attn_kernel(q_ref, k_ref, v_ref, o_ref):
    scale = jnp.asarray(MEM_HEAD_DIM ** -0.5, q_ref.dtype)
    for h in range(MEM_HEADS):
        sl = slice(h * MEM_HEAD_DIM, (h + 1) * MEM_HEAD_DIM)
        q = q_ref[0, :, sl] * scale
        s = lax.dot_general(q, k_ref[0, :, sl].astype(BF16), (((1,), (1,)), ((), ())), preferred_element_type=F32)
        p = jnp.exp(s - jnp.max(s, axis=1, keepdims=True))
        p = p / jnp.sum(p, axis=1, keepdims=True)
        o = jnp.dot(p.astype(BF16), v_ref[0, :, sl].astype(BF16), preferred_element_type=F32)
        o_ref[0, :, sl] = o.astype(o_ref.dtype)


def _mem_attention(mq, mk, mv):
    bsz, t_len, d_mem = mq.shape
    n_mem = mk.shape[1]
    tt = _tile(t_len, 512)
    return pl.pallas_call(
        _mem_attn_kernel,
        grid=(bsz, t_len // tt),
        in_specs=[pl.BlockSpec((1, tt, d_mem), lambda b, t: (b, t, 0)),
                  pl.BlockSpec((1, n_mem, d_mem), lambda b, t: (b, 0, 0)),
                  pl.BlockSpec((1, n_mem, d_mem), lambda b, t: (b, 0, 0))],
        out_specs=pl.BlockSpec((1, tt, d_mem), lambda b, t: (b, t, 0)),
        out_shape=jax.ShapeDtypeStruct((bsz, t_len, d_mem), BF16),
        compiler_params=_params("parallel", "parallel"),
        name="mem_attention",
    )(mq, mk, mv)


def _router_kernel(x_ref, g_ref, whi_ref, wlo_ref, b_ref, hn_ref, eid_ref, wt_ref):
    x = x_ref[...]
    hn = x * lax.rsqrt(jnp.mean(x * x, axis=-1, keepdims=True) + EPS) * g_ref[...]
    hn_ref[...] = hn
    h_hi = hn.astype(BF16)
    h_lo = (hn - h_hi.astype(F32)).astype(BF16)
    logits = (jnp.dot(h_hi, whi_ref[...], preferred_element_type=F32)
              + jnp.dot(h_lo, whi_ref[...], preferred_element_type=F32)
              + jnp.dot(h_hi, wlo_ref[...], preferred_element_type=F32)) + b_ref[...]
    lane = lax.broadcasted_iota(jnp.int32, logits.shape, 1)
    first = lambda hit: jnp.min(jnp.where(hit, lane, LANES), axis=1, keepdims=True)

    gl = jnp.where(lane < N_GROUPS, logits, -jnp.inf)
    gmax = jnp.max(gl, axis=1, keepdims=True)
    g_idx = first(gl == gmax)
    p_g = 1.0 / jnp.sum(jnp.exp(gl - gmax), axis=1, keepdims=True)

    lo = N_GROUPS + EXPERTS_PER_GROUP * g_idx
    in_grp = (lane >= lo) & (lane < lo + EXPERTS_PER_GROUP)
    el = jnp.where(in_grp, logits, -jnp.inf)
    pe = jnp.exp(el - jnp.max(el, axis=1, keepdims=True))
    probs = jnp.where(in_grp, pe / jnp.sum(pe, axis=1, keepdims=True), -1.0)
    v1 = jnp.max(probs, axis=1, keepdims=True)
    i1 = first(probs == v1)
    probs2 = jnp.where(lane == i1, -1.0, probs)
    v2 = jnp.max(probs2, axis=1, keepdims=True)
    i2 = first(probs2 == v2)
    tot = v1 + v2
    eid_ref[:, 0:1] = i1 - N_GROUPS
    eid_ref[:, 1:2] = i2 - N_GROUPS
    wt_ref[:, 0:1] = p_g * (v1 / tot)
    wt_ref[:, 1:2] = p_g * (v2 / tot)


def _router(x, g, w_rg, b_rg, w_re, b_re):
    n, d = x.shape
    tr = _tile(n, 256)
    w = jnp.zeros((d, LANES), F32).at[:, :N_GROUPS].set(w_rg).at[:, N_GROUPS:N_GROUPS + N_EXPERTS].set(w_re)
    b = jnp.zeros((1, LANES), F32).at[0, :N_GROUPS].set(b_rg).at[0, N_GROUPS:N_GROUPS + N_EXPERTS].set(b_re)
    w_hi = w.astype(BF16)
    w_lo = (w - w_hi.astype(F32)).astype(BF16)
    return pl.pallas_call(
        _router_kernel,
        grid=(n // tr,),
        in_specs=[pl.BlockSpec((tr, d), lambda i: (i, 0)),
                  pl.BlockSpec((1, d), lambda i: (0, 0)),
                  pl.BlockSpec((d, LANES), lambda i: (0, 0)),
                  pl.BlockSpec((d, LANES), lambda i: (0, 0)),
                  pl.BlockSpec((1, LANES), lambda i: (0, 0))],
        out_specs=[pl.BlockSpec((tr, d), lambda i: (i, 0)),
                   pl.BlockSpec((tr, TOP_K), lambda i: (i, 0)),
                   pl.BlockSpec((tr, TOP_K), lambda i: (i, 0))],
        out_shape=[jax.ShapeDtypeStruct((n, d), F32),
                   jax.ShapeDtypeStruct((n, TOP_K), jnp.int32),
                   jax.ShapeDtypeStruct((n, TOP_K), F32)],
        compiler_params=_params("parallel"),
        name="moe_router",
    )(x, g.reshape(1, d).astype(F32), w_hi, w_lo, b)


def _row_gather_copy(src_hbm, dst_buf, sem, row, slot, r):
    return pltpu.make_async_copy(src_hbm.at[pl.ds(row, 1), :], dst_buf.at[slot, pl.ds(r, 1), :], sem.at[slot])


def _start_row_gather(idx_ref, base, src_hbm, dst_buf, sem, slot, rows):
    def body(r, carry):
        _row_gather_copy(src_hbm, dst_buf, sem, idx_ref[base + r], slot, r).start()
        return carry
    lax.fori_loop(0, rows, body, 0)


def _wait_row_gather(src_hbm, dst_buf, sem, slot, rows):
    pltpu.make_async_copy(src_hbm.at[pl.ds(0, rows), :], dst_buf.at[slot], sem.at[slot]).wait()


def _expert_kernel(tile_expert_ref, n_active_ref, row_token_ref, hn_hbm, wg_ref, wu_ref, wd_ref, roww_ref,
                   ys_ref, xbuf, sem, *, tm):
    t = pl.program_id(0)
    n_active = n_active_ref[0]
    slot = lax.rem(t, 2)

    @pl.when((t == 0) & (n_active > 0))
    def _():
        _start_row_gather(row_token_ref, 0, hn_hbm, xbuf, sem, 0, tm)

    @pl.when(t + 1 < n_active)
    def _():
        _start_row_gather(row_token_ref, (t + 1) * tm, hn_hbm, xbuf, sem, 1 - slot, tm)

    @pl.when(t < n_active)
    def _():
        _wait_row_gather(hn_hbm, xbuf, sem, slot, tm)
        x = xbuf[slot].astype(BF16)
        g = jnp.dot(x, wg_ref[0], preferred_element_type=F32)
        u = jnp.dot(x, wu_ref[0], preferred_element_type=F32)
        act = (g * jax.nn.sigmoid(g) * u).astype(BF16)
        y = jnp.dot(act, wd_ref[0], preferred_element_type=F32)
        ys_ref[...] = y * roww_ref[...]

    @pl.when(t >= n_active)
    def _():
        ys_ref[...] = jnp.zeros(ys_ref.shape, ys_ref.dtype)


def _combine_kernel(pos_ref, x_ref, g_ref, ys_hbm, o_ref, ybuf, sem, *, tc):
    t = pl.program_id(0)
    nt = pl.num_programs(0)
    slot = lax.rem(t, 2)
    rows = TOP_K * tc

    @pl.when(t == 0)
    def _():
        _start_row_gather(pos_ref, 0, ys_hbm, ybuf, sem, 0, rows)

    @pl.when(t + 1 < nt)
    def _():
        _start_row_gather(pos_ref, (t + 1) * rows, ys_hbm, ybuf, sem, 1 - slot, rows)

    _wait_row_gather(ys_hbm, ybuf, sem, slot, rows)
    y = x_ref[...]
    for kk in range(TOP_K):
        y = y + ybuf[slot, pl.ds(kk * tc, tc), :]
    o_ref[...] = y * lax.rsqrt(jnp.mean(y * y, axis=-1, keepdims=True) + EPS) * g_ref[...]


def _hier_moe_final(x, norm_ffn, w_rg, b_rg, w_re, b_re, w_g, w_u, w_d, norm_final):
    n, d = x.shape
    d_exp = w_g.shape[2]
    hn, eid, wts = _router(x, norm_ffn, w_rg, b_rg, w_re, b_re)

    tm = 256
    na = n * TOP_K
    n_rows = na + N_EXPERTS * tm
    n_tiles = n_rows // tm
    flat_e = eid.reshape(na)
    onehot = (flat_e[:, None] == jnp.arange(N_EXPERTS, dtype=jnp.int32)[None, :]).astype(jnp.int32)
    csum = jnp.cumsum(onehot, axis=0)
    rank = jnp.sum(onehot * (csum - 1), axis=1)
    counts = csum[-1]
    padded = ((counts + tm - 1) // tm) * tm
    pend = jnp.cumsum(padded)
    pos = (pend - padded)[flat_e] + rank
    row_token = jnp.zeros((n_rows,), jnp.int32).at[pos].set(jnp.arange(na, dtype=jnp.int32) // TOP_K)
    row_w = jnp.zeros((n_rows,), F32).at[pos].set(wts.reshape(na)).reshape(n_rows, 1)
    tile_expert = jnp.minimum(
        jnp.searchsorted(pend, jnp.arange(n_tiles, dtype=jnp.int32) * tm, side="right"), N_EXPERTS - 1).astype(jnp.int32)
    n_active = (pend[-1] // tm).astype(jnp.int32).reshape(1)

    ys = pl.pallas_call(
        functools.partial(_expert_kernel, tm=tm),
        grid_spec=pltpu.PrefetchScalarGridSpec(
            num_scalar_prefetch=3,
            grid=(n_tiles,),
            in_specs=[pl.BlockSpec(memory_space=pl.ANY),
                      pl.BlockSpec((1, d, d_exp), lambda t, te, na_, rt: (te[t], 0, 0)),
                      pl.BlockSpec((1, d, d_exp), lambda t, te, na_, rt: (te[t], 0, 0)),
                      pl.BlockSpec((1, d_exp, d), lambda t, te, na_, rt: (te[t], 0, 0)),
                      pl.BlockSpec((tm, 1), lambda t, te, na_, rt: (t, 0))],
            out_specs=pl.BlockSpec((tm, d), lambda t, te, na_, rt: (t, 0)),
            scratch_shapes=[pltpu.VMEM((2, tm, d), F32), pltpu.SemaphoreType.DMA((2,))]),
        out_shape=jax.ShapeDtypeStruct((n_rows, d), F32),
        compiler_params=_params("arbitrary"),
        name="moe_experts",
    )(tile_expert, n_active, row_token, hn, w_g, w_u, w_d, row_w)

    tc = _tile(n, 256)
    pos_tiled = pos.reshape(n // tc, tc, TOP_K).transpose(0, 2, 1).reshape(na)
    return pl.pallas_call(
        functools.partial(_combine_kernel, tc=tc),
        grid_spec=pltpu.PrefetchScalarGridSpec(
            num_scalar_prefetch=1,
            grid=(n // tc,),
            in_specs=[pl.BlockSpec((tc, d), lambda t, p: (t, 0)),
                      pl.BlockSpec((1, d), lambda t, p: (0, 0)),
                      pl.BlockSpec(memory_space=pl.ANY)],
            out_specs=pl.BlockSpec((tc, d), lambda t, p: (t, 0)),
            scratch_shapes=[pltpu.VMEM((2, TOP_K * tc, d), F32), pltpu.SemaphoreType.DMA((2,))]),
        out_shape=jax.ShapeDtypeStruct((n, d), F32),
        compiler_params=_params("arbitrary"),
        name="moe_combine",
    )(pos_tiled, x, norm_final.reshape(1, d).astype(F32), ys)


def _layer(x, conv_buf, k_past, v_past, mk, mv, lam_init, w):
    bsz, t_len, d = x.shape
    n = bsz * t_len
    d_conv = w["w_dw"].shape[1]
    d_attn = w["w_attn_out"].shape[0]
    d_mem = w["w_mem_out"].shape[0]
    c0 = 2 * d_conv
    c1, c2, c3 = c0 + d_attn, c0 + 2 * d_attn, c0 + 3 * d_attn
    c4 = c3 + d_mem
    x2d = x.reshape(n, d)

    h = _rmsnorm(x2d, w["norm_mix"], BF16)
    w_in = w["w_in"]
    u_in = _matmul(h, w_in, F32, b_col0=0, n_out=c0)
    q = _matmul(h, w_in, BF16, b_col0=c0, n_out=d_attn)
    k_rows = _matmul(h, w_in, F32, b_col0=c1, n_out=d_attn)
    v_rows = _matmul(h, w_in, F32, b_col0=c2, n_out=d_attn)
    mq = _matmul(h, w_in, BF16, b_col0=c3, n_out=d_mem)
    gates = _matmul(h, w_in, BF16, b_col0=c4, n_out=N_BRANCH * d)

    conv_act, new_buf = _conv_branch(u_in.reshape(bsz, t_len, c0), conv_buf, w["w_dw"], w["b_dw"],
                                     w["conv_ln_g"], w["conv_ln_b"])
    o = _diff_attention(q.reshape(bsz, t_len, d_attn), k_rows.reshape(bsz, t_len, d_attn),
                        v_rows.reshape(bsz, t_len, d_attn), k_past, v_past,
                        w["lambda_q1"], w["lambda_k1"], w["lambda_q2"], w["lambda_k2"], w["subln_g"], lam_init)
    mo = _mem_attention(mq.reshape(bsz, t_len, d_mem), mk, mv)

    merged = _matmul(conv_act.reshape(n, d_conv), w["w_conv_out"], BF16, gate=gates, gate_col0=0)
    merged = _matmul(o.reshape(n, d_attn), w["w_attn_out"], BF16, gate=gates, gate_col0=d, add=merged)
    merged = _matmul(mo.reshape(n, d_mem), w["w_mem_out"], BF16, gate=gates, gate_col0=2 * d, add=merged)
    x2 = _matmul(merged, w["w_out"], F32, add=x2d)

    y = _hier_moe_final(x2, w["norm_ffn"], w["w_router_grp"], w["b_router_grp"], w["w_router_exp"],
                        w["b_router_exp"], w["w_exp_gate"], w["w_exp_up"], w["w_exp_down"], w["norm_final"])
    hd2 = 2 * HEAD_DIM
    return (y.reshape(bsz, t_len, d), new_buf, k_rows.reshape(bsz, t_len, N_HEADS, hd2),
            v_rows.reshape(bsz, t_len, N_HEADS, hd2))


def kernel(x_prompt, x_sample, mem_prompt, cache_conv, cache_diff_k, cache_diff_v, cache_mem_k, cache_mem_v, norm_mix, w_in, w_dw, b_dw, conv_ln_g, conv_ln_b, w_conv_out, lambda_q1, lambda_k1, lambda_q2, lambda_k2, subln_g, w_attn_out, norm_mem, w_mem_k, w_mem_v, w_mem_out, w_out, norm_ffn, w_router_grp, b_router_grp, w_router_exp, b_router_exp, w_exp_gate, w_exp_up, w_exp_down, norm_final):
    depth = w_in.shape[0]
    assert depth == 1, "the final rmsnorm is fused into the last layer's MoE combine"
    bp, _, d = x_prompt.shape
    n_mem = mem_prompt.shape[1]
    d_conv = w_dw.shape[2]
    d_mem = w_mem_k.shape[2]
    l = 0
    lam_init = 0.8 - 0.6 * math.exp(-0.3 * l)
    w = dict(norm_mix=norm_mix[l], w_in=w_in[l].astype(BF16), w_dw=w_dw[l], b_dw=b_dw[l], conv_ln_g=conv_ln_g[l],
             conv_ln_b=conv_ln_b[l], w_conv_out=w_conv_out[l].astype(BF16), lambda_q1=lambda_q1[l],
             lambda_k1=lambda_k1[l], lambda_q2=lambda_q2[l], lambda_k2=lambda_k2[l], subln_g=subln_g[l],
             w_attn_out=w_attn_out[l].astype(BF16), w_mem_out=w_mem_out[l].astype(BF16), w_out=w_out[l].astype(BF16),
             norm_ffn=norm_ffn[l], w_router_grp=w_router_grp[l], b_router_grp=b_router_grp[l],
             w_router_exp=w_router_exp[l], b_router_exp=b_router_exp[l], w_exp_gate=w_exp_gate[l].astype(BF16),
             w_exp_up=w_exp_up[l].astype(BF16), w_exp_down=w_exp_down[l].astype(BF16), norm_final=norm_final)

    hm = _rmsnorm(mem_prompt.reshape(bp * n_mem, d), norm_mem[l], BF16)
    mk = _matmul(hm, w_mem_k[l].astype(BF16), F32).reshape(bp, n_mem, d_mem)
    mv = _matmul(hm, w_mem_v[l].astype(BF16), F32).reshape(bp, n_mem, d_mem)
    conv0 = jnp.zeros((bp, CONV_W - 1, d_conv), F32)
    y_p, cb_p, k_p, v_p = _layer(x_prompt, conv0, None, None, mk, mv, lam_init, w)

    bs, past_len = cache_diff_k.shape[1], cache_diff_k.shape[2]
    d_attn = N_HEADS * 2 * HEAD_DIM
    y_s, cb_s, k_s, v_s = _layer(x_sample, cache_conv[l],
                                 cache_diff_k[l].reshape(bs, past_len, d_attn),
                                 cache_diff_v[l].reshape(bs, past_len, d_attn),
                                 cache_mem_k[l].reshape(bs, n_mem, d_mem), cache_mem_v[l].reshape(bs, n_mem, d_mem),
                                 lam_init, w)
    mem_shape = (1, bp, n_mem, MEM_HEADS, MEM_HEAD_DIM)
    return (y_p, y_s, cb_p[None], k_p[None], v_p[None], mk.reshape(mem_shape), mv.reshape(mem_shape),
            cb_s[None], k_s[None], v_s[None])
```

```python
import functools
import math

import jax
import jax.numpy as jnp
from jax import lax
from jax.experimental import pallas as pl
from jax.experimental.pallas import tpu as pltpu

F32 = jnp.float32
BF16 = jnp.bfloat16

CHUNK = 64
EPS = 1e-6
SUBLN_EPS = 1e-5
NEG = -1e30
CONV_W = 31
N_HEADS = 16
HEAD_DIM = 64
MEM_HEADS = 4
MEM_HEAD_DIM = 256
N_BRANCH = 3
N_GROUPS = 4
EXPERTS_PER_GROUP = 8
N_EXPERTS = N_GROUPS * EXPERTS_PER_GROUP
TOP_K = 2

LANES = 128
SUBLANES = 8
VMEM_LIMIT_BYTES = 56 * 1024 * 1024
HALO = 32


def _params(*semantics):
    return pltpu.CompilerParams(dimension_semantics=semantics, vmem_limit_bytes=VMEM_LIMIT_BYTES)


def _tile(n, pref):
    if n <= pref:
        return n
    t = pref
    while n % t:
        t //= 2
    return t


def _rmsnorm_kernel(x_ref, g_ref, o_ref, *, eps):
    x = x_ref[...].astype(F32)
    ms = jnp.mean(x * x, axis=-1, keepdims=True)
    o_ref[...] = (x * lax.rsqrt(ms + eps) * g_ref[...]).astype(o_ref.dtype)


def _rmsnorm(x, g, out_dtype, eps=EPS):
    n, d = x.shape
    tr = _tile(n, 512)
    return pl.pallas_call(
        functools.partial(_rmsnorm_kernel, eps=eps),
        grid=(n // tr,),
        in_specs=[pl.BlockSpec((tr, d), lambda i: (i, 0)), pl.BlockSpec((1, d), lambda i: (0, 0))],
        out_specs=pl.BlockSpec((tr, d), lambda i: (i, 0)),
        out_shape=jax.ShapeDtypeStruct((n, d), out_dtype),
        compiler_params=_params("parallel"),
        name="rmsnorm",
    )(x, g.reshape(1, d).astype(F32))


def _mm_kernel(*refs, has_gate, has_add):
    a_ref, b_ref = refs[0], refs[1]
    o_ref = refs[-1]
    acc = jnp.dot(a_ref[...], b_ref[...], preferred_element_type=F32)
    pos = 2
    if has_gate:
        acc = acc * jax.nn.sigmoid(refs[pos][...].astype(F32))
        pos += 1
    if has_add:
        acc = acc + refs[pos][...].astype(F32)
    o_ref[...] = acc.astype(o_ref.dtype)


def _matmul(a, b, out_dtype, *, b_col0=0, n_out=None, gate=None, gate_col0=0, add=None):
    m, k = a.shape
    n_out = b.shape[1] if n_out is None else n_out
    tm = _tile(m, 512)
    tn = _tile(n_out, 1024)
    assert b_col0 % tn == 0 and gate_col0 % tn == 0
    bj0, gj0 = b_col0 // tn, gate_col0 // tn
    in_specs = [pl.BlockSpec((tm, k), lambda j, i: (i, 0)),
                pl.BlockSpec((k, tn), lambda j, i: (0, bj0 + j))]
    args = [a, b]
    if gate is not None:
        in_specs.append(pl.BlockSpec((tm, tn), lambda j, i: (i, gj0 + j)))
        args.append(gate)
    if add is not None:
        in_specs.append(pl.BlockSpec((tm, tn), lambda j, i: (i, j)))
        args.append(add)
    return pl.pallas_call(
        functools.partial(_mm_kernel, has_gate=gate is not None, has_add=add is not None),
        grid=(n_out // tn, m // tm),
        in_specs=in_specs,
        out_specs=pl.BlockSpec((tm, tn), lambda j, i: (i, j)),
        out_shape=jax.ShapeDtypeStruct((m, n_out), out_dtype),
        compiler_params=_params("parallel", "parallel"),
        name="matmul",
    )(*args)


def _conv_kernel(u_ref, buf_ref, w_ref, bdw_ref, g_ref, b_ref, act_ref, newbuf_ref, ext_ref, *, tt, dc):
    t = pl.program_id(1)
    hist = CONV_W - 1

    @pl.when(t == 0)
    def _():
        ext_ref[pl.ds(HALO - hist, hist), :] = buf_ref[0].astype(F32)

    @pl.when(t > 0)
    def _():
        ext_ref[pl.ds(0, HALO), :] = ext_ref[pl.ds(tt, HALO), :]

    a = u_ref[0, :, :dc].astype(F32)
    gt = u_ref[0, :, dc:].astype(F32)
    ext_ref[pl.ds(HALO, tt), :] = a * jax.nn.sigmoid(gt)

    y = jnp.zeros((tt, dc), F32) + bdw_ref[...]
    for j in range(CONV_W):
        y = y + ext_ref[pl.ds(HALO - hist + j, tt), :] * w_ref[pl.ds(j, 1), :]
    mu = jnp.mean(y, axis=-1, keepdims=True)
    var = jnp.mean(jnp.square(y - mu), axis=-1, keepdims=True)
    z = (y - mu) * lax.rsqrt(var + EPS) * g_ref[...] + b_ref[...]
    act_ref[0] = (z * jax.nn.sigmoid(z)).astype(act_ref.dtype)
    newbuf_ref[0] = ext_ref[pl.ds(HALO + tt - hist, hist), :]


def _conv_branch(u_in, conv_buf, w_dw, b_dw, ln_g, ln_b):
    bsz, t_len, two_dc = u_in.shape
    dc = two_dc // 2
    tt = _tile(t_len, 256)
    assert tt >= HALO and tt % SUBLANES == 0
    vec = lambda v: v.reshape(1, dc).astype(F32)
    return pl.pallas_call(
        functools.partial(_conv_kernel, tt=tt, dc=dc),
        grid=(bsz, t_len // tt),
        in_specs=[pl.BlockSpec((1, tt, two_dc), lambda b, t: (b, t, 0)),
                  pl.BlockSpec((1, CONV_W - 1, dc), lambda b, t: (b, 0, 0)),
                  pl.BlockSpec((CONV_W, dc), lambda b, t: (0, 0)),
                  pl.BlockSpec((1, dc), lambda b, t: (0, 0)),
                  pl.BlockSpec((1, dc), lambda b, t: (0, 0)),
                  pl.BlockSpec((1, dc), lambda b, t: (0, 0))],
        out_specs=[pl.BlockSpec((1, tt, dc), lambda b, t: (b, t, 0)),
                   pl.BlockSpec((1, CONV_W - 1, dc), lambda b, t: (b, 0, 0))],
        out_shape=[jax.ShapeDtypeStruct((bsz, t_len, dc), BF16),
                   jax.ShapeDtypeStruct((bsz, CONV_W - 1, dc), F32)],
        scratch_shapes=[pltpu.VMEM((HALO + tt, dc), F32)],
        compiler_params=_params("parallel", "arbitrary"),
        name="conv_branch",
    )(u_in, conv_buf, w_dw.astype(F32), vec(b_dw), vec(ln_g), vec(ln_b))


ONES_ROWS = 16
ATTN_TQ = 1024
ATTN_TK = 512


def _diff_attn_kernel(*refs, has_past, n_past_blk, n_cur_blk, tq, tkc, tkp, lam_init):
    if has_past:
        (q_ref, kc_ref, vc_ref, kp_ref, vp_ref, lq1_ref, lk1_ref, lq2_ref, lk2_ref, g_ref,
         o_ref, kc_scr, vct_scr, kp_scr, vpt_scr, acc_ref) = refs
    else:
        (q_ref, kc_ref, vc_ref, lq1_ref, lk1_ref, lq2_ref, lk2_ref, g_ref,
         o_ref, kc_scr, vct_scr, acc_ref) = refs
    qi = pl.program_id(2)
    hd2 = 2 * HEAD_DIM
    nq = 2 * tq

    def fill(k_src, v_src, k_scr, vt_scr, n_blk, blk):
        ones = jnp.ones((ONES_ROWS, blk), BF16)

        def body(i, carry):
            off = pl.multiple_of(i * blk, blk)
            k_scr[i] = k_src[0, pl.ds(off, blk), :].astype(BF16)
            vt = v_src[0, pl.ds(off, blk), :].astype(F32).T.astype(BF16)
            vt_scr[i] = jnp.concatenate([vt, ones], axis=0)
            return carry
        lax.fori_loop(0, n_blk, body, 0)

    @pl.when(qi == 0)
    def _():
        fill(kc_ref, vc_ref, kc_scr, vct_scr, n_cur_blk, tkc)
        if has_past:
            fill(kp_ref, vp_ref, kp_scr, vpt_scr, n_past_blk, tkp)

    qt = (q_ref[0].astype(F32) * (HEAD_DIM ** -0.5)).T
    z = jnp.zeros((HEAD_DIM, tq), F32)
    qqt = jnp.concatenate([jnp.concatenate([qt[:HEAD_DIM], z], axis=0),
                           jnp.concatenate([z, qt[HEAD_DIM:]], axis=0)], axis=1).astype(BF16)

    acc_ref[...] = jnp.zeros(acc_ref.shape, F32)

    def step(m_prev, k_blk, vt_blk, mask):
        s = jnp.dot(k_blk, qqt, preferred_element_type=F32)
        if mask is not None:
            s = jnp.where(mask, s, NEG)
        m_new = jnp.maximum(m_prev, jnp.max(s, axis=0, keepdims=True))
        alpha = jnp.exp(m_prev - m_new)
        p = jnp.exp(s - m_new).astype(BF16)
        acc_ref[...] = alpha * acc_ref[...] + jnp.dot(vt_blk, p, preferred_element_type=F32)
        return m_new

    m = jnp.full((1, nq), NEG, F32)
    if has_past:
        m = lax.fori_loop(0, n_past_blk, lambda i, mm: step(mm, kp_scr[i], vpt_scr[i], None), m)
    blk_per_q = tq // tkc
    m = lax.fori_loop(0, qi * blk_per_q, lambda kb, mm: step(mm, kc_scr[kb], vct_scr[kb], None), m)
    q_chunk = lax.rem(lax.broadcasted_iota(jnp.int32, (tkc, nq), 1), tq) // CHUNK
    for d in range(blk_per_q):
        k_chunk = (lax.broadcasted_iota(jnp.int32, (tkc, nq), 0) + d * tkc) // CHUNK
        kb = qi * blk_per_q + d
        m = step(m, kc_scr[kb], vct_scr[kb], k_chunk <= q_chunk)

    lam = (jnp.exp(jnp.sum(lq1_ref[...] * lk1_ref[...], axis=1, keepdims=True))
           - jnp.exp(jnp.sum(lq2_ref[...] * lk2_ref[...], axis=1, keepdims=True)) + lam_init)
    acc = acc_ref[...]
    inv_l = 1.0 / acc[hd2:hd2 + 1, :]
    ot = acc[:hd2, :tq] * inv_l[:, :tq] - lam * (acc[:hd2, tq:] * inv_l[:, tq:])
    ot = ot * lax.rsqrt(jnp.mean(ot * ot, axis=0, keepdims=True) + SUBLN_EPS) * g_ref[...]
    o_ref[0] = (ot * (1.0 - lam_init)).T.astype(o_ref.dtype)


def _diff_attention(q, k_cur, v_cur, k_past, v_past, lq1, lk1, lq2, lk2, subln_g, lam_init):
    bsz, t_len, d_attn = q.shape
    hd2 = 2 * HEAD_DIM
    n_heads = d_attn // hd2
    has_past = k_past is not None
    past_len = k_past.shape[1] if has_past else 0
    tq = _tile(t_len, ATTN_TQ)
    tkc = _tile(tq, ATTN_TK)
    tkp = _tile(past_len, max(ATTN_TK, ATTN_TK * ATTN_TQ // tq)) if has_past else 0
    n_cur_blk = t_len // tkc
    n_past_blk = past_len // tkp if has_past else 0
    assert tkc % CHUNK == 0 and past_len % CHUNK == 0
    vec = lambda v: v.reshape(1, -1).astype(F32)
    in_specs = [pl.BlockSpec((1, tq, hd2), lambda b, h, i: (b, i, h)),
                pl.BlockSpec((1, t_len, hd2), lambda b, h, i: (b, 0, h)),
                pl.BlockSpec((1, t_len, hd2), lambda b, h, i: (b, 0, h))]
    args = [q, k_cur, v_cur]
    scratch = [pltpu.VMEM((n_cur_blk, tkc, hd2), BF16), pltpu.VMEM((n_cur_blk, hd2 + ONES_ROWS, tkc), BF16)]
    if has_past:
        in_specs += [pl.BlockSpec((1, past_len, hd2), lambda b, h, i: (b, 0, h)),
                     pl.BlockSpec((1, past_len, hd2), lambda b, h, i: (b, 0, h))]
        args += [k_past, v_past]
        scratch += [pltpu.VMEM((n_past_blk, tkp, hd2), BF16), pltpu.VMEM((n_past_blk, hd2 + ONES_ROWS, tkp), BF16)]
    in_specs += [pl.BlockSpec((1, HEAD_DIM), lambda b, h, i: (0, 0))] * 4
    in_specs += [pl.BlockSpec((hd2, 1), lambda b, h, i: (0, 0))]
    args += [vec(lq1), vec(lk1), vec(lq2), vec(lk2), subln_g.reshape(hd2, 1).astype(F32)]
    scratch += [pltpu.VMEM((hd2 + ONES_ROWS, 2 * tq), F32)]
    return pl.pallas_call(
        functools.partial(_diff_attn_kernel, has_past=has_past, n_past_blk=n_past_blk, n_cur_blk=n_cur_blk,
                          tq=tq, tkc=tkc, tkp=tkp, lam_init=lam_init),
        grid=(bsz, n_heads, t_len // tq),
        in_specs=in_specs,
        out_specs=pl.BlockSpec((1, tq, hd2), lambda b, h, i: (b, i, h)),
        out_shape=jax.ShapeDtypeStruct((bsz, t_len, d_attn), BF16),
        scratch_shapes=scratch,
        compiler_params=_params("parallel", "parallel", "arbitrary"),
        name="diff_attention",
    )(*args)


def _mem_attn_kernel(q_ref, k_ref, v_ref, o_ref):
    scale = jnp.asarray(MEM_HEAD_DIM ** -0.5, q_ref.dtype)
    for h in range(MEM_HEADS):
        sl = slice(h * MEM_HEAD_DIM, (h + 1) * MEM_HEAD_DIM)
        q = q_ref[0, :, sl] * scale
        s = lax.dot_general(q, k_ref[0, :, sl].astype(BF16), (((1,), (1,)), ((), ())), preferred_element_type=F32)
        p = jnp.exp(s - jnp.max(s, axis=1, keepdims=True))
        p = p / jnp.sum(p, axis=1, keepdims=True)
        o = jnp.dot(p.astype(BF16), v_ref[0, :, sl].astype(BF16), preferred_element_type=F32)
        o_ref[0, :, sl] = o.astype(o_ref.dtype)


def _mem_attention(mq, mk, mv):
    bsz, t_len, d_mem = mq.shape
    n_mem = mk.shape[1]
    tt = _tile(t_len, 512)
    return pl.pallas_call(
        _mem_attn_kernel,
        grid=(bsz, t_len // tt),
        in_specs=[pl.BlockSpec((1, tt, d_mem), lambda b, t: (b, t, 0)),
                  pl.BlockSpec((1, n_mem, d_mem), lambda b, t: (b, 0, 0)),
                  pl.BlockSpec((1, n_mem, d_mem), lambda b, t: (b, 0, 0))],
        out_specs=pl.BlockSpec((1, tt, d_mem), lambda b, t: (b, t, 0)),
        out_shape=jax.ShapeDtypeStruct((bsz, t_len, d_mem), BF16),
        compiler_params=_params("parallel", "parallel"),
        name="mem_attention",
    )(mq, mk, mv)


def _router_kernel(x_ref, g_ref, whi_ref, wlo_ref, b_ref, hn_ref, eid_ref, wt_ref):
    x = x_ref[...]
    hn = x * lax.rsqrt(jnp.mean(x * x, axis=-1, keepdims=True) + EPS) * g_ref[...]
    hn_ref[...] = hn
    h_hi = hn.astype(BF16)
    h_lo = (hn - h_hi.astype(F32)).astype(BF16)
    logits = (jnp.dot(h_hi, whi_ref[...], preferred_element_type=F32)
              + jnp.dot(h_lo, whi_ref[...], preferred_element_type=F32)
              + jnp.dot(h_hi, wlo_ref[...], preferred_element_type=F32)) + b_ref[...]
    lane = lax.broadcasted_iota(jnp.int32, logits.shape, 1)
    first = lambda hit: jnp.min(jnp.where(hit, lane, LANES), axis=1, keepdims=True)

    gl = jnp.where(lane < N_GROUPS, logits, -jnp.inf)
    gmax = jnp.max(gl, axis=1, keepdims=True)
    g_idx = first(gl == gmax)
    p_g = 1.0 / jnp.sum(jnp.exp(gl - gmax), axis=1, keepdims=True)

    lo = N_GROUPS + EXPERTS_PER_GROUP * g_idx
    in_grp = (lane >= lo) & (lane < lo + EXPERTS_PER_GROUP)
    el = jnp.where(in_grp, logits, -jnp.inf)
    pe = jnp.exp(el - jnp.max(el, axis=1, keepdims=True))
    probs = jnp.where(in_grp, pe / jnp.sum(pe, axis=1, keepdims=True), -1.0)
    v1 = jnp.max(probs, axis=1, keepdims=True)
    i1 = first(probs == v1)
    probs2 = jnp.where(lane == i1, -1.0, probs)
    v2 = jnp.max(probs2, axis=1, keepdims=True)
    i2 = first(probs2 == v2)
    tot = v1 + v2
    eid_ref[:, 0:1] = i1 - N_GROUPS
    eid_ref[:, 1:2] = i2 - N_GROUPS
    wt_ref[:, 0:1] = p_g * (v1 / tot)
    wt_ref[:, 1:2] = p_g * (v2 / tot)


def _router(x, g, w_rg, b_rg, w_re, b_re):
    n, d = x.shape
    tr = _tile(n, 256)
    w = jnp.zeros((d, LANES), F32).at[:, :N_GROUPS].set(w_rg).at[:, N_GROUPS:N_GROUPS + N_EXPERTS].set(w_re)
    b = jnp.zeros((1, LANES), F32).at[0, :N_GROUPS].set(b_rg).at[0, N_GROUPS:N_GROUPS + N_EXPERTS].set(b_re)
    w_hi = w.astype(BF16)
    w_lo = (w - w_hi.astype(F32)).astype(BF16)
    return pl.pallas_call(
        _router_kernel,
        grid=(n // tr,),
        in_specs=[pl.BlockSpec((tr, d), lambda i: (i, 0)),
                  pl.BlockSpec((1, d), lambda i: (0, 0)),
                  pl.BlockSpec((d, LANES), lambda i: (0, 0)),
                  pl.BlockSpec((d, LANES), lambda i: (0, 0)),
                  pl.BlockSpec((1, LANES), lambda i: (0, 0))],
        out_specs=[pl.BlockSpec((tr, d), lambda i: (i, 0)),
                   pl.BlockSpec((tr, TOP_K), lambda i: (i, 0)),
                   pl.BlockSpec((tr, TOP_K), lambda i: (i, 0))],
        out_shape=[jax.ShapeDtypeStruct((n, d), F32),
                   jax.ShapeDtypeStruct((n, TOP_K), jnp.int32),
                   jax.ShapeDtypeStruct((n, TOP_K), F32)],
        compiler_params=_params("parallel"),
        name="moe_router",
    )(x, g.reshape(1, d).astype(F32), w_hi, w_lo, b)


def _row_gather_copy(src_hbm, dst_buf, sem, row, slot, r):
    return pltpu.make_async_copy(src_hbm.at[pl.ds(row, 1), :], dst_buf.at[slot, pl.ds(r, 1), :], sem.at[slot])


def _start_row_gather(idx_ref, base, src_hbm, dst_buf, sem, slot, rows):
    def body(r, carry):
        _row_gather_copy(src_hbm, dst_buf, sem, idx_ref[base + r], slot, r).start()
        return carry
    lax.fori_loop(0, rows, body, 0)


def _wait_row_gather(src_hbm, dst_buf, sem, slot, rows):
    pltpu.make_async_copy(src_hbm.at[pl.ds(0, rows), :], dst_buf.at[slot], sem.at[slot]).wait()


def _expert_kernel(tile_expert_ref, n_active_ref, row_token_ref, hn_hbm, wg_ref, wu_ref, wd_ref, roww_ref,
                   ys_ref, xbuf, sem, *, tm):
    t = pl.program_id(0)
    n_active = n_active_ref[0]
    slot = lax.rem(t, 2)

    @pl.when((t == 0) & (n_active > 0))
    def _():
        _start_row_gather(row_token_ref, 0, hn_hbm, xbuf, sem, 0, tm)

    @pl.when(t + 1 < n_active)
    def _():
        _start_row_gather(row_token_ref, (t + 1) * tm, hn_hbm, xbuf, sem, 1 - slot, tm)

    @pl.when(t < n_active)
    def _():
        _wait_row_gather(hn_hbm, xbuf, sem, slot, tm)
        x = xbuf[slot].astype(BF16)
        g = jnp.dot(x, wg_ref[0], preferred_element_type=F32)
        u = jnp.dot(x, wu_ref[0], preferred_element_type=F32)
        act = (g * jax.nn.sigmoid(g) * u).astype(BF16)
        y = jnp.dot(act, wd_ref[0], preferred_element_type=F32)
        ys_ref[...] = y * roww_ref[...]

    @pl.when(t >= n_active)
    def _():
        ys_ref[...] = jnp.zeros(ys_ref.shape, ys_ref.dtype)


def _combine_kernel(pos_ref, x_ref, g_ref, ys_hbm, o_ref, ybuf, sem, *, tc):
    t = pl.program_id(0)
    nt = pl.num_programs(0)
    slot = lax.rem(t, 2)
    rows = TOP_K * tc

    @pl.when(t == 0)
    def _():
        _start_row_gather(pos_ref, 0, ys_hbm, ybuf, sem, 0, rows)

    @pl.when(t + 1 < nt)
    def _():
        _start_row_gather(pos_ref, (t + 1) * rows, ys_hbm, ybuf, sem, 1 - slot, rows)

    _wait_row_gather(ys_hbm, ybuf, sem, slot, rows)
    y = x_ref[...]
    for kk in range(TOP_K):
        y = y + ybuf[slot, pl.ds(kk * tc, tc), :]
    o_ref[...] = y * lax.rsqrt(jnp.mean(y * y, axis=-1, keepdims=True) + EPS) * g_ref[...]


def _hier_moe_final(x, norm_ffn, w_rg, b_rg, w_re, b_re, w_g, w_u, w_d, norm_final):
    n, d = x.shape
    d_exp = w_g.shape[2]
    hn, eid, wts = _router(x, norm_ffn, w_rg, b_rg, w_re, b_re)

    tm = 256
    na = n * TOP_K
    n_rows = na + N_EXPERTS * tm
    n_tiles = n_rows // tm
    flat_e = eid.reshape(na)
    onehot = (flat_e[:, None] == jnp.arange(N_EXPERTS, dtype=jnp.int32)[None, :]).astype(jnp.int32)
    csum = jnp.cumsum(onehot, axis=0)
    rank = jnp.sum(onehot * (csum - 1), axis=1)
    counts = csum[-1]
    padded = ((counts + tm - 1) // tm) * tm
    pend = jnp.cumsum(padded)
    pos = (pend - padded)[flat_e] + rank
    row_token = jnp.zeros((n_rows,), jnp.int32).at[pos].set(jnp.arange(na, dtype=jnp.int32) // TOP_K)
    row_w = jnp.zeros((n_rows,), F32).at[pos].set(wts.reshape(na)).reshape(n_rows, 1)
    tile_start = jnp.arange(n_tiles, dtype=jnp.int32) * tm
    tile_expert = jnp.minimum(jnp.sum((pend[None, :] <= tile_start[:, None]).astype(jnp.int32), axis=1), N_EXPERTS - 1)
    n_active = (pend[-1] // tm).astype(jnp.int32).reshape(1)

    ys = pl.pallas_call(
        functools.partial(_expert_kernel, tm=tm),
        grid_spec=pltpu.PrefetchScalarGridSpec(
            num_scalar_prefetch=3,
            grid=(n_tiles,),
            in_specs=[pl.BlockSpec(memory_space=pl.ANY),
                      pl.BlockSpec((1, d, d_exp), lambda t, te, na_, rt: (te[t], 0, 0)),
                      pl.BlockSpec((1, d, d_exp), lambda t, te, na_, rt: (te[t], 0, 0)),
                      pl.BlockSpec((1, d_exp, d), lambda t, te, na_, rt: (te[t], 0, 0)),
                      pl.BlockSpec((tm, 1), lambda t, te, na_, rt: (t, 0))],
            out_specs=pl.BlockSpec((tm, d), lambda t, te, na_, rt: (t, 0)),
            scratch_shapes=[pltpu.VMEM((2, tm, d), F32), pltpu.SemaphoreType.DMA((2,))]),
        out_shape=jax.ShapeDtypeStruct((n_rows, d), F32),
        compiler_params=_params("arbitrary"),
        name="moe_experts",
    )(tile_expert, n_active, row_token, hn, w_g, w_u, w_d, row_w)

    tc = _tile(n, 256)
    pos_tiled = pos.reshape(n // tc, tc, TOP_K).transpose(0, 2, 1).reshape(na)
    return pl.pallas_call(
        functools.partial(_combine_kernel, tc=tc),
        grid_spec=pltpu.PrefetchScalarGridSpec(
            num_scalar_prefetch=1,
            grid=(n // tc,),
            in_specs=[pl.BlockSpec((tc, d), lambda t, p: (t, 0)),
                      pl.BlockSpec((1, d), lambda t, p: (0, 0)),
                      pl.BlockSpec(memory_space=pl.ANY)],
            out_specs=pl.BlockSpec((tc, d), lambda t, p: (t, 0)),
            scratch_shapes=[pltpu.VMEM((2, TOP_K * tc, d), F32), pltpu.SemaphoreType.DMA((2,))]),
        out_shape=jax.ShapeDtypeStruct((n, d), F32),
        compiler_params=_params("arbitrary"),
        name="moe_combine",
    )(pos_tiled, x, norm_final.reshape(1, d).astype(F32), ys)


def _layer(x, conv_buf, k_past, v_past, mk, mv, lam_init, w):
    bsz, t_len, d = x.shape
    n = bsz * t_len
    d_conv = w["w_dw"].shape[1]
    d_attn = w["w_attn_out"].shape[0]
    d_mem = w["w_mem_out"].shape[0]
    c0 = 2 * d_conv
    c1, c2, c3 = c0 + d_attn, c0 + 2 * d_attn, c0 + 3 * d_attn
    c4 = c3 + d_mem
    x2d = x.reshape(n, d)

    h = _rmsnorm(x2d, w["norm_mix"], BF16)
    w_in = w["w_in"]
    u_in = _matmul(h, w_in, F32, b_col0=0, n_out=c0)
    q = _matmul(h, w_in, BF16, b_col0=c0, n_out=d_attn)
    k_rows = _matmul(h, w_in, F32, b_col0=c1, n_out=d_attn)
    v_rows = _matmul(h, w_in, F32, b_col0=c2, n_out=d_attn)
    mq = _matmul(h, w_in, BF16, b_col0=c3, n_out=d_mem)
    gates = _matmul(h, w_in, BF16, b_col0=c4, n_out=N_BRANCH * d)

    conv_act, new_buf = _conv_branch(u_in.reshape(bsz, t_len, c0), conv_buf, w["w_dw"], w["b_dw"],
                                     w["conv_ln_g"], w["conv_ln_b"])
    o = _diff_attention(q.reshape(bsz, t_len, d_attn), k_rows.reshape(bsz, t_len, d_attn),
                        v_rows.reshape(bsz, t_len, d_attn), k_past, v_past,
                        w["lambda_q1"], w["lambda_k1"], w["lambda_q2"], w["lambda_k2"], w["subln_g"], lam_init)
    mo = _mem_attention(mq.reshape(bsz, t_len, d_mem), mk, mv)

    merged = _matmul(conv_act.reshape(n, d_conv), w["w_conv_out"], BF16, gate=gates, gate_col0=0)
    merged = _matmul(o.reshape(n, d_attn), w["w_attn_out"], BF16, gate=gates, gate_col0=d, add=merged)
    merged = _matmul(mo.reshape(n, d_mem), w["w_mem_out"], BF16, gate=gates, gate_col0=2 * d, add=merged)
    x2 = _matmul(merged, w["w_out"], F32, add=x2d)

    y = _hier_moe_final(x2, w["norm_ffn"], w["w_router_grp"], w["b_router_grp"], w["w_router_exp"],
                        w["b_router_exp"], w["w_exp_gate"], w["w_exp_up"], w["w_exp_down"], w["norm_final"])
    hd2 = 2 * HEAD_DIM
    return (y.reshape(bsz, t_len, d), new_buf, k_rows.reshape(bsz, t_len, N_HEADS, hd2),
            v_rows.reshape(bsz, t_len, N_HEADS, hd2))


def kernel(x_prompt, x_sample, mem_prompt, cache_conv, cache_diff_k, cache_diff_v, cache_mem_k, cache_mem_v, norm_mix, w_in, w_dw, b_dw, conv_ln_g, conv_ln_b, w_conv_out, lambda_q1, lambda_k1, lambda_q2, lambda_k2, subln_g, w_attn_out, norm_mem, w_mem_k, w_mem_v, w_mem_out, w_out, norm_ffn, w_router_grp, b_router_grp, w_router_exp, b_router_exp, w_exp_gate, w_exp_up, w_exp_down, norm_final):
    depth = w_in.shape[0]
    assert depth == 1, "the final rmsnorm is fused into the last layer's MoE combine"
    bp, _, d = x_prompt.shape
    n_mem = mem_prompt.shape[1]
    d_conv = w_dw.shape[2]
    d_mem = w_mem_k.shape[2]
    l = 0
    lam_init = 0.8 - 0.6 * math.exp(-0.3 * l)
    w = dict(norm_mix=norm_mix[l], w_in=w_in[l].astype(BF16), w_dw=w_dw[l], b_dw=b_dw[l], conv_ln_g=conv_ln_g[l],
             conv_ln_b=conv_ln_b[l], w_conv_out=w_conv_out[l].astype(BF16), lambda_q1=lambda_q1[l],
             lambda_k1=lambda_k1[l], lambda_q2=lambda_q2[l], lambda_k2=lambda_k2[l], subln_g=subln_g[l],
             w_attn_out=w_attn_out[l].astype(BF16), w_mem_out=w_mem_out[l].astype(BF16), w_out=w_out[l].astype(BF16),
             norm_ffn=norm_ffn[l], w_router_grp=w_router_grp[l], b_router_grp=b_router_grp[l],
             w_router_exp=w_router_exp[l], b_router_exp=b_router_exp[l], w_exp_gate=w_exp_gate[l].astype(BF16),
             w_exp_up=w_exp_up[l].astype(BF16), w_exp_down=w_exp_down[l].astype(BF16), norm_final=norm_final)

    hm = _rmsnorm(mem_prompt.reshape(bp * n_mem, d), norm_mem[l], BF16)
    mk = _matmul(hm, w_mem_k[l].astype(BF16), F32).reshape(bp, n_mem, d_mem)
    mv = _matmul(hm, w_mem_v[l].astype(BF16), F32).reshape(bp, n_mem, d_mem)
    conv0 = jnp.zeros((bp, CONV_W - 1, d_conv), F32)
    y_p, cb_p, k_p, v_p = _layer(x_prompt, conv0, None, None, mk, mv, lam_init, w)

    bs, past_len = cache_diff_k.shape[1], cache_diff_k.shape[2]
    d_attn = N_HEADS * 2 * HEAD_DIM
    y_s, cb_s, k_s, v_s = _layer(x_sample, cache_conv.reshape(cache_conv.shape[1:]),
                                 cache_diff_k.reshape(bs, past_len, d_attn),
                                 cache_diff_v.reshape(bs, past_len, d_attn),
                                 cache_mem_k.reshape(bs, n_mem, d_mem), cache_mem_v.reshape(bs, n_mem, d_mem),
                                 lam_init, w)
    mem_shape = (1, bp, n_mem, MEM_HEADS, MEM_HEAD_DIM)
    return (y_p, y_s, cb_p[None], k_p[None], v_p[None], mk.reshape(mem_shape), mv.reshape(mem_shape),
            cb_s[None], k_s[None], v_s[None])
```

```python
import functools
import math

import jax
import jax.numpy as jnp
from jax import lax
from jax.experimental import pallas as pl
from jax.experimental.pallas import tpu as pltpu

F32 = jnp.float32
BF16 = jnp.bfloat16

CHUNK = 64
EPS = 1e-6
SUBLN_EPS = 1e-5
NEG = -1e30
CONV_W = 31
N_HEADS = 16
HEAD_DIM = 64
MEM_HEADS = 4
MEM_HEAD_DIM = 256
N_BRANCH = 3
N_GROUPS = 4
EXPERTS_PER_GROUP = 8
N_EXPERTS = N_GROUPS * EXPERTS_PER_GROUP
TOP_K = 2

LANES = 128
SUBLANES = 8
VMEM_LIMIT_BYTES = 56 * 1024 * 1024
HALO = 32


def _params(*semantics):
    return pltpu.CompilerParams(dimension_semantics=semantics, vmem_limit_bytes=VMEM_LIMIT_BYTES)


def _tile(n, pref):
    if n <= pref:
        return n
    t = pref
    while n % t:
        t //= 2
    return t


def _rmsnorm_kernel(x_ref, g_ref, o_ref, *, eps):
    x = x_ref[...].astype(F32)
    ms = jnp.mean(x * x, axis=-1, keepdims=True)
    o_ref[...] = (x * lax.rsqrt(ms + eps) * g_ref[...]).astype(o_ref.dtype)


def _rmsnorm(x, g, out_dtype, eps=EPS):
    n, d = x.shape
    tr = _tile(n, 512)
    return pl.pallas_call(
        functools.partial(_rmsnorm_kernel, eps=eps),
        grid=(n // tr,),
        in_specs=[pl.BlockSpec((tr, d), lambda i: (i, 0)), pl.BlockSpec((1, d), lambda i: (0, 0))],
        out_specs=pl.BlockSpec((tr, d), lambda i: (i, 0)),
        out_shape=jax.ShapeDtypeStruct((n, d), out_dtype),
        compiler_params=_params("parallel"),
        name="rmsnorm",
    )(x, g.reshape(1, d).astype(F32))


def _mm_kernel(*refs, has_gate, has_add):
    a_ref, b_ref = refs[0], refs[1]
    o_ref = refs[-1]
    acc = jnp.dot(a_ref[...], b_ref[...], preferred_element_type=F32)
    pos = 2
    if has_gate:
        acc = acc * jax.nn.sigmoid(refs[pos][...].astype(F32))
        pos += 1
    if has_add:
        acc = acc + refs[pos][...].astype(F32)
    o_ref[...] = acc.astype(o_ref.dtype)


def _matmul(a, b, out_dtype, *, b_col0=0, n_out=None, gate=None, gate_col0=0, add=None):
    m, k = a.shape
    n_out = b.shape[1] if n_out is None else n_out
    tm = _tile(m, 512)
    tn = _tile(n_out, 1024)
    assert b_col0 % tn == 0 and gate_col0 % tn == 0
    bj0, gj0 = b_col0 // tn, gate_col0 // tn
    in_specs = [pl.BlockSpec((tm, k), lambda j, i: (i, 0)),
                pl.BlockSpec((k, tn), lambda j, i: (0, bj0 + j))]
    args = [a, b]
    if gate is not None:
        in_specs.append(pl.BlockSpec((tm, tn), lambda j, i: (i, gj0 + j)))
        args.append(gate)
    if add is not None:
        in_specs.append(pl.BlockSpec((tm, tn), lambda j, i: (i, j)))
        args.append(add)
    return pl.pallas_call(
        functools.partial(_mm_kernel, has_gate=gate is not None, has_add=add is not None),
        grid=(n_out // tn, m // tm),
        in_specs=in_specs,
        out_specs=pl.BlockSpec((tm, tn), lambda j, i: (i, j)),
        out_shape=jax.ShapeDtypeStruct((m, n_out), out_dtype),
        compiler_params=_params("parallel", "parallel"),
        name="matmul",
    )(*args)


def _conv_kernel(u_ref, buf_ref, w_ref, bdw_ref, g_ref, b_ref, act_ref, newbuf_ref, ext_ref, *, tt, dc):
    t = pl.program_id(1)
    hist = CONV_W - 1

    @pl.when(t == 0)
    def _():
        ext_ref[pl.ds(HALO - hist, hist), :] = buf_ref[0].astype(F32)

    @pl.when(t > 0)
    def _():
        ext_ref[pl.ds(0, HALO), :] = ext_ref[pl.ds(tt, HALO), :]

    a = u_ref[0, :, :dc].astype(F32)
    gt = u_ref[0, :, dc:].astype(F32)
    ext_ref[pl.ds(HALO, tt), :] = a * jax.nn.sigmoid(gt)

    y = jnp.zeros((tt, dc), F32) + bdw_ref[...]
    for j in range(CONV_W):
        y = y + ext_ref[pl.ds(HALO - hist + j, tt), :] * w_ref[pl.ds(j, 1), :]
    mu = jnp.mean(y, axis=-1, keepdims=True)
    var = jnp.mean(jnp.square(y - mu), axis=-1, keepdims=True)
    z = (y - mu) * lax.rsqrt(var + EPS) * g_ref[...] + b_ref[...]
    act_ref[0] = (z * jax.nn.sigmoid(z)).astype(act_ref.dtype)
    newbuf_ref[0] = ext_ref[pl.ds(HALO + tt - hist, hist), :]


def _conv_branch(u_in, conv_buf, w_dw, b_dw, ln_g, ln_b):
    bsz, t_len, two_dc = u_in.shape
    dc = two_dc // 2
    tt = _tile(t_len, 256)
    assert tt >= HALO and tt % SUBLANES == 0
    vec = lambda v: v.reshape(1, dc).astype(F32)
    return pl.pallas_call(
        functools.partial(_conv_kernel, tt=tt, dc=dc),
        grid=(bsz, t_len // tt),
        in_specs=[pl.BlockSpec((1, tt, two_dc), lambda b, t: (b, t, 0)),
                  pl.BlockSpec((1, CONV_W - 1, dc), lambda b, t: (b, 0, 0)),
                  pl.BlockSpec((CONV_W, dc), lambda b, t: (0, 0)),
                  pl.BlockSpec((1, dc), lambda b, t: (0, 0)),
                  pl.BlockSpec((1, dc), lambda b, t: (0, 0)),
                  pl.BlockSpec((1, dc), lambda b, t: (0, 0))],
        out_specs=[pl.BlockSpec((1, tt, dc), lambda b, t: (b, t, 0)),
                   pl.BlockSpec((1, CONV_W - 1, dc), lambda b, t: (b, 0, 0))],
        out_shape=[jax.ShapeDtypeStruct((bsz, t_len, dc), BF16),
                   jax.ShapeDtypeStruct((bsz, CONV_W - 1, dc), F32)],
        scratch_shapes=[pltpu.VMEM((HALO + tt, dc), F32)],
        compiler_params=_params("parallel", "arbitrary"),
        name="conv_branch",
    )(u_in, conv_buf, w_dw.astype(F32), vec(b_dw), vec(ln_g), vec(ln_b))


ONES_ROWS = 16
ATTN_TQ = 1024
ATTN_TK = 1024
ATTN_TK_CACHED = 512
LOG2E = 1.4426950408889634


def _stacked_qt(q):
    tq = q.shape[0]
    qt = (q * (HEAD_DIM ** -0.5 * LOG2E)).T
    z = jnp.zeros((HEAD_DIM, tq), F32)
    return jnp.concatenate([jnp.concatenate([qt[:HEAD_DIM], z], axis=0),
                            jnp.concatenate([z, qt[HEAD_DIM:]], axis=0)], axis=1).astype(BF16)


def _vt_ones(v):
    vt = v.astype(F32).T.astype(BF16)
    return jnp.concatenate([vt, jnp.ones((ONES_ROWS, v.shape[0]), BF16)], axis=0)


def _softmax_step(m_prev, acc_prev, qqt, k_blk, vt_blk, mask):
    s = jnp.dot(k_blk, qqt, preferred_element_type=F32)
    if mask is not None:
        s = jnp.where(mask, s, NEG)
    m_new = jnp.maximum(m_prev, jnp.max(s, axis=0, keepdims=True))
    alpha = jnp.exp2(m_prev - m_new)
    p = jnp.exp2(s - m_new).astype(BF16)
    return m_new, alpha * acc_prev + jnp.dot(vt_blk, p, preferred_element_type=F32)


def _chunk_mask(n_keys, key0, tq):
    q_chunk = lax.rem(lax.broadcasted_iota(jnp.int32, (n_keys, 2 * tq), 1), tq) // CHUNK
    k_chunk = (lax.broadcasted_iota(jnp.int32, (n_keys, 2 * tq), 0) + key0) // CHUNK
    return k_chunk <= q_chunk


def _diff_finalize(acc, lam, g_col, tq, lam_init):
    hd2 = 2 * HEAD_DIM
    inv_l = 1.0 / acc[hd2:hd2 + 1, :]
    ot = acc[:hd2, :tq] * inv_l[:, :tq] - lam * (acc[:hd2, tq:] * inv_l[:, tq:])
    ot = ot * lax.rsqrt(jnp.mean(ot * ot, axis=0, keepdims=True) + SUBLN_EPS) * g_col
    return (ot * (1.0 - lam_init)).T


def _lambda(lq1_ref, lk1_ref, lq2_ref, lk2_ref, lam_init):
    return (jnp.exp(jnp.sum(lq1_ref[...] * lk1_ref[...], axis=1, keepdims=True))
            - jnp.exp(jnp.sum(lq2_ref[...] * lk2_ref[...], axis=1, keepdims=True)) + lam_init)


def _diff_attn_kernel(q_ref, k_ref, v_ref, lq1_ref, lk1_ref, lq2_ref, lk2_ref, g_ref, o_ref, k_scr, vt_scr, acc_ref,
                      *, n_blk, tq, tk, lam_init):
    qi = pl.program_id(2)

    @pl.when(qi == 0)
    def _():
        def body(i, carry):
            off = pl.multiple_of(i * tk, tk)
            k_scr[i] = k_ref[0, pl.ds(off, tk), :].astype(BF16)
            vt_scr[i] = _vt_ones(v_ref[0, pl.ds(off, tk), :])
            return carry
        lax.fori_loop(0, n_blk, body, 0)

    qqt = _stacked_qt(q_ref[0].astype(F32))
    acc_ref[...] = jnp.zeros(acc_ref.shape, F32)

    def step(m_prev, kb, mask):
        m_new, acc = _softmax_step(m_prev, acc_ref[...], qqt, k_scr[kb], vt_scr[kb], mask)
        acc_ref[...] = acc
        return m_new

    m = jnp.full((1, 2 * tq), NEG, F32)
    blk_per_q = tq // tk
    m = lax.fori_loop(0, qi * blk_per_q, lambda kb, mm: step(mm, kb, None), m)
    for d in range(blk_per_q):
        m = step(m, qi * blk_per_q + d, _chunk_mask(tk, d * tk, tq))

    lam = _lambda(lq1_ref, lk1_ref, lq2_ref, lk2_ref, lam_init)
    o_ref[0] = _diff_finalize(acc_ref[...], lam, g_ref[...], tq, lam_init).astype(o_ref.dtype)


def _cached_attn_kernel(q_ref, kc_ref, vc_ref, kp_ref, vp_ref, lq1_ref, lk1_ref, lq2_ref, lk2_ref, g_ref, o_ref,
                        qq_scr, m_scr, acc_scr, *, n_heads, tq, tkp, lam_init):
    j = pl.program_id(1)
    hd2 = 2 * HEAD_DIM
    head = lambda h: slice(h * hd2, (h + 1) * hd2)

    @pl.when(j == 0)
    def _():
        for h in range(n_heads):
            qq_scr[h] = _stacked_qt(q_ref[0, :, head(h)].astype(F32))
        m_scr[...] = jnp.full(m_scr.shape, NEG, F32)
        acc_scr[...] = jnp.zeros(acc_scr.shape, F32)

    def update(h, k_blk, v_blk, mask):
        m_new, acc = _softmax_step(m_scr[h], acc_scr[h], qq_scr[h], k_blk.astype(BF16), _vt_ones(v_blk), mask)
        m_scr[h] = m_new
        acc_scr[h] = acc

    for h in range(n_heads):
        update(h, kp_ref[0, pl.ds(h, tkp, stride=n_heads), :], vp_ref[0, pl.ds(h, tkp, stride=n_heads), :], None)

    @pl.when(j == pl.num_programs(1) - 1)
    def _():
        lam = _lambda(lq1_ref, lk1_ref, lq2_ref, lk2_ref, lam_init)
        mask = _chunk_mask(tq, 0, tq)
        for h in range(n_heads):
            update(h, kc_ref[0, :, head(h)], vc_ref[0, :, head(h)], mask)
            o_ref[0, :, head(h)] = _diff_finalize(acc_scr[h], lam, g_ref[...], tq, lam_init).astype(o_ref.dtype)


def _lambda_args(lq1, lk1, lq2, lk2, subln_g):
    vec = lambda v: v.reshape(1, -1).astype(F32)
    return [vec(lq1), vec(lk1), vec(lq2), vec(lk2), subln_g.reshape(-1, 1).astype(F32)]


def _diff_attention(q, k_cur, v_cur, lq1, lk1, lq2, lk2, subln_g, lam_init):
    bsz, t_len, d_attn = q.shape
    hd2 = 2 * HEAD_DIM
    tq = _tile(t_len, ATTN_TQ)
    tk = _tile(tq, ATTN_TK)
    n_blk = t_len // tk
    assert tk % CHUNK == 0
    const = lambda shape: pl.BlockSpec(shape, lambda b, h, i: (0, 0))
    return pl.pallas_call(
        functools.partial(_diff_attn_kernel, n_blk=n_blk, tq=tq, tk=tk, lam_init=lam_init),
        grid=(bsz, d_attn // hd2, t_len // tq),
        in_specs=[pl.BlockSpec((1, tq, hd2), lambda b, h, i: (b, i, h)),
                  pl.BlockSpec((1, t_len, hd2), lambda b, h, i: (b, 0, h)),
                  pl.BlockSpec((1, t_len, hd2), lambda b, h, i: (b, 0, h))]
        + [const((1, HEAD_DIM))] * 4 + [const((hd2, 1))],
        out_specs=pl.BlockSpec((1, tq, hd2), lambda b, h, i: (b, i, h)),
        out_shape=jax.ShapeDtypeStruct((bsz, t_len, d_attn), BF16),
        scratch_shapes=[pltpu.VMEM((n_blk, tk, hd2), BF16), pltpu.VMEM((n_blk, hd2 + ONES_ROWS, tk), BF16),
                        pltpu.VMEM((hd2 + ONES_ROWS, 2 * tq), F32)],
        compiler_params=_params("parallel", "parallel", "arbitrary"),
        name="diff_attention",
    )(q, k_cur, v_cur, *_lambda_args(lq1, lk1, lq2, lk2, subln_g))


def _cached_diff_attention(q, k_cur, v_cur, k_past, v_past, lq1, lk1, lq2, lk2, subln_g, lam_init):
    bsz, t_len, d_attn = q.shape
    hd2 = 2 * HEAD_DIM
    past_len, n_heads = k_past.shape[1], k_past.shape[2]
    tq = t_len
    tkp = _tile(past_len, ATTN_TK_CACHED)
    assert tq % CHUNK == 0 and past_len % CHUNK == 0 and tq <= ATTN_TQ and n_heads * hd2 == d_attn
    assert n_heads % SUBLANES == 0
    flat = lambda c: c.reshape(bsz, past_len * n_heads, hd2)
    const = lambda shape: pl.BlockSpec(shape, lambda b, j: (0, 0))
    rows = pl.BlockSpec((1, t_len, d_attn), lambda b, j: (b, 0, 0))
    hist = pl.BlockSpec((1, tkp * n_heads, hd2), lambda b, j: (b, j, 0))
    return pl.pallas_call(
        functools.partial(_cached_attn_kernel, n_heads=n_heads, tq=tq, tkp=tkp, lam_init=lam_init),
        grid=(bsz, past_len // tkp),
        in_specs=[rows, rows, rows, hist, hist] + [const((1, HEAD_DIM))] * 4 + [const((hd2, 1))],
        out_specs=rows,
        out_shape=jax.ShapeDtypeStruct((bsz, t_len, d_attn), BF16),
        scratch_shapes=[pltpu.VMEM((n_heads, hd2, 2 * tq), BF16), pltpu.VMEM((n_heads, 1, 2 * tq), F32),
                        pltpu.VMEM((n_heads, hd2 + ONES_ROWS, 2 * tq), F32)],
        compiler_params=_params("parallel", "arbitrary"),
        name="cached_diff_attention",
    )(q, k_cur, v_cur, flat(k_past), flat(v_past), *_lambda_args(lq1, lk1, lq2, lk2, subln_g))


def _mem_attn_kernel(q_ref, k_ref, v_ref, o_ref):
    scale = jnp.asarray(MEM_HEAD_DIM ** -0.5, q_ref.dtype)
    for h in range(MEM_HEADS):
        sl = slice(h * MEM_HEAD_DIM, (h + 1) * MEM_HEAD_DIM)
        q = q_ref[0, :, sl] * scale
        s = lax.dot_general(q, k_ref[0, :, sl].astype(BF16), (((1,), (1,)), ((), ())), preferred_element_type=F32)
        p = jnp.exp(s - jnp.max(s, axis=1, keepdims=True))
        p = p / jnp.sum(p, axis=1, keepdims=True)
        o = jnp.dot(p.astype(BF16), v_ref[0, :, sl].astype(BF16), preferred_element_type=F32)
        o_ref[0, :, sl] = o.astype(o_ref.dtype)


def _mem_attention(mq, mk, mv):
    bsz, t_len, d_mem = mq.shape
    n_mem = mk.shape[1]
    tt = _tile(t_len, 512)
    return pl.pallas_call(
        _mem_attn_kernel,
        grid=(bsz, t_len // tt),
        in_specs=[pl.BlockSpec((1, tt, d_mem), lambda b, t: (b, t, 0)),
                  pl.BlockSpec((1, n_mem, d_mem), lambda b, t: (b, 0, 0)),
                  pl.BlockSpec((1, n_mem, d_mem), lambda b, t: (b, 0, 0))],
        out_specs=pl.BlockSpec((1, tt, d_mem), lambda b, t: (b, t, 0)),
        out_shape=jax.ShapeDtypeStruct((bsz, t_len, d_mem), BF16),
        compiler_params=_params("parallel", "parallel"),
        name="mem_attention",
    )(mq, mk, mv)


def _to_slabs(dst_ref, val, rows, n_slab):
    for c in range(n_slab):
        dst_ref[pl.ds(c, rows, stride=n_slab), :] = val[:, c * LANES:(c + 1) * LANES].astype(dst_ref.dtype)


def _slab_chunk(src_ref, c, rows, n_slab, row0=0):
    return src_ref[pl.ds(row0 * n_slab + c, rows, stride=n_slab), :]


def _router_kernel(*refs, tr, n_slab, group_tiles):
    n_grp = len(group_tiles)
    x_refs = refs[:n_grp]
    g_ref, whi_ref, wlo_ref, b_ref, hn_ref, eid_ref, wt_ref, rank_ref, cnt_ref, run_ref = refs[n_grp:]
    t = pl.program_id(0)
    x = x_refs[-1][...]
    first_tile = sum(group_tiles[:-1])
    for gi in range(n_grp - 2, -1, -1):
        x = jnp.where(t < first_tile, x_refs[gi][...], x)
        first_tile -= group_tiles[gi]
    hn = x * lax.rsqrt(jnp.mean(x * x, axis=-1, keepdims=True) + EPS) * g_ref[...]
    _to_slabs(hn_ref, hn, tr, n_slab)
    h_hi = hn.astype(BF16)
    h_lo = (hn - h_hi.astype(F32)).astype(BF16)
    logits = (jnp.dot(h_hi, whi_ref[...], preferred_element_type=F32)
              + jnp.dot(h_lo, whi_ref[...], preferred_element_type=F32)
              + jnp.dot(h_hi, wlo_ref[...], preferred_element_type=F32)) + b_ref[...]
    lane = lax.broadcasted_iota(jnp.int32, logits.shape, 1)
    first = lambda hit: jnp.min(jnp.where(hit, lane, LANES), axis=1, keepdims=True)

    gl = jnp.where(lane < N_GROUPS, logits, -jnp.inf)
    gmax = jnp.max(gl, axis=1, keepdims=True)
    g_idx = first(gl == gmax)
    p_g = 1.0 / jnp.sum(jnp.exp(gl - gmax), axis=1, keepdims=True)

    lo = N_GROUPS + EXPERTS_PER_GROUP * g_idx
    in_grp = (lane >= lo) & (lane < lo + EXPERTS_PER_GROUP)
    el = jnp.where(in_grp, logits, -jnp.inf)
    pe = jnp.exp(el - jnp.max(el, axis=1, keepdims=True))
    probs = jnp.where(in_grp, pe / jnp.sum(pe, axis=1, keepdims=True), -1.0)
    v1 = jnp.max(probs, axis=1, keepdims=True)
    i1 = first(probs == v1)
    probs2 = jnp.where(lane == i1, -1.0, probs)
    v2 = jnp.max(probs2, axis=1, keepdims=True)
    i2 = first(probs2 == v2)
    tot = v1 + v2
    eid_ref[:, 0:1] = i1 - N_GROUPS
    eid_ref[:, 1:2] = i2 - N_GROUPS
    wt_ref[:, 0:1] = p_g * (v1 / tot)
    wt_ref[:, 1:2] = p_g * (v2 / tot)

    @pl.when(t == 0)
    def _():
        run_ref[...] = jnp.zeros(run_ref.shape, F32)

    sel1 = lane == i1
    sel2 = lane == i2
    oh1 = jnp.where(sel1, 1.0, 0.0).astype(BF16)
    oh2 = jnp.where(sel2, 1.0, 0.0).astype(BF16)
    earlier = (lax.broadcasted_iota(jnp.int32, (tr, tr), 1) < lax.broadcasted_iota(jnp.int32, (tr, tr), 0))
    tri = jnp.where(earlier, 1.0, 0.0).astype(BF16)
    run = run_ref[...]
    cnt1 = jnp.sum(oh1.astype(F32), axis=0, keepdims=True)
    cnt2 = jnp.sum(oh2.astype(F32), axis=0, keepdims=True)
    before1 = jnp.dot(tri, oh1, preferred_element_type=F32) + run
    before2 = jnp.dot(tri, oh2, preferred_element_type=F32) + (run + cnt1)
    rank_ref[:, 0:1] = jnp.sum(jnp.where(sel1, before1, 0.0), axis=1, keepdims=True).astype(jnp.int32)
    rank_ref[:, 1:2] = jnp.sum(jnp.where(sel2, before2, 0.0), axis=1, keepdims=True).astype(jnp.int32)
    run = run + cnt1 + cnt2
    run_ref[...] = run
    cnt_ref[...] = run.astype(jnp.int32)


def _router(xs, g, w_rg, b_rg, w_re, b_re):
    d = xs[0].shape[1]
    n = sum(x.shape[0] for x in xs)
    n_slab = d // LANES
    tr = 256
    assert all(x.shape[0] % tr == 0 for x in xs)
    group_tiles = tuple(x.shape[0] // tr for x in xs)
    x_specs = []
    for gi, nt in enumerate(group_tiles):
        t0 = sum(group_tiles[:gi])
        x_specs.append(pl.BlockSpec((tr, d), lambda i, t0=t0, nt=nt: (jnp.clip(i - t0, 0, nt - 1), 0)))
    w = jnp.zeros((d, LANES), F32).at[:, :N_GROUPS].set(w_rg).at[:, N_GROUPS:N_GROUPS + N_EXPERTS].set(w_re)
    b = jnp.zeros((1, LANES), F32).at[0, :N_GROUPS].set(b_rg).at[0, N_GROUPS:N_GROUPS + N_EXPERTS].set(b_re)
    w_hi = w.astype(BF16)
    w_lo = (w - w_hi.astype(F32)).astype(BF16)
    small = lambda dt: jax.ShapeDtypeStruct((n, TOP_K), dt)
    small_spec = pl.BlockSpec((tr, TOP_K), lambda i: (i, 0))
    return pl.pallas_call(
        functools.partial(_router_kernel, tr=tr, n_slab=n_slab, group_tiles=group_tiles),
        grid=(n // tr,),
        in_specs=x_specs + [
                  pl.BlockSpec((1, d), lambda i: (0, 0)),
                  pl.BlockSpec((d, LANES), lambda i: (0, 0)),
                  pl.BlockSpec((d, LANES), lambda i: (0, 0)),
                  pl.BlockSpec((1, LANES), lambda i: (0, 0))],
        out_specs=[pl.BlockSpec((tr * n_slab, LANES), lambda i: (i, 0)), small_spec, small_spec, small_spec,
                   pl.BlockSpec((1, LANES), lambda i: (0, 0))],
        out_shape=[jax.ShapeDtypeStruct((n * n_slab, LANES), F32), small(jnp.int32), small(F32), small(jnp.int32),
                   jax.ShapeDtypeStruct((1, LANES), jnp.int32)],
        scratch_shapes=[pltpu.VMEM((1, LANES), F32)],
        compiler_params=_params("arbitrary"),
        name="moe_router",
    )(*xs, g.reshape(1, d).astype(F32), w_hi, w_lo, b)


def _dispatch_kernel(pos_ref, zstart_ref, zvalid_ref, n_active_ref, hn_ref, xs_out, zbuf, sem, zsem,
                     *, td, tm, n_slab, n_tiles):
    t = pl.program_id(0)
    zrows = tm * n_slab

    def zero_tile(row0):
        return pltpu.make_async_copy(zbuf, xs_out.at[pl.ds(pl.multiple_of(row0 * n_slab, n_slab), zrows), :], zsem)

    @pl.when(t == 0)
    def _():
        zbuf[...] = jnp.zeros(zbuf.shape, zbuf.dtype)

        def for_each_zero_tile(act):
            for e in range(N_EXPERTS):
                @pl.when(zvalid_ref[e] > 0)
                def _():
                    act(zero_tile(zstart_ref[e]))

            def unused(i, carry):
                act(zero_tile(i * tm))
                return carry
            lax.fori_loop(n_active_ref[0], n_tiles, unused, 0)

        for_each_zero_tile(lambda copy: copy.start())
        for_each_zero_tile(lambda copy: copy.wait())

    def body(r, carry):
        src = hn_ref.at[pl.ds(pl.multiple_of(r * n_slab, n_slab), n_slab), :]
        for kk in range(TOP_K):
            dst_row = pl.multiple_of(pos_ref[(t * td + r) * TOP_K + kk] * n_slab, n_slab)
            pltpu.make_async_copy(src, xs_out.at[pl.ds(dst_row, n_slab), :], sem).start()
        return carry
    lax.fori_loop(0, td, body, 0)
    for kk in range(TOP_K):
        pltpu.make_async_copy(hn_ref, xs_out.at[pl.ds(0, td * n_slab), :], sem).wait()


def _expert_kernel(tile_expert_ref, n_active_ref, x_ref, wg_ref, wu_ref, wd_ref, ys_ref, xb_ref, *, tm, n_slab):
    del tile_expert_ref
    active = pl.program_id(0) < n_active_ref[0]

    @pl.when(jnp.logical_not(active))
    def _():
        ys_ref[...] = jnp.zeros(ys_ref.shape, ys_ref.dtype)

    @pl.when(active)
    def _():
        for c in range(n_slab):
            xb_ref[:, c * LANES:(c + 1) * LANES] = _slab_chunk(x_ref, c, tm, n_slab).astype(BF16)
        x = xb_ref[...]
        g = jnp.dot(x, wg_ref[0], preferred_element_type=F32)
        u = jnp.dot(x, wu_ref[0], preferred_element_type=F32)
        act = (g * jax.nn.sigmoid(g) * u).astype(BF16)
        _to_slabs(ys_ref, jnp.dot(act, wd_ref[0], preferred_element_type=F32), tm, n_slab)


def _start_slab_gather(pos_ref, tok0, ys_hbm, ybuf, sem, slot, tc, n_slab):
    def body(r, carry):
        for kk in range(TOP_K):
            src_row = pl.multiple_of(pos_ref[(tok0 + r) * TOP_K + kk] * n_slab, n_slab)
            dst_row = pl.multiple_of((kk * tc + r) * n_slab, n_slab)
            pltpu.make_async_copy(ys_hbm.at[pl.ds(src_row, n_slab), :],
                                  ybuf.at[slot, pl.ds(dst_row, n_slab), :], sem.at[slot]).start()
        return carry
    lax.fori_loop(0, tc, body, 0)


def _combine_kernel(pos_ref, x_ref, w_ref, g_ref, ys_hbm, o_ref, ybuf, sem, *, tc, tok_base, n_slab):
    t = pl.program_id(0)
    nt = pl.num_programs(0)
    slot = lax.rem(t, 2)

    @pl.when(t == 0)
    def _():
        _start_slab_gather(pos_ref, tok_base, ys_hbm, ybuf, sem, 0, tc, n_slab)

    @pl.when(t + 1 < nt)
    def _():
        _start_slab_gather(pos_ref, tok_base + (t + 1) * tc, ys_hbm, ybuf, sem, 1 - slot, tc, n_slab)

    pltpu.make_async_copy(ys_hbm.at[pl.ds(0, TOP_K * tc * n_slab), :], ybuf.at[slot], sem.at[slot]).wait()
    yb = ybuf.at[slot]
    wk = [jnp.broadcast_to(w_ref[:, kk:kk + 1], (tc, LANES)) for kk in range(TOP_K)]
    sq = jnp.zeros((tc, LANES), F32)
    for c in range(n_slab):
        moe = wk[0] * _slab_chunk(yb, c, tc, n_slab)
        for kk in range(1, TOP_K):
            moe = moe + wk[kk] * _slab_chunk(yb, c, tc, n_slab, row0=kk * tc)
        y = x_ref[:, c * LANES:(c + 1) * LANES] + moe
        o_ref[:, c * LANES:(c + 1) * LANES] = y
        sq = sq + y * y
    d = n_slab * LANES
    scale = lax.rsqrt(jnp.sum(sq, axis=1, keepdims=True) / d + EPS)
    o_ref[...] = o_ref[...] * scale * g_ref[...]


def _hier_moe_final(xs, norm_ffn, w_rg, b_rg, w_re, b_re, w_g, w_u, w_d, norm_final):
    d = xs[0].shape[1]
    n = sum(x.shape[0] for x in xs)
    n_slab = d // LANES
    d_exp = w_g.shape[2]
    hn, eid, wts, rank, cnt = _router(xs, norm_ffn, w_rg, b_rg, w_re, b_re)

    tm = 256
    na = n * TOP_K
    n_tiles = (na + N_EXPERTS * (tm - 1)) // tm + 1
    n_rows = n_tiles * tm
    counts = cnt[0, N_GROUPS:N_GROUPS + N_EXPERTS]
    padded = ((counts + tm - 1) // tm) * tm
    pend = jnp.cumsum(padded)
    pstart = pend - padded
    experts = jnp.arange(N_EXPERTS, dtype=jnp.int32)
    pos = (jnp.sum(jnp.where(eid[:, :, None] == experts, pstart, 0), axis=-1) + rank).reshape(na)
    tile_start = jnp.arange(n_tiles, dtype=jnp.int32) * tm
    tile_expert = jnp.minimum(jnp.sum((pend[None, :] <= tile_start[:, None]).astype(jnp.int32), axis=1), N_EXPERTS - 1)
    n_active = (pend[-1] // tm).astype(jnp.int32).reshape(1)
    zstart = jnp.maximum(pend - tm, 0).astype(jnp.int32)
    zvalid = (counts > 0).astype(jnp.int32)

    td = _tile(n, 256)
    x_sorted = pl.pallas_call(
        functools.partial(_dispatch_kernel, td=td, tm=tm, n_slab=n_slab, n_tiles=n_tiles),
        grid_spec=pltpu.PrefetchScalarGridSpec(
            num_scalar_prefetch=4,
            grid=(n // td,),
            in_specs=[pl.BlockSpec((td * n_slab, LANES), lambda t, p, zs, zv, na_: (t, 0))],
            out_specs=pl.BlockSpec(memory_space=pl.ANY),
            scratch_shapes=[pltpu.VMEM((tm * n_slab, LANES), F32), pltpu.SemaphoreType.DMA(()),
                            pltpu.SemaphoreType.DMA(())]),
        out_shape=jax.ShapeDtypeStruct((n_rows * n_slab, LANES), F32),
        compiler_params=_params("arbitrary"),
        name="moe_dispatch",
    )(pos, zstart, zvalid, n_active, hn)

    active = lambda t, te, na_: (jnp.minimum(t, na_[0] - 1), 0)
    ys = pl.pallas_call(
        functools.partial(_expert_kernel, tm=tm, n_slab=n_slab),
        grid_spec=pltpu.PrefetchScalarGridSpec(
            num_scalar_prefetch=2,
            grid=(n_tiles,),
            in_specs=[pl.BlockSpec((tm * n_slab, LANES), active),
                      pl.BlockSpec((1, d, d_exp), lambda t, te, na_: (te[t], 0, 0)),
                      pl.BlockSpec((1, d, d_exp), lambda t, te, na_: (te[t], 0, 0)),
                      pl.BlockSpec((1, d_exp, d), lambda t, te, na_: (te[t], 0, 0))],
            out_specs=pl.BlockSpec((tm * n_slab, LANES), lambda t, te, na_: (t, 0)),
            scratch_shapes=[pltpu.VMEM((tm, d), BF16)]),
        out_shape=jax.ShapeDtypeStruct((n_rows * n_slab, LANES), F32),
        compiler_params=_params("arbitrary"),
        name="moe_experts",
    )(tile_expert, n_active, x_sorted, w_g, w_u, w_d)

    outs = []
    row0 = 0
    for x in xs:
        rows = x.shape[0]
        tc = _tile(rows, 256)
        assert row0 % tc == 0
        blk0 = row0 // tc
        outs.append(pl.pallas_call(
            functools.partial(_combine_kernel, tc=tc, tok_base=row0, n_slab=n_slab),
            grid_spec=pltpu.PrefetchScalarGridSpec(
                num_scalar_prefetch=1,
                grid=(rows // tc,),
                in_specs=[pl.BlockSpec((tc, d), lambda t, p: (t, 0)),
                          pl.BlockSpec((tc, TOP_K), lambda t, p, blk0=blk0: (blk0 + t, 0)),
                          pl.BlockSpec((1, d), lambda t, p: (0, 0)),
                          pl.BlockSpec(memory_space=pl.ANY)],
                out_specs=pl.BlockSpec((tc, d), lambda t, p: (t, 0)),
                scratch_shapes=[pltpu.VMEM((2, TOP_K * tc * n_slab, LANES), F32), pltpu.SemaphoreType.DMA((2,))]),
            out_shape=jax.ShapeDtypeStruct((rows, d), F32),
            compiler_params=_params("arbitrary"),
            name="moe_combine",
        )(pos, x, wts, norm_final.reshape(1, d).astype(F32), ys))
        row0 += rows
    return outs


def _mixer(x, conv_buf, k_past, v_past, mk, mv, lam_init, w):
    bsz, t_len, d = x.shape
    n = bsz * t_len
    d_conv = w["w_dw"].shape[1]
    d_attn = w["w_attn_out"].shape[0]
    d_mem = w["w_mem_out"].shape[0]
    c0 = 2 * d_conv
    c1, c2, c3 = c0 + d_attn, c0 + 2 * d_attn, c0 + 3 * d_attn
    c4 = c3 + d_mem
    x2d = x.reshape(n, d)

    h = _rmsnorm(x2d, w["norm_mix"], BF16)
    w_in = w["w_in"]
    u_in = _matmul(h, w_in, F32, b_col0=0, n_out=c0)
    q = _matmul(h, w_in, F32, b_col0=c0, n_out=d_attn)
    k_rows = _matmul(h, w_in, F32, b_col0=c1, n_out=d_attn)
    v_rows = _matmul(h, w_in, F32, b_col0=c2, n_out=d_attn)
    mq = _matmul(h, w_in, BF16, b_col0=c3, n_out=d_mem)
    gates = _matmul(h, w_in, BF16, b_col0=c4, n_out=N_BRANCH * d)

    conv_act, new_buf = _conv_branch(u_in.reshape(bsz, t_len, c0), conv_buf, w["w_dw"], w["b_dw"],
                                     w["conv_ln_g"], w["conv_ln_b"])
    qkv = [a.reshape(bsz, t_len, d_attn) for a in (q, k_rows, v_rows)]
    lam_args = (w["lambda_q1"], w["lambda_k1"], w["lambda_q2"], w["lambda_k2"], w["subln_g"], lam_init)
    if k_past is None:
        o = _diff_attention(*qkv, *lam_args)
    else:
        o = _cached_diff_attention(*qkv, k_past, v_past, *lam_args)
    mo = _mem_attention(mq.reshape(bsz, t_len, d_mem), mk, mv)

    merged = _matmul(conv_act.reshape(n, d_conv), w["w_conv_out"], BF16, gate=gates, gate_col0=0)
    merged = _matmul(o.reshape(n, d_attn), w["w_attn_out"], BF16, gate=gates, gate_col0=d, add=merged)
    merged = _matmul(mo.reshape(n, d_mem), w["w_mem_out"], BF16, gate=gates, gate_col0=2 * d, add=merged)
    x2 = _matmul(merged, w["w_out"], F32, add=x2d)
    hd2 = 2 * HEAD_DIM
    return x2, new_buf, k_rows.reshape(bsz, t_len, N_HEADS, hd2), v_rows.reshape(bsz, t_len, N_HEADS, hd2)


def kernel(x_prompt, x_sample, mem_prompt, cache_conv, cache_diff_k, cache_diff_v, cache_mem_k, cache_mem_v, norm_mix, w_in, w_dw, b_dw, conv_ln_g, conv_ln_b, w_conv_out, lambda_q1, lambda_k1, lambda_q2, lambda_k2, subln_g, w_attn_out, norm_mem, w_mem_k, w_mem_v, w_mem_out, w_out, norm_ffn, w_router_grp, b_router_grp, w_router_exp, b_router_exp, w_exp_gate, w_exp_up, w_exp_down, norm_final):
    depth = w_in.shape[0]
    assert depth == 1, "the final rmsnorm is fused into the last layer's MoE combine"
    bp, tp, d = x_prompt.shape
    bs, ts, _ = x_sample.shape
    n_mem = mem_prompt.shape[1]
    d_conv = w_dw.shape[2]
    d_mem = w_mem_k.shape[2]
    l = 0
    lam_init = 0.8 - 0.6 * math.exp(-0.3 * l)
    w = dict(norm_mix=norm_mix[l], w_in=w_in[l].astype(BF16), w_dw=w_dw[l], b_dw=b_dw[l], conv_ln_g=conv_ln_g[l],
             conv_ln_b=conv_ln_b[l], w_conv_out=w_conv_out[l].astype(BF16), lambda_q1=lambda_q1[l],
             lambda_k1=lambda_k1[l], lambda_q2=lambda_q2[l], lambda_k2=lambda_k2[l], subln_g=subln_g[l],
             w_attn_out=w_attn_out[l].astype(BF16), w_mem_out=w_mem_out[l].astype(BF16), w_out=w_out[l].astype(BF16))

    hm = _rmsnorm(mem_prompt.reshape(bp * n_mem, d), norm_mem[l], BF16)
    mk = _matmul(hm, w_mem_k[l].astype(BF16), F32).reshape(bp, n_mem, d_mem)
    mv = _matmul(hm, w_mem_v[l].astype(BF16), F32).reshape(bp, n_mem, d_mem)
    conv0 = jnp.zeros((bp, CONV_W - 1, d_conv), F32)
    x2_p, cb_p, k_p, v_p = _mixer(x_prompt, conv0, None, None, mk, mv, lam_init, w)

    x2_s, cb_s, k_s, v_s = _mixer(x_sample, cache_conv.reshape(cache_conv.shape[1:]),
                                  cache_diff_k.reshape(cache_diff_k.shape[1:]),
                                  cache_diff_v.reshape(cache_diff_v.shape[1:]),
                                  cache_mem_k.reshape(bs, n_mem, d_mem), cache_mem_v.reshape(bs, n_mem, d_mem),
                                  lam_init, w)

    y_p, y_s = _hier_moe_final([x2_p, x2_s], norm_ffn[l], w_router_grp[l], b_router_grp[l], w_router_exp[l],
                               b_router_exp[l], w_exp_gate[l].astype(BF16), w_exp_up[l].astype(BF16),
                               w_exp_down[l].astype(BF16), norm_final)
    mem_shape = (1, bp, n_mem, MEM_HEADS, MEM_HEAD_DIM)
    return (y_p.reshape(bp, tp, d), y_s.reshape(bs, ts, d), cb_p[None], k_p[None], v_p[None],
            mk.reshape(mem_shape), mv.reshape(mem_shape), cb_s[None], k_s[None], v_s[None])
```

```python
import functools
import math

import jax
import jax.numpy as jnp
from jax import lax
from jax.experimental import pallas as pl
from jax.experimental.pallas import tpu as pltpu

F32 = jnp.float32
BF16 = jnp.bfloat16

CHUNK = 64
EPS = 1e-6
SUBLN_EPS = 1e-5
NEG = -1e30
CONV_W = 31
N_HEADS = 16
HEAD_DIM = 64
MEM_HEADS = 4
MEM_HEAD_DIM = 256
N_BRANCH = 3
N_GROUPS = 4
EXPERTS_PER_GROUP = 8
N_EXPERTS = N_GROUPS * EXPERTS_PER_GROUP
TOP_K = 2

LANES = 128
SUBLANES = 8
VMEM_LIMIT_BYTES = 56 * 1024 * 1024
HALO = 32


def _params(*semantics):
    return pltpu.CompilerParams(dimension_semantics=semantics, vmem_limit_bytes=VMEM_LIMIT_BYTES)


def _tile(n, pref):
    if n <= pref:
        return n
    t = pref
    while n % t:
        t //= 2
    return t


def _rmsnorm_kernel(x_ref, g_ref, o_ref, *, eps):
    x = x_ref[...].astype(F32)
    ms = jnp.mean(x * x, axis=-1, keepdims=True)
    o_ref[...] = (x * lax.rsqrt(ms + eps) * g_ref[...]).astype(o_ref.dtype)


def _rmsnorm(x, g, out_dtype, eps=EPS):
    n, d = x.shape
    tr = _tile(n, 512)
    return pl.pallas_call(
        functools.partial(_rmsnorm_kernel, eps=eps),
        grid=(n // tr,),
        in_specs=[pl.BlockSpec((tr, d), lambda i: (i, 0)), pl.BlockSpec((1, d), lambda i: (0, 0))],
        out_specs=pl.BlockSpec((tr, d), lambda i: (i, 0)),
        out_shape=jax.ShapeDtypeStruct((n, d), out_dtype),
        compiler_params=_params("parallel"),
        name="rmsnorm",
    )(x, g.reshape(1, d).astype(F32))


def _mm_kernel(*refs, has_gate, has_add):
    a_ref, b_ref = refs[0], refs[1]
    o_ref = refs[-1]
    acc = jnp.dot(a_ref[...], b_ref[...], preferred_element_type=F32)
    pos = 2
    if has_gate:
        acc = acc * jax.nn.sigmoid(refs[pos][...].astype(F32))
        pos += 1
    if has_add:
        acc = acc + refs[pos][...].astype(F32)
    o_ref[...] = acc.astype(o_ref.dtype)


def _matmul(a, b, out_dtype, *, b_col0=0, n_out=None, gate=None, gate_col0=0, add=None):
    m, k = a.shape
    n_out = b.shape[1] if n_out is None else n_out
    tm = _tile(m, 1024)
    tn = _tile(n_out, 1024)
    assert b_col0 % tn == 0 and gate_col0 % tn == 0
    bj0, gj0 = b_col0 // tn, gate_col0 // tn
    in_specs = [pl.BlockSpec((tm, k), lambda j, i: (i, 0)),
                pl.BlockSpec((k, tn), lambda j, i: (0, bj0 + j))]
    args = [a, b]
    if gate is not None:
        in_specs.append(pl.BlockSpec((tm, tn), lambda j, i: (i, gj0 + j)))
        args.append(gate)
    if add is not None:
        in_specs.append(pl.BlockSpec((tm, tn), lambda j, i: (i, j)))
        args.append(add)
    return pl.pallas_call(
        functools.partial(_mm_kernel, has_gate=gate is not None, has_add=add is not None),
        grid=(n_out // tn, m // tm),
        in_specs=in_specs,
        out_specs=pl.BlockSpec((tm, tn), lambda j, i: (i, j)),
        out_shape=jax.ShapeDtypeStruct((m, n_out), out_dtype),
        compiler_params=_params("parallel", "parallel"),
        name="matmul",
    )(*args)


def _conv_kernel(u_ref, buf_ref, w_ref, bdw_ref, g_ref, b_ref, act_ref, newbuf_ref, ext_ref, shift_ref, *, tt, dc):
    t = pl.program_id(1)
    hist = CONV_W - 1

    @pl.when(t == 0)
    def _():
        ext_ref[pl.ds(HALO - hist, hist), :] = buf_ref[0].astype(F32)

    @pl.when(t > 0)
    def _():
        ext_ref[pl.ds(0, HALO), :] = ext_ref[pl.ds(tt, HALO), :]

    a = u_ref[0, :, :dc].astype(F32)
    gt = u_ref[0, :, dc:].astype(F32)
    ext_ref[pl.ds(HALO, tt), :] = a * jax.nn.sigmoid(gt)

    n_ext = HALO + tt
    ext = ext_ref[...]
    for b in range(1, SUBLANES):
        shift_ref[b - 1] = pltpu.roll(ext, n_ext - b, axis=0)
    y = jnp.zeros((tt, dc), F32) + bdw_ref[...]
    for j in range(CONV_W):
        a, b = divmod(HALO - hist + j, SUBLANES)
        src = ext_ref if b == 0 else shift_ref.at[b - 1]
        y = y + src[pl.ds(a * SUBLANES, tt), :] * w_ref[pl.ds(j, 1), :]
    mu = jnp.mean(y, axis=-1, keepdims=True)
    var = jnp.mean(jnp.square(y - mu), axis=-1, keepdims=True)
    z = (y - mu) * lax.rsqrt(var + EPS) * g_ref[...] + b_ref[...]
    act_ref[0] = (z * jax.nn.sigmoid(z)).astype(act_ref.dtype)
    newbuf_ref[0] = ext_ref[pl.ds(HALO + tt - hist, hist), :]


def _conv_branch(u_in, conv_buf, w_dw, b_dw, ln_g, ln_b):
    bsz, t_len, two_dc = u_in.shape
    dc = two_dc // 2
    tt = _tile(t_len, 256)
    assert tt >= HALO and tt % SUBLANES == 0
    vec = lambda v: v.reshape(1, dc).astype(F32)
    return pl.pallas_call(
        functools.partial(_conv_kernel, tt=tt, dc=dc),
        grid=(bsz, t_len // tt),
        in_specs=[pl.BlockSpec((1, tt, two_dc), lambda b, t: (b, t, 0)),
                  pl.BlockSpec((1, CONV_W - 1, dc), lambda b, t: (b, 0, 0)),
                  pl.BlockSpec((CONV_W, dc), lambda b, t: (0, 0)),
                  pl.BlockSpec((1, dc), lambda b, t: (0, 0)),
                  pl.BlockSpec((1, dc), lambda b, t: (0, 0)),
                  pl.BlockSpec((1, dc), lambda b, t: (0, 0))],
        out_specs=[pl.BlockSpec((1, tt, dc), lambda b, t: (b, t, 0)),
                   pl.BlockSpec((1, CONV_W - 1, dc), lambda b, t: (b, 0, 0))],
        out_shape=[jax.ShapeDtypeStruct((bsz, t_len, dc), BF16),
                   jax.ShapeDtypeStruct((bsz, CONV_W - 1, dc), F32)],
        scratch_shapes=[pltpu.VMEM((HALO + tt, dc), F32), pltpu.VMEM((SUBLANES - 1, HALO + tt, dc), F32)],
        compiler_params=_params("parallel", "arbitrary"),
        name="conv_branch",
    )(u_in, conv_buf, w_dw.astype(F32), vec(b_dw), vec(ln_g), vec(ln_b))


ONES_ROWS = 16
ATTN_TQ = 1024
ATTN_TK_CACHED = 512
LOG2E = 1.4426950408889634


def _stacked_qt(q):
    tq = q.shape[0]
    qt = (q * (HEAD_DIM ** -0.5 * LOG2E)).T
    z = jnp.zeros((HEAD_DIM, tq), F32)
    return jnp.concatenate([jnp.concatenate([qt[:HEAD_DIM], z], axis=0),
                            jnp.concatenate([z, qt[HEAD_DIM:]], axis=0)], axis=1).astype(BF16)


def _vt_ones(v):
    vt = v.astype(F32).T.astype(BF16)
    return jnp.concatenate([vt, jnp.ones((ONES_ROWS, v.shape[0]), BF16)], axis=0)


def _scores(k_blk, qqt):
    return jnp.dot(k_blk, qqt, preferred_element_type=F32)


def _softmax_step(m_prev, acc_prev, s, vt_blk, mask):
    if mask is not None:
        s = jnp.where(mask, s, NEG)
    m_new = jnp.maximum(m_prev, jnp.max(s, axis=0, keepdims=True))
    alpha = jnp.exp2(m_prev - m_new)
    p = jnp.exp2(s - m_new).astype(BF16)
    return m_new, alpha * acc_prev + jnp.dot(vt_blk, p, preferred_element_type=F32)


def _chunk_mask(n_keys, key0, tq):
    q_chunk = lax.rem(lax.broadcasted_iota(jnp.int32, (n_keys, 2 * tq), 1), tq) // CHUNK
    k_chunk = (lax.broadcasted_iota(jnp.int32, (n_keys, 2 * tq), 0) + key0) // CHUNK
    return k_chunk <= q_chunk


def _diff_finalize(acc, lam, g_col, tq, lam_init):
    hd2 = 2 * HEAD_DIM
    inv_l = 1.0 / acc[hd2:hd2 + 1, :]
    ot = acc[:hd2, :tq] * inv_l[:, :tq] - lam * (acc[:hd2, tq:] * inv_l[:, tq:])
    ot = ot * lax.rsqrt(jnp.mean(ot * ot, axis=0, keepdims=True) + SUBLN_EPS) * g_col
    return (ot * (1.0 - lam_init)).T


def _lambda(lq1_ref, lk1_ref, lq2_ref, lk2_ref, lam_init):
    return (jnp.exp(jnp.sum(lq1_ref[...] * lk1_ref[...], axis=1, keepdims=True))
            - jnp.exp(jnp.sum(lq2_ref[...] * lk2_ref[...], axis=1, keepdims=True)) + lam_init)


def _diff_attn_kernel(q_ref, k_ref, v_ref, lq1_ref, lk1_ref, lq2_ref, lk2_ref, g_ref, o_ref, k_scr, vt_scr, acc_ref,
                      sa_ref, sb_ref, *, n_blk, tq, lam_init):
    qi = pl.program_id(2)

    @pl.when(qi == 0)
    def _():
        def body(i, carry):
            off = pl.multiple_of(i * tq, tq)
            k_scr[i] = k_ref[0, pl.ds(off, tq), :].astype(BF16)
            vt_scr[i] = _vt_ones(v_ref[0, pl.ds(off, tq), :])
            return carry
        lax.fori_loop(0, n_blk, body, 0)

    qqt = _stacked_qt(q_ref[0].astype(F32))
    acc_ref[...] = jnp.zeros(acc_ref.shape, F32)

    def consume(m_prev, s_ref, kb, mask):
        m_new, acc = _softmax_step(m_prev, acc_ref[...], s_ref[...], vt_scr[kb], mask)
        acc_ref[...] = acc
        return m_new

    def finish(m_prev, s_ref):
        consume(m_prev, s_ref, qi, _chunk_mask(tq, 0, tq))
        lam = _lambda(lq1_ref, lk1_ref, lq2_ref, lk2_ref, lam_init)
        o_ref[0] = _diff_finalize(acc_ref[...], lam, g_ref[...], tq, lam_init).astype(o_ref.dtype)

    sa_ref[...] = _scores(k_scr[0], qqt)

    def two_blocks(i2, m_prev):
        kb = 2 * i2
        sb_ref[...] = _scores(k_scr[kb + 1], qqt)
        m_mid = consume(m_prev, sa_ref, kb, None)
        sa_ref[...] = _scores(k_scr[kb + 2], qqt)
        return consume(m_mid, sb_ref, kb + 1, None)

    m = lax.fori_loop(0, qi // 2, two_blocks, jnp.full((1, 2 * tq), NEG, F32))

    @pl.when(lax.rem(qi, 2) == 1)
    def _():
        sb_ref[...] = _scores(k_scr[qi], qqt)
        finish(consume(m, sa_ref, qi - 1, None), sb_ref)

    @pl.when(lax.rem(qi, 2) == 0)
    def _():
        finish(m, sa_ref)


def _cached_attn_kernel(q_ref, kc_ref, vc_ref, kp_ref, vp_ref, lq1_ref, lk1_ref, lq2_ref, lk2_ref, g_ref, o_ref,
                        qq_scr, m_scr, acc_scr, *, n_heads, tq, tkp, lam_init):
    j = pl.program_id(1)
    hd2 = 2 * HEAD_DIM
    head = lambda h: slice(h * hd2, (h + 1) * hd2)

    @pl.when(j == 0)
    def _():
        for h in range(n_heads):
            qq_scr[h] = _stacked_qt(q_ref[0, :, head(h)].astype(F32))
        m_scr[...] = jnp.full(m_scr.shape, NEG, F32)
        acc_scr[...] = jnp.zeros(acc_scr.shape, F32)

    def update(h, k_blk, v_blk, mask):
        s = _scores(k_blk.astype(BF16), qq_scr[h])
        m_new, acc = _softmax_step(m_scr[h], acc_scr[h], s, _vt_ones(v_blk), mask)
        m_scr[h] = m_new
        acc_scr[h] = acc

    for h in range(n_heads):
        update(h, kp_ref[0, pl.ds(h, tkp, stride=n_heads), :], vp_ref[0, pl.ds(h, tkp, stride=n_heads), :], None)

    @pl.when(j == pl.num_programs(1) - 1)
    def _():
        lam = _lambda(lq1_ref, lk1_ref, lq2_ref, lk2_ref, lam_init)
        mask = _chunk_mask(tq, 0, tq)
        for h in range(n_heads):
            update(h, kc_ref[0, :, head(h)], vc_ref[0, :, head(h)], mask)
            o_ref[0, :, head(h)] = _diff_finalize(acc_scr[h], lam, g_ref[...], tq, lam_init).astype(o_ref.dtype)


def _lambda_args(lq1, lk1, lq2, lk2, subln_g):
    vec = lambda v: v.reshape(1, -1).astype(F32)
    return [vec(lq1), vec(lk1), vec(lq2), vec(lk2), subln_g.reshape(-1, 1).astype(F32)]


def _diff_attention(q, k_cur, v_cur, lq1, lk1, lq2, lk2, subln_g, lam_init):
    bsz, t_len, d_attn = q.shape
    hd2 = 2 * HEAD_DIM
    tq = _tile(t_len, ATTN_TQ)
    n_blk = t_len // tq
    assert tq % CHUNK == 0
    const = lambda shape: pl.BlockSpec(shape, lambda b, h, i: (0, 0))
    return pl.pallas_call(
        functools.partial(_diff_attn_kernel, n_blk=n_blk, tq=tq, lam_init=lam_init),
        grid=(bsz, d_attn // hd2, t_len // tq),
        in_specs=[pl.BlockSpec((1, tq, hd2), lambda b, h, i: (b, i, h)),
                  pl.BlockSpec((1, t_len, hd2), lambda b, h, i: (b, 0, h)),
                  pl.BlockSpec((1, t_len, hd2), lambda b, h, i: (b, 0, h))]
        + [const((1, HEAD_DIM))] * 4 + [const((hd2, 1))],
        out_specs=pl.BlockSpec((1, tq, hd2), lambda b, h, i: (b, i, h)),
        out_shape=jax.ShapeDtypeStruct((bsz, t_len, d_attn), BF16),
        scratch_shapes=[pltpu.VMEM((n_blk, tq, hd2), BF16), pltpu.VMEM((n_blk, hd2 + ONES_ROWS, tq), BF16),
                        pltpu.VMEM((hd2 + ONES_ROWS, 2 * tq), F32),
                        pltpu.VMEM((tq, 2 * tq), F32), pltpu.VMEM((tq, 2 * tq), F32)],
        compiler_params=_params("parallel", "parallel", "arbitrary"),
        name="diff_attention",
    )(q, k_cur, v_cur, *_lambda_args(lq1, lk1, lq2, lk2, subln_g))


def _cached_diff_attention(q, k_cur, v_cur, k_past, v_past, lq1, lk1, lq2, lk2, subln_g, lam_init):
    bsz, t_len, d_attn = q.shape
    hd2 = 2 * HEAD_DIM
    past_len, n_heads = k_past.shape[1], k_past.shape[2]
    tq = t_len
    tkp = _tile(past_len, ATTN_TK_CACHED)
    assert tq % CHUNK == 0 and past_len % CHUNK == 0 and tq <= ATTN_TQ and n_heads * hd2 == d_attn
    assert n_heads % SUBLANES == 0
    flat = lambda c: c.reshape(bsz, past_len * n_heads, hd2)
    const = lambda shape: pl.BlockSpec(shape, lambda b, j: (0, 0))
    rows = pl.BlockSpec((1, t_len, d_attn), lambda b, j: (b, 0, 0))
    hist = pl.BlockSpec((1, tkp * n_heads, hd2), lambda b, j: (b, j, 0))
    return pl.pallas_call(
        functools.partial(_cached_attn_kernel, n_heads=n_heads, tq=tq, tkp=tkp, lam_init=lam_init),
        grid=(bsz, past_len // tkp),
        in_specs=[rows, rows, rows, hist, hist] + [const((1, HEAD_DIM))] * 4 + [const((hd2, 1))],
        out_specs=rows,
        out_shape=jax.ShapeDtypeStruct((bsz, t_len, d_attn), BF16),
        scratch_shapes=[pltpu.VMEM((n_heads, hd2, 2 * tq), BF16), pltpu.VMEM((n_heads, 1, 2 * tq), F32),
                        pltpu.VMEM((n_heads, hd2 + ONES_ROWS, 2 * tq), F32)],
        compiler_params=_params("parallel", "arbitrary"),
        name="cached_diff_attention",
    )(q, k_cur, v_cur, flat(k_past), flat(v_past), *_lambda_args(lq1, lk1, lq2, lk2, subln_g))


def _mem_attn_kernel(q_ref, k_ref, v_ref, o_ref):
    scale = jnp.asarray(MEM_HEAD_DIM ** -0.5, q_ref.dtype)
    for h in range(MEM_HEADS):
        sl = slice(h * MEM_HEAD_DIM, (h + 1) * MEM_HEAD_DIM)
        q = q_ref[0, :, sl] * scale
        s = lax.dot_general(q, k_ref[0, :, sl].astype(BF16), (((1,), (1,)), ((), ())), preferred_element_type=F32)
        p = jnp.exp(s - jnp.max(s, axis=1, keepdims=True))
        p = p / jnp.sum(p, axis=1, keepdims=True)
        o = jnp.dot(p.astype(BF16), v_ref[0, :, sl].astype(BF16), preferred_element_type=F32)
        o_ref[0, :, sl] = o.astype(o_ref.dtype)


def _mem_attention(mq, mk, mv):
    bsz, t_len, d_mem = mq.shape
    n_mem = mk.shape[1]
    tt = _tile(t_len, 512)
    return pl.pallas_call(
        _mem_attn_kernel,
        grid=(bsz, t_len // tt),
        in_specs=[pl.BlockSpec((1, tt, d_mem), lambda b, t: (b, t, 0)),
                  pl.BlockSpec((1, n_mem, d_mem), lambda b, t: (b, 0, 0)),
                  pl.BlockSpec((1, n_mem, d_mem), lambda b, t: (b, 0, 0))],
        out_specs=pl.BlockSpec((1, tt, d_mem), lambda b, t: (b, t, 0)),
        out_shape=jax.ShapeDtypeStruct((bsz, t_len, d_mem), BF16),
        compiler_params=_params("parallel", "parallel"),
        name="mem_attention",
    )(mq, mk, mv)


def _to_slabs(dst_ref, val, rows, n_slab):
    for c in range(n_slab):
        dst_ref[pl.ds(c, rows, stride=n_slab), :] = val[:, c * LANES:(c + 1) * LANES].astype(dst_ref.dtype)


def _slab_chunk(src_ref, c, rows, n_slab, row0=0):
    return src_ref[pl.ds(row0 * n_slab + c, rows, stride=n_slab), :]


def _router_kernel(*refs, tr, n_slab, group_tiles):
    n_grp = len(group_tiles)
    x_refs = refs[:n_grp]
    g_ref, whi_ref, wlo_ref, b_ref, hn_ref, eid_ref, wt_ref, rank_ref, cnt_ref, run_ref = refs[n_grp:]
    t = pl.program_id(0)
    x = x_refs[-1][...]
    first_tile = sum(group_tiles[:-1])
    for gi in range(n_grp - 2, -1, -1):
        x = jnp.where(t < first_tile, x_refs[gi][...], x)
        first_tile -= group_tiles[gi]
    hn = x * lax.rsqrt(jnp.mean(x * x, axis=-1, keepdims=True) + EPS) * g_ref[...]
    _to_slabs(hn_ref, hn, tr, n_slab)
    h_hi = hn.astype(BF16)
    h_lo = (hn - h_hi.astype(F32)).astype(BF16)
    logits = (jnp.dot(h_hi, whi_ref[...], preferred_element_type=F32)
              + jnp.dot(h_lo, whi_ref[...], preferred_element_type=F32)
              + jnp.dot(h_hi, wlo_ref[...], preferred_element_type=F32)) + b_ref[...]
    lane = lax.broadcasted_iota(jnp.int32, logits.shape, 1)
    first = lambda hit: jnp.min(jnp.where(hit, lane, LANES), axis=1, keepdims=True)

    gl = jnp.where(lane < N_GROUPS, logits, -jnp.inf)
    gmax = jnp.max(gl, axis=1, keepdims=True)
    g_idx = first(gl == gmax)
    p_g = 1.0 / jnp.sum(jnp.exp(gl - gmax), axis=1, keepdims=True)

    lo = N_GROUPS + EXPERTS_PER_GROUP * g_idx
    in_grp = (lane >= lo) & (lane < lo + EXPERTS_PER_GROUP)
    el = jnp.where(in_grp, logits, -jnp.inf)
    pe = jnp.exp(el - jnp.max(el, axis=1, keepdims=True))
    probs = jnp.where(in_grp, pe / jnp.sum(pe, axis=1, keepdims=True), -1.0)
    v1 = jnp.max(probs, axis=1, keepdims=True)
    i1 = first(probs == v1)
    probs2 = jnp.where(lane == i1, -1.0, probs)
    v2 = jnp.max(probs2, axis=1, keepdims=True)
    i2 = first(probs2 == v2)
    tot = v1 + v2
    eid_ref[:, 0:1] = i1 - N_GROUPS
    eid_ref[:, 1:2] = i2 - N_GROUPS
    wt_ref[:, 0:1] = p_g * (v1 / tot)
    wt_ref[:, 1:2] = p_g * (v2 / tot)

    @pl.when(t == 0)
    def _():
        run_ref[...] = jnp.zeros(run_ref.shape, F32)

    sel1 = lane == i1
    sel2 = lane == i2
    oh1 = jnp.where(sel1, 1.0, 0.0).astype(BF16)
    oh2 = jnp.where(sel2, 1.0, 0.0).astype(BF16)
    earlier = (lax.broadcasted_iota(jnp.int32, (tr, tr), 1) < lax.broadcasted_iota(jnp.int32, (tr, tr), 0))
    tri = jnp.where(earlier, 1.0, 0.0).astype(BF16)
    run = run_ref[...]
    cnt1 = jnp.sum(oh1.astype(F32), axis=0, keepdims=True)
    cnt2 = jnp.sum(oh2.astype(F32), axis=0, keepdims=True)
    before1 = jnp.dot(tri, oh1, preferred_element_type=F32) + run
    before2 = jnp.dot(tri, oh2, preferred_element_type=F32) + (run + cnt1)
    rank_ref[:, 0:1] = jnp.sum(jnp.where(sel1, before1, 0.0), axis=1, keepdims=True).astype(jnp.int32)
    rank_ref[:, 1:2] = jnp.sum(jnp.where(sel2, before2, 0.0), axis=1, keepdims=True).astype(jnp.int32)
    run = run + cnt1 + cnt2
    run_ref[...] = run
    cnt_ref[...] = run.astype(jnp.int32)


def _router(xs, g, w_rg, b_rg, w_re, b_re):
    d = xs[0].shape[1]
    n = sum(x.shape[0] for x in xs)
    n_slab = d // LANES
    tr = 256
    assert all(x.shape[0] % tr == 0 for x in xs)
    group_tiles = tuple(x.shape[0] // tr for x in xs)
    x_specs = []
    for gi, nt in enumerate(group_tiles):
        t0 = sum(group_tiles[:gi])
        x_specs.append(pl.BlockSpec((tr, d), lambda i, t0=t0, nt=nt: (jnp.clip(i - t0, 0, nt - 1), 0)))
    w = jnp.zeros((d, LANES), F32).at[:, :N_GROUPS].set(w_rg).at[:, N_GROUPS:N_GROUPS + N_EXPERTS].set(w_re)
    b = jnp.zeros((1, LANES), F32).at[0, :N_GROUPS].set(b_rg).at[0, N_GROUPS:N_GROUPS + N_EXPERTS].set(b_re)
    w_hi = w.astype(BF16)
    w_lo = (w - w_hi.astype(F32)).astype(BF16)
    small = lambda dt: jax.ShapeDtypeStruct((n, TOP_K), dt)
    small_spec = pl.BlockSpec((tr, TOP_K), lambda i: (i, 0))
    return pl.pallas_call(
        functools.partial(_router_kernel, tr=tr, n_slab=n_slab, group_tiles=group_tiles),
        grid=(n // tr,),
        in_specs=x_specs + [
                  pl.BlockSpec((1, d), lambda i: (0, 0)),
                  pl.BlockSpec((d, LANES), lambda i: (0, 0)),
                  pl.BlockSpec((d, LANES), lambda i: (0, 0)),
                  pl.BlockSpec((1, LANES), lambda i: (0, 0))],
        out_specs=[pl.BlockSpec((tr * n_slab, LANES), lambda i: (i, 0)), small_spec, small_spec, small_spec,
                   pl.BlockSpec((1, LANES), lambda i: (0, 0))],
        out_shape=[jax.ShapeDtypeStruct((n * n_slab, LANES), F32), small(jnp.int32), small(F32), small(jnp.int32),
                   jax.ShapeDtypeStruct((1, LANES), jnp.int32)],
        scratch_shapes=[pltpu.VMEM((1, LANES), F32)],
        compiler_params=_params("arbitrary"),
        name="moe_router",
    )(*xs, g.reshape(1, d).astype(F32), w_hi, w_lo, b)


def _dispatch_kernel(pos_ref, zstart_ref, zvalid_ref, n_active_ref, hn_ref, xs_out, zbuf, sem, zsem,
                     *, td, tm, n_slab, n_tiles):
    t = pl.program_id(0)
    zrows = tm * n_slab

    def zero_tile(row0):
        return pltpu.make_async_copy(zbuf, xs_out.at[pl.ds(pl.multiple_of(row0 * n_slab, n_slab), zrows), :], zsem)

    @pl.when(t == 0)
    def _():
        zbuf[...] = jnp.zeros(zbuf.shape, zbuf.dtype)

        def for_each_zero_tile(act):
            for e in range(N_EXPERTS):
                @pl.when(zvalid_ref[e] > 0)
                def _():
                    act(zero_tile(zstart_ref[e]))

            def unused(i, carry):
                act(zero_tile(i * tm))
                return carry
            lax.fori_loop(n_active_ref[0], n_tiles, unused, 0)

        for_each_zero_tile(lambda copy: copy.start())
        for_each_zero_tile(lambda copy: copy.wait())

    def body(r, carry):
        src = hn_ref.at[pl.ds(pl.multiple_of(r * n_slab, n_slab), n_slab), :]
        for kk in range(TOP_K):
            dst_row = pl.multiple_of(pos_ref[(t * td + r) * TOP_K + kk] * n_slab, n_slab)
            pltpu.make_async_copy(src, xs_out.at[pl.ds(dst_row, n_slab), :], sem).start()
        return carry
    lax.fori_loop(0, td, body, 0)
    for kk in range(TOP_K):
        pltpu.make_async_copy(hn_ref, xs_out.at[pl.ds(0, td * n_slab), :], sem).wait()


def _expert_kernel(tile_expert_ref, n_active_ref, x_ref, wg_ref, wu_ref, wd_ref, ys_ref, xb_ref, *, tm, n_slab):
    del tile_expert_ref
    active = pl.program_id(0) < n_active_ref[0]

    @pl.when(jnp.logical_not(active))
    def _():
        ys_ref[...] = jnp.zeros(ys_ref.shape, ys_ref.dtype)

    @pl.when(active)
    def _():
        for c in range(n_slab):
            xb_ref[:, c * LANES:(c + 1) * LANES] = _slab_chunk(x_ref, c, tm, n_slab).astype(BF16)
        x = xb_ref[...]
        g = jnp.dot(x, wg_ref[0], preferred_element_type=F32)
        u = jnp.dot(x, wu_ref[0], preferred_element_type=F32)
        act = (g * jax.nn.sigmoid(g) * u).astype(BF16)
        _to_slabs(ys_ref, jnp.dot(act, wd_ref[0], preferred_element_type=F32), tm, n_slab)


def _start_slab_gather(pos_ref, tok0, ys_hbm, ybuf, sem, slot, tc, n_slab):
    def body(r, carry):
        for kk in range(TOP_K):
            src_row = pl.multiple_of(pos_ref[(tok0 + r) * TOP_K + kk] * n_slab, n_slab)
            dst_row = pl.multiple_of((kk * tc + r) * n_slab, n_slab)
            pltpu.make_async_copy(ys_hbm.at[pl.ds(src_row, n_slab), :],
                                  ybuf.at[slot, pl.ds(dst_row, n_slab), :], sem.at[slot]).start()
        return carry
    lax.fori_loop(0, tc, body, 0)


def _combine_kernel(pos_ref, x_ref, w_ref, g_ref, ys_hbm, o_ref, ybuf, sem, *, tc, tok_base, n_slab):
    t = pl.program_id(0)
    nt = pl.num_programs(0)
    slot = lax.rem(t, 2)

    @pl.when(t == 0)
    def _():
        _start_slab_gather(pos_ref, tok_base, ys_hbm, ybuf, sem, 0, tc, n_slab)

    @pl.when(t + 1 < nt)
    def _():
        _start_slab_gather(pos_ref, tok_base + (t + 1) * tc, ys_hbm, ybuf, sem, 1 - slot, tc, n_slab)

    pltpu.make_async_copy(ys_hbm.at[pl.ds(0, TOP_K * tc * n_slab), :], ybuf.at[slot], sem.at[slot]).wait()
    yb = ybuf.at[slot]
    wk = [jnp.broadcast_to(w_ref[:, kk:kk + 1], (tc, LANES)) for kk in range(TOP_K)]
    sq = jnp.zeros((tc, LANES), F32)
    for c in range(n_slab):
        moe = wk[0] * _slab_chunk(yb, c, tc, n_slab)
        for kk in range(1, TOP_K):
            moe = moe + wk[kk] * _slab_chunk(yb, c, tc, n_slab, row0=kk * tc)
        y = x_ref[:, c * LANES:(c + 1) * LANES] + moe
        o_ref[:, c * LANES:(c + 1) * LANES] = y
        sq = sq + y * y
    d = n_slab * LANES
    scale = lax.rsqrt(jnp.sum(sq, axis=1, keepdims=True) / d + EPS)
    o_ref[...] = o_ref[...] * scale * g_ref[...]


def _hier_moe_final(xs, norm_ffn, w_rg, b_rg, w_re, b_re, w_g, w_u, w_d, norm_final):
    d = xs[0].shape[1]
    n = sum(x.shape[0] for x in xs)
    n_slab = d // LANES
    d_exp = w_g.shape[2]
    hn, eid, wts, rank, cnt = _router(xs, norm_ffn, w_rg, b_rg, w_re, b_re)

    tm = 256
    na = n * TOP_K
    n_tiles = (na + N_EXPERTS * (tm - 1)) // tm + 1
    n_rows = n_tiles * tm
    counts = cnt[0, N_GROUPS:N_GROUPS + N_EXPERTS]
    padded = ((counts + tm - 1) // tm) * tm
    pend = jnp.cumsum(padded)
    pstart = pend - padded
    experts = jnp.arange(N_EXPERTS, dtype=jnp.int32)
    pos = (jnp.sum(jnp.where(eid[:, :, None] == experts, pstart, 0), axis=-1) + rank).reshape(na)
    tile_start = jnp.arange(n_tiles, dtype=jnp.int32) * tm
    tile_expert = jnp.minimum(jnp.sum((pend[None, :] <= tile_start[:, None]).astype(jnp.int32), axis=1), N_EXPERTS - 1)
    n_active = (pend[-1] // tm).astype(jnp.int32).reshape(1)
    zstart = jnp.maximum(pend - tm, 0).astype(jnp.int32)
    zvalid = (counts > 0).astype(jnp.int32)

    td = _tile(n, 256)
    x_sorted = pl.pallas_call(
        functools.partial(_dispatch_kernel, td=td, tm=tm, n_slab=n_slab, n_tiles=n_tiles),
        grid_spec=pltpu.PrefetchScalarGridSpec(
            num_scalar_prefetch=4,
            grid=(n // td,),
            in_specs=[pl.BlockSpec((td * n_slab, LANES), lambda t, p, zs, zv, na_: (t, 0))],
            out_specs=pl.BlockSpec(memory_space=pl.ANY),
            scratch_shapes=[pltpu.VMEM((tm * n_slab, LANES), F32), pltpu.SemaphoreType.DMA(()),
                            pltpu.SemaphoreType.DMA(())]),
        out_shape=jax.ShapeDtypeStruct((n_rows * n_slab, LANES), F32),
        compiler_params=_params("arbitrary"),
        name="moe_dispatch",
    )(pos, zstart, zvalid, n_active, hn)

    active = lambda t, te, na_: (jnp.minimum(t, na_[0] - 1), 0)
    ys = pl.pallas_call(
        functools.partial(_expert_kernel, tm=tm, n_slab=n_slab),
        grid_spec=pltpu.PrefetchScalarGridSpec(
            num_scalar_prefetch=2,
            grid=(n_tiles,),
            in_specs=[pl.BlockSpec((tm * n_slab, LANES), active),
                      pl.BlockSpec((1, d, d_exp), lambda t, te, na_: (te[t], 0, 0)),
                      pl.BlockSpec((1, d, d_exp), lambda t, te, na_: (te[t], 0, 0)),
                      pl.BlockSpec((1, d_exp, d), lambda t, te, na_: (te[t], 0, 0))],
            out_specs=pl.BlockSpec((tm * n_slab, LANES), lambda t, te, na_: (t, 0)),
            scratch_shapes=[pltpu.VMEM((tm, d), BF16)]),
        out_shape=jax.ShapeDtypeStruct((n_rows * n_slab, LANES), F32),
        compiler_params=_params("arbitrary"),
        name="moe_experts",
    )(tile_expert, n_active, x_sorted, w_g, w_u, w_d)

    outs = []
    row0 = 0
    for x in xs:
        rows = x.shape[0]
        tc = _tile(rows, 256)
        assert row0 % tc == 0
        blk0 = row0 // tc
        outs.append(pl.pallas_call(
            functools.partial(_combine_kernel, tc=tc, tok_base=row0, n_slab=n_slab),
            grid_spec=pltpu.PrefetchScalarGridSpec(
                num_scalar_prefetch=1,
                grid=(rows // tc,),
                in_specs=[pl.BlockSpec((tc, d), lambda t, p: (t, 0)),
                          pl.BlockSpec((tc, TOP_K), lambda t, p, blk0=blk0: (blk0 + t, 0)),
                          pl.BlockSpec((1, d), lambda t, p: (0, 0)),
                          pl.BlockSpec(memory_space=pl.ANY)],
                out_specs=pl.BlockSpec((tc, d), lambda t, p: (t, 0)),
                scratch_shapes=[pltpu.VMEM((2, TOP_K * tc * n_slab, LANES), F32), pltpu.SemaphoreType.DMA((2,))]),
            out_shape=jax.ShapeDtypeStruct((rows, d), F32),
            compiler_params=_params("arbitrary"),
            name="moe_combine",
        )(pos, x, wts, norm_final.reshape(1, d).astype(F32), ys))
        row0 += rows
    return outs


def _mixer(x, conv_buf, k_past, v_past, mk, mv, lam_init, w):
    bsz, t_len, d = x.shape
    n = bsz * t_len
    d_conv = w["w_dw"].shape[1]
    d_attn = w["w_attn_out"].shape[0]
    d_mem = w["w_mem_out"].shape[0]
    c0 = 2 * d_conv
    c1, c2, c3 = c0 + d_attn, c0 + 2 * d_attn, c0 + 3 * d_attn
    c4 = c3 + d_mem
    x2d = x.reshape(n, d)

    h = _rmsnorm(x2d, w["norm_mix"], BF16)
    w_in = w["w_in"]
    u_in = _matmul(h, w_in, F32, b_col0=0, n_out=c0)
    q = _matmul(h, w_in, F32, b_col0=c0, n_out=d_attn)
    k_rows = _matmul(h, w_in, F32, b_col0=c1, n_out=d_attn)
    v_rows = _matmul(h, w_in, F32, b_col0=c2, n_out=d_attn)
    mq = _matmul(h, w_in, BF16, b_col0=c3, n_out=d_mem)
    gates = _matmul(h, w_in, BF16, b_col0=c4, n_out=N_BRANCH * d)

    conv_act, new_buf = _conv_branch(u_in.reshape(bsz, t_len, c0), conv_buf, w["w_dw"], w["b_dw"],
                                     w["conv_ln_g"], w["conv_ln_b"])
    qkv = [a.reshape(bsz, t_len, d_attn) for a in (q, k_rows, v_rows)]
    lam_args = (w["lambda_q1"], w["lambda_k1"], w["lambda_q2"], w["lambda_k2"], w["subln_g"], lam_init)
    if k_past is None:
        o = _diff_attention(*qkv, *lam_args)
    else:
        o = _cached_diff_attention(*qkv, k_past, v_past, *lam_args)
    mo = _mem_attention(mq.reshape(bsz, t_len, d_mem), mk, mv)

    merged = _matmul(conv_act.reshape(n, d_conv), w["w_conv_out"], BF16, gate=gates, gate_col0=0)
    merged = _matmul(o.reshape(n, d_attn), w["w_attn_out"], BF16, gate=gates, gate_col0=d, add=merged)
    merged = _matmul(mo.reshape(n, d_mem), w["w_mem_out"], BF16, gate=gates, gate_col0=2 * d, add=merged)
    x2 = _matmul(merged, w["w_out"], F32, add=x2d)
    hd2 = 2 * HEAD_DIM
    return x2, new_buf, k_rows.reshape(bsz, t_len, N_HEADS, hd2), v_rows.reshape(bsz, t_len, N_HEADS, hd2)


def kernel(x_prompt, x_sample, mem_prompt, cache_conv, cache_diff_k, cache_diff_v, cache_mem_k, cache_mem_v, norm_mix, w_in, w_dw, b_dw, conv_ln_g, conv_ln_b, w_conv_out, lambda_q1, lambda_k1, lambda_q2, lambda_k2, subln_g, w_attn_out, norm_mem, w_mem_k, w_mem_v, w_mem_out, w_out, norm_ffn, w_router_grp, b_router_grp, w_router_exp, b_router_exp, w_exp_gate, w_exp_up, w_exp_down, norm_final):
    depth = w_in.shape[0]
    assert depth == 1, "the final rmsnorm is fused into the last layer's MoE combine"
    bp, tp, d = x_prompt.shape
    bs, ts, _ = x_sample.shape
    n_mem = mem_prompt.shape[1]
    d_conv = w_dw.shape[2]
    d_mem = w_mem_k.shape[2]
    l = 0
    lam_init = 0.8 - 0.6 * math.exp(-0.3 * l)
    w = dict(norm_mix=norm_mix[l], w_in=w_in[l].astype(BF16), w_dw=w_dw[l], b_dw=b_dw[l], conv_ln_g=conv_ln_g[l],
             conv_ln_b=conv_ln_b[l], w_conv_out=w_conv_out[l].astype(BF16), lambda_q1=lambda_q1[l],
             lambda_k1=lambda_k1[l], lambda_q2=lambda_q2[l], lambda_k2=lambda_k2[l], subln_g=subln_g[l],
             w_attn_out=w_attn_out[l].astype(BF16), w_mem_out=w_mem_out[l].astype(BF16), w_out=w_out[l].astype(BF16))

    hm = _rmsnorm(mem_prompt.reshape(bp * n_mem, d), norm_mem[l], BF16)
    mk = _matmul(hm, w_mem_k[l].astype(BF16), F32).reshape(bp, n_mem, d_mem)
    mv = _matmul(hm, w_mem_v[l].astype(BF16), F32).reshape(bp, n_mem, d_mem)
    conv0 = jnp.zeros((bp, CONV_W - 1, d_conv), F32)
    x2_p, cb_p, k_p, v_p = _mixer(x_prompt, conv0, None, None, mk, mv, lam_init, w)

    x2_s, cb_s, k_s, v_s = _mixer(x_sample, cache_conv.reshape(cache_conv.shape[1:]),
                                  cache_diff_k.reshape(cache_diff_k.shape[1:]),
                                  cache_diff_v.reshape(cache_diff_v.shape[1:]),
                                  cache_mem_k.reshape(bs, n_mem, d_mem), cache_mem_v.reshape(bs, n_mem, d_mem),
                                  lam_init, w)

    y_p, y_s = _hier_moe_final([x2_p, x2_s], norm_ffn[l], w_router_grp[l], b_router_grp[l], w_router_exp[l],
                               b_router_exp[l], w_exp_gate[l].astype(BF16), w_exp_up[l].astype(BF16),
                               w_exp_down[l].astype(BF16), norm_final)
    mem_shape = (1, bp, n_mem, MEM_HEADS, MEM_HEAD_DIM)
    return (y_p.reshape(bp, tp, d), y_s.reshape(bs, ts, d), cb_p[None], k_p[None], v_p[None],
            mk.reshape(mem_shape), mv.reshape(mem_shape), cb_s[None], k_s[None], v_s[None])
```

```python
import functools
import math

import jax
import jax.numpy as jnp
from jax import lax
from jax.experimental import pallas as pl
from jax.experimental.pallas import tpu as pltpu

F32 = jnp.float32
BF16 = jnp.bfloat16

CHUNK = 64
EPS = 1e-6
SUBLN_EPS = 1e-5
NEG = -1e30
CONV_W = 31
N_HEADS = 16
HEAD_DIM = 64
MEM_HEADS = 4
MEM_HEAD_DIM = 256
N_BRANCH = 3
N_GROUPS = 4
EXPERTS_PER_GROUP = 8
N_EXPERTS = N_GROUPS * EXPERTS_PER_GROUP
TOP_K = 2

LANES = 128
SUBLANES = 8
VMEM_LIMIT_BYTES = 56 * 1024 * 1024
HALO = 32


def _params(*semantics):
    return pltpu.CompilerParams(dimension_semantics=semantics, vmem_limit_bytes=VMEM_LIMIT_BYTES)


def _tile(n, pref):
    if n <= pref:
        return n
    t = pref
    while n % t:
        t //= 2
    return t


def _rmsnorm_kernel(x_ref, g_ref, o_ref, *, eps):
    x = x_ref[...].astype(F32)
    ms = jnp.mean(x * x, axis=-1, keepdims=True)
    o_ref[...] = (x * lax.rsqrt(ms + eps) * g_ref[...]).astype(o_ref.dtype)


def _rmsnorm(x, g, out_dtype, eps=EPS):
    n, d = x.shape
    tr = _tile(n, 512)
    return pl.pallas_call(
        functools.partial(_rmsnorm_kernel, eps=eps),
        grid=(n // tr,),
        in_specs=[pl.BlockSpec((tr, d), lambda i: (i, 0)), pl.BlockSpec((1, d), lambda i: (0, 0))],
        out_specs=pl.BlockSpec((tr, d), lambda i: (i, 0)),
        out_shape=jax.ShapeDtypeStruct((n, d), out_dtype),
        compiler_params=_params("parallel"),
        name="rmsnorm",
    )(x, g.reshape(1, d).astype(F32))


def _mm_kernel(*refs, has_gate, has_add):
    a_ref, b_ref = refs[0], refs[1]
    o_ref = refs[-1]
    acc = jnp.dot(a_ref[...], b_ref[...], preferred_element_type=F32)
    pos = 2
    if has_gate:
        acc = acc * jax.nn.sigmoid(refs[pos][...].astype(F32))
        pos += 1
    if has_add:
        acc = acc + refs[pos][...].astype(F32)
    o_ref[...] = acc.astype(o_ref.dtype)


def _matmul(a, b, out_dtype, *, b_col0=0, n_out=None, gate=None, gate_col0=0, add=None):
    m, k = a.shape
    n_out = b.shape[1] if n_out is None else n_out
    tm = _tile(m, 1024)
    tn = _tile(n_out, 1024)
    assert b_col0 % tn == 0 and gate_col0 % tn == 0
    bj0, gj0 = b_col0 // tn, gate_col0 // tn
    in_specs = [pl.BlockSpec((tm, k), lambda j, i: (i, 0)),
                pl.BlockSpec((k, tn), lambda j, i: (0, bj0 + j))]
    args = [a, b]
    if gate is not None:
        in_specs.append(pl.BlockSpec((tm, tn), lambda j, i: (i, gj0 + j)))
        args.append(gate)
    if add is not None:
        in_specs.append(pl.BlockSpec((tm, tn), lambda j, i: (i, j)))
        args.append(add)
    return pl.pallas_call(
        functools.partial(_mm_kernel, has_gate=gate is not None, has_add=add is not None),
        grid=(n_out // tn, m // tm),
        in_specs=in_specs,
        out_specs=pl.BlockSpec((tm, tn), lambda j, i: (i, j)),
        out_shape=jax.ShapeDtypeStruct((m, n_out), out_dtype),
        compiler_params=_params("parallel", "parallel"),
        name="matmul",
    )(*args)


def _conv_kernel(u_ref, buf_ref, w_ref, bdw_ref, g_ref, b_ref, act_ref, newbuf_ref, ext_ref, shift_ref, *, tt, dc):
    t = pl.program_id(1)
    hist = CONV_W - 1

    @pl.when(t == 0)
    def _():
        ext_ref[pl.ds(HALO - hist, hist), :] = buf_ref[0].astype(F32)

    @pl.when(t > 0)
    def _():
        ext_ref[pl.ds(0, HALO), :] = ext_ref[pl.ds(tt, HALO), :]

    a = u_ref[0, :, :dc].astype(F32)
    gt = u_ref[0, :, dc:].astype(F32)
    ext_ref[pl.ds(HALO, tt), :] = a * jax.nn.sigmoid(gt)

    n_ext = HALO + tt
    ext = ext_ref[...]
    for b in range(1, SUBLANES):
        shift_ref[b - 1] = pltpu.roll(ext, n_ext - b, axis=0)
    y = jnp.zeros((tt, dc), F32) + bdw_ref[...]
    for j in range(CONV_W):
        a, b = divmod(HALO - hist + j, SUBLANES)
        src = ext_ref if b == 0 else shift_ref.at[b - 1]
        y = y + src[pl.ds(a * SUBLANES, tt), :] * w_ref[pl.ds(j, 1), :]
    mu = jnp.mean(y, axis=-1, keepdims=True)
    var = jnp.mean(jnp.square(y - mu), axis=-1, keepdims=True)
    z = (y - mu) * lax.rsqrt(var + EPS) * g_ref[...] + b_ref[...]
    act_ref[0] = (z * jax.nn.sigmoid(z)).astype(act_ref.dtype)
    newbuf_ref[0] = ext_ref[pl.ds(HALO + tt - hist, hist), :]


def _conv_branch(u_in, conv_buf, w_dw, b_dw, ln_g, ln_b):
    bsz, t_len, two_dc = u_in.shape
    dc = two_dc // 2
    tt = _tile(t_len, 256)
    assert tt >= HALO and tt % SUBLANES == 0
    vec = lambda v: v.reshape(1, dc).astype(F32)
    return pl.pallas_call(
        functools.partial(_conv_kernel, tt=tt, dc=dc),
        grid=(bsz, t_len // tt),
        in_specs=[pl.BlockSpec((1, tt, two_dc), lambda b, t: (b, t, 0)),
                  pl.BlockSpec((1, CONV_W - 1, dc), lambda b, t: (b, 0, 0)),
                  pl.BlockSpec((CONV_W, dc), lambda b, t: (0, 0)),
                  pl.BlockSpec((1, dc), lambda b, t: (0, 0)),
                  pl.BlockSpec((1, dc), lambda b, t: (0, 0)),
                  pl.BlockSpec((1, dc), lambda b, t: (0, 0))],
        out_specs=[pl.BlockSpec((1, tt, dc), lambda b, t: (b, t, 0)),
                   pl.BlockSpec((1, CONV_W - 1, dc), lambda b, t: (b, 0, 0))],
        out_shape=[jax.ShapeDtypeStruct((bsz, t_len, dc), BF16),
                   jax.ShapeDtypeStruct((bsz, CONV_W - 1, dc), F32)],
        scratch_shapes=[pltpu.VMEM((HALO + tt, dc), F32), pltpu.VMEM((SUBLANES - 1, HALO + tt, dc), F32)],
        compiler_params=_params("parallel", "arbitrary"),
        name="conv_branch",
    )(u_in, conv_buf, w_dw.astype(F32), vec(b_dw), vec(ln_g), vec(ln_b))


ONES_ROWS = 16
ATTN_TQ = 1024
ATTN_LANE_GROUP = 512
ATTN_TK_CACHED = 512
ATTN_HEAD_GROUP = 4
LOG2E = 1.4426950408889634


def _stacked_qt(q):
    tq = q.shape[0]
    qt = (q * (HEAD_DIM ** -0.5 * LOG2E)).T
    z = jnp.zeros((HEAD_DIM, tq), F32)
    return jnp.concatenate([jnp.concatenate([qt[:HEAD_DIM], z], axis=0),
                            jnp.concatenate([z, qt[HEAD_DIM:]], axis=0)], axis=1).astype(BF16)


def _vt_ones(v):
    vt = v.astype(F32).T.astype(BF16)
    return jnp.concatenate([vt, jnp.ones((ONES_ROWS, v.shape[0]), BF16)], axis=0)


def _scores(k_blk, qqt):
    return jnp.dot(k_blk, qqt, preferred_element_type=F32)


def _softmax_step(m_prev, acc_prev, s, vt_blk):
    m_new = jnp.maximum(m_prev, jnp.max(s, axis=0, keepdims=True))
    alpha = jnp.exp2(m_prev - m_new)
    p = jnp.exp2(s - m_new).astype(BF16)
    return m_new, alpha * acc_prev + jnp.dot(vt_blk, p, preferred_element_type=F32)


def _mask_aligned_block(s, lane0, tq):
    q_chunk = lax.rem(lax.broadcasted_iota(jnp.int32, (1, s.shape[1]), 1) + lane0, tq) // CHUNK
    bands = [jnp.where(q_chunk >= c, s[c * CHUNK:(c + 1) * CHUNK, :], NEG) for c in range(s.shape[0] // CHUNK)]
    return jnp.concatenate(bands, axis=0)


def _diff_finalize(acc, lam, g_col, tq, lam_init):
    hd2 = 2 * HEAD_DIM
    inv_l = 1.0 / acc[hd2:hd2 + 1, :]
    ot = acc[:hd2, :tq] * inv_l[:, :tq] - lam * (acc[:hd2, tq:] * inv_l[:, tq:])
    ot = ot * lax.rsqrt(jnp.mean(ot * ot, axis=0, keepdims=True) + SUBLN_EPS) * g_col
    return (ot * (1.0 - lam_init)).T


def _lambda(lq1_ref, lk1_ref, lq2_ref, lk2_ref, lam_init):
    return (jnp.exp(jnp.sum(lq1_ref[...] * lk1_ref[...], axis=1, keepdims=True))
            - jnp.exp(jnp.sum(lq2_ref[...] * lk2_ref[...], axis=1, keepdims=True)) + lam_init)


def _diff_attn_kernel(q_ref, k_ref, v_ref, lq1_ref, lk1_ref, lq2_ref, lk2_ref, g_ref, o_ref, k_scr, vt_scr, qq_scr,
                      acc_ref, s_ref, *, n_blk, tq, lam_init):
    g = pl.program_id(2)

    @pl.when(g == 0)
    def _():
        def body(i, carry):
            off = pl.multiple_of(i * tq, tq)
            k_scr[i] = k_ref[0, pl.ds(off, tq), :].astype(BF16)
            vt_scr[i] = _vt_ones(v_ref[0, pl.ds(off, tq), :])
            return carry
        lax.fori_loop(0, n_blk, body, 0)

    for r in range(2):
        qq_scr[r] = _stacked_qt(q_ref[0, r * tq:(r + 1) * tq, :].astype(F32))
    acc_ref[...] = jnp.zeros(acc_ref.shape, F32)
    lg = min(ATTN_LANE_GROUP, tq)
    n_lg = 2 * tq // lg
    group = lambda c: slice(c * lg, (c + 1) * lg)

    def keys_needed(c, masked):
        return (c % (tq // lg) + 1) * lg if masked else tq

    def issue(kb, r, c, masked=False):
        nk = keys_needed(c, masked)
        s_ref[:nk, group(c)] = _scores(k_scr[kb, :nk], qq_scr[r, :, group(c)])

    def block(kb, r, m_prev, masked, then):
        m_new = []
        for c in range(n_lg):
            if c + 1 < n_lg:
                issue(kb, r, c + 1, masked)
            elif then is not None:
                issue(then[0], then[1], 0)
            nk = keys_needed(c, masked)
            s = s_ref[:nk, group(c)]
            if masked:
                s = _mask_aligned_block(s, c * lg, tq)
            m_c, acc = _softmax_step(m_prev[c], acc_ref[r, :, group(c)], s, vt_scr[kb, :, :nk])
            acc_ref[r, :, group(c)] = acc
            m_new.append(m_c)
        return tuple(m_new)

    def query_block(r, n_full, then):
        m_init = tuple(jnp.full((1, lg), NEG, F32) for _ in range(n_lg))
        m = lax.fori_loop(0, n_full, lambda kb, m_prev: block(kb, r, m_prev, False, (kb + 1, r)), m_init)
        block(n_full, r, m, True, then)
        lam = _lambda(lq1_ref, lk1_ref, lq2_ref, lk2_ref, lam_init)
        o_ref[0, r * tq:(r + 1) * tq, :] = _diff_finalize(acc_ref[r], lam, g_ref[...], tq, lam_init).astype(o_ref.dtype)

    issue(0, 0, 0)
    query_block(0, 2 * g, (0, 1))
    query_block(1, 2 * g + 1, None)


def _cached_attn_kernel(q_ref, kc_ref, vc_ref, kp_ref, vp_ref, lq1_ref, lk1_ref, lq2_ref, lk2_ref, g_ref, o_ref,
                        qq_scr, m_scr, acc_scr, *, n_heads, tq, tkp, lam_init):
    j = pl.program_id(1)
    hd2 = 2 * HEAD_DIM
    head = lambda h: slice(h * hd2, (h + 1) * hd2)

    @pl.when(j == 0)
    def _():
        for h in range(n_heads):
            qq_scr[h] = _stacked_qt(q_ref[0, :, head(h)].astype(F32))
        m_scr[...] = jnp.full(m_scr.shape, NEG, F32)
        acc_scr[...] = jnp.zeros(acc_scr.shape, F32)

    def update(heads, keys, values, masked):
        ss = [_scores(keys(h).astype(BF16), qq_scr[h]) for h in heads]
        if masked:
            ss = [_mask_aligned_block(s, 0, tq) for s in ss]
        vts = [_vt_ones(values(h)) for h in heads]
        res = [_softmax_step(m_scr[h], acc_scr[h], s, vt) for h, s, vt in zip(heads, ss, vts)]
        for h, (m_new, acc) in zip(heads, res):
            m_scr[h] = m_new
            acc_scr[h] = acc

    head_groups = [range(h0, min(h0 + ATTN_HEAD_GROUP, n_heads)) for h0 in range(0, n_heads, ATTN_HEAD_GROUP)]
    for heads in head_groups:
        update(heads, lambda h: kp_ref[0, pl.ds(h, tkp, stride=n_heads), :],
               lambda h: vp_ref[0, pl.ds(h, tkp, stride=n_heads), :], False)

    @pl.when(j == pl.num_programs(1) - 1)
    def _():
        lam = _lambda(lq1_ref, lk1_ref, lq2_ref, lk2_ref, lam_init)
        for heads in head_groups:
            update(heads, lambda h: kc_ref[0, :, head(h)], lambda h: vc_ref[0, :, head(h)], True)
            for h in heads:
                o_ref[0, :, head(h)] = _diff_finalize(acc_scr[h], lam, g_ref[...], tq, lam_init).astype(o_ref.dtype)


def _lambda_args(lq1, lk1, lq2, lk2, subln_g):
    vec = lambda v: v.reshape(1, -1).astype(F32)
    return [vec(lq1), vec(lk1), vec(lq2), vec(lk2), subln_g.reshape(-1, 1).astype(F32)]


def _diff_attention(q, k_cur, v_cur, lq1, lk1, lq2, lk2, subln_g, lam_init):
    bsz, t_len, d_attn = q.shape
    hd2 = 2 * HEAD_DIM
    tq = _tile(t_len // 2, ATTN_TQ)
    n_blk = t_len // tq
    assert tq % CHUNK == 0 and n_blk % 2 == 0 and tq % min(ATTN_LANE_GROUP, tq) == 0
    const = lambda shape: pl.BlockSpec(shape, lambda b, h, i: (0, 0))
    return pl.pallas_call(
        functools.partial(_diff_attn_kernel, n_blk=n_blk, tq=tq, lam_init=lam_init),
        grid=(bsz, d_attn // hd2, n_blk // 2),
        in_specs=[pl.BlockSpec((1, 2 * tq, hd2), lambda b, h, i: (b, i, h)),
                  pl.BlockSpec((1, t_len, hd2), lambda b, h, i: (b, 0, h)),
                  pl.BlockSpec((1, t_len, hd2), lambda b, h, i: (b, 0, h))]
        + [const((1, HEAD_DIM))] * 4 + [const((hd2, 1))],
        out_specs=pl.BlockSpec((1, 2 * tq, hd2), lambda b, h, i: (b, i, h)),
        out_shape=jax.ShapeDtypeStruct((bsz, t_len, d_attn), BF16),
        scratch_shapes=[pltpu.VMEM((n_blk, tq, hd2), BF16), pltpu.VMEM((n_blk, hd2 + ONES_ROWS, tq), BF16),
                        pltpu.VMEM((2, hd2, 2 * tq), BF16), pltpu.VMEM((2, hd2 + ONES_ROWS, 2 * tq), F32),
                        pltpu.VMEM((tq, 2 * tq), F32)],
        compiler_params=_params("parallel", "parallel", "arbitrary"),
        name="diff_attention",
    )(q, k_cur, v_cur, *_lambda_args(lq1, lk1, lq2, lk2, subln_g))


def _cached_diff_attention(q, k_cur, v_cur, k_past, v_past, lq1, lk1, lq2, lk2, subln_g, lam_init):
    bsz, t_len, d_attn = q.shape
    hd2 = 2 * HEAD_DIM
    past_len, n_heads = k_past.shape[1], k_past.shape[2]
    tq = t_len
    tkp = _tile(past_len, ATTN_TK_CACHED)
    assert tq % CHUNK == 0 and past_len % CHUNK == 0 and tq <= ATTN_TQ and n_heads * hd2 == d_attn
    assert n_heads % SUBLANES == 0
    flat = lambda c: c.reshape(bsz, past_len * n_heads, hd2)
    const = lambda shape: pl.BlockSpec(shape, lambda b, j: (0, 0))
    rows = pl.BlockSpec((1, t_len, d_attn), lambda b, j: (b, 0, 0))
    hist = pl.BlockSpec((1, tkp * n_heads, hd2), lambda b, j: (b, j, 0))
    return pl.pallas_call(
        functools.partial(_cached_attn_kernel, n_heads=n_heads, tq=tq, tkp=tkp, lam_init=lam_init),
        grid=(bsz, past_len // tkp),
        in_specs=[rows, rows, rows, hist, hist] + [const((1, HEAD_DIM))] * 4 + [const((hd2, 1))],
        out_specs=rows,
        out_shape=jax.ShapeDtypeStruct((bsz, t_len, d_attn), BF16),
        scratch_shapes=[pltpu.VMEM((n_heads, hd2, 2 * tq), BF16), pltpu.VMEM((n_heads, 1, 2 * tq), F32),
                        pltpu.VMEM((n_heads, hd2 + ONES_ROWS, 2 * tq), F32)],
        compiler_params=_params("parallel", "arbitrary"),
        name="cached_diff_attention",
    )(q, k_cur, v_cur, flat(k_past), flat(v_past), *_lambda_args(lq1, lk1, lq2, lk2, subln_g))


def _mem_attn_kernel(q_ref, k_ref, v_ref, o_ref):
    scale = jnp.asarray(MEM_HEAD_DIM ** -0.5, q_ref.dtype)
    for h in range(MEM_HEADS):
        sl = slice(h * MEM_HEAD_DIM, (h + 1) * MEM_HEAD_DIM)
        q = q_ref[0, :, sl] * scale
        s = lax.dot_general(q, k_ref[0, :, sl].astype(BF16), (((1,), (1,)), ((), ())), preferred_element_type=F32)
        p = jnp.exp(s - jnp.max(s, axis=1, keepdims=True))
        p = p / jnp.sum(p, axis=1, keepdims=True)
        o = jnp.dot(p.astype(BF16), v_ref[0, :, sl].astype(BF16), preferred_element_type=F32)
        o_ref[0, :, sl] = o.astype(o_ref.dtype)


def _mem_attention(mq, mk, mv):
    bsz, t_len, d_mem = mq.shape
    n_mem = mk.shape[1]
    tt = _tile(t_len, 512)
    return pl.pallas_call(
        _mem_attn_kernel,
        grid=(bsz, t_len // tt),
        in_specs=[pl.BlockSpec((1, tt, d_mem), lambda b, t: (b, t, 0)),
                  pl.BlockSpec((1, n_mem, d_mem), lambda b, t: (b, 0, 0)),
                  pl.BlockSpec((1, n_mem, d_mem), lambda b, t: (b, 0, 0))],
        out_specs=pl.BlockSpec((1, tt, d_mem), lambda b, t: (b, t, 0)),
        out_shape=jax.ShapeDtypeStruct((bsz, t_len, d_mem), BF16),
        compiler_params=_params("parallel", "parallel"),
        name="mem_attention",
    )(mq, mk, mv)


def _to_slabs(dst_ref, val, rows, n_slab):
    for c in range(n_slab):
        dst_ref[pl.ds(c, rows, stride=n_slab), :] = val[:, c * LANES:(c + 1) * LANES].astype(dst_ref.dtype)


def _slab_chunk(src_ref, c, rows, n_slab, row0=0):
    return src_ref[pl.ds(row0 * n_slab + c, rows, stride=n_slab), :]


def _router_kernel(*refs, tr, n_slab, group_tiles):
    n_grp = len(group_tiles)
    x_refs = refs[:n_grp]
    g_ref, whi_ref, wlo_ref, b_ref, hn_ref, eid_ref, wt_ref, rank_ref, cnt_ref, run_ref = refs[n_grp:]
    t = pl.program_id(0)
    x = x_refs[-1][...]
    first_tile = sum(group_tiles[:-1])
    for gi in range(n_grp - 2, -1, -1):
        x = jnp.where(t < first_tile, x_refs[gi][...], x)
        first_tile -= group_tiles[gi]
    hn = x * lax.rsqrt(jnp.mean(x * x, axis=-1, keepdims=True) + EPS) * g_ref[...]
    _to_slabs(hn_ref, hn, tr, n_slab)
    h_hi = hn.astype(BF16)
    h_lo = (hn - h_hi.astype(F32)).astype(BF16)
    logits = (jnp.dot(h_hi, whi_ref[...], preferred_element_type=F32)
              + jnp.dot(h_lo, whi_ref[...], preferred_element_type=F32)
              + jnp.dot(h_hi, wlo_ref[...], preferred_element_type=F32)) + b_ref[...]
    lane = lax.broadcasted_iota(jnp.int32, logits.shape, 1)
    first = lambda hit: jnp.min(jnp.where(hit, lane, LANES), axis=1, keepdims=True)

    gl = jnp.where(lane < N_GROUPS, logits, -jnp.inf)
    gmax = jnp.max(gl, axis=1, keepdims=True)
    g_idx = first(gl == gmax)
    p_g = 1.0 / jnp.sum(jnp.exp(gl - gmax), axis=1, keepdims=True)

    lo = N_GROUPS + EXPERTS_PER_GROUP * g_idx
    in_grp = (lane >= lo) & (lane < lo + EXPERTS_PER_GROUP)
    el = jnp.where(in_grp, logits, -jnp.inf)
    pe = jnp.exp(el - jnp.max(el, axis=1, keepdims=True))
    probs = jnp.where(in_grp, pe / jnp.sum(pe, axis=1, keepdims=True), -1.0)
    v1 = jnp.max(probs, axis=1, keepdims=True)
    i1 = first(probs == v1)
    probs2 = jnp.where(lane == i1, -1.0, probs)
    v2 = jnp.max(probs2, axis=1, keepdims=True)
    i2 = first(probs2 == v2)
    tot = v1 + v2
    eid_ref[:, 0:1] = i1 - N_GROUPS
    eid_ref[:, 1:2] = i2 - N_GROUPS
    wt_ref[:, 0:1] = p_g * (v1 / tot)
    wt_ref[:, 1:2] = p_g * (v2 / tot)

    @pl.when(t == 0)
    def _():
        run_ref[...] = jnp.zeros(run_ref.shape, F32)

    sel1 = lane == i1
    sel2 = lane == i2
    oh1 = jnp.where(sel1, 1.0, 0.0).astype(BF16)
    oh2 = jnp.where(sel2, 1.0, 0.0).astype(BF16)
    earlier = (lax.broadcasted_iota(jnp.int32, (tr, tr), 1) < lax.broadcasted_iota(jnp.int32, (tr, tr), 0))
    tri = jnp.where(earlier, 1.0, 0.0).astype(BF16)
    run = run_ref[...]
    cnt1 = jnp.sum(oh1.astype(F32), axis=0, keepdims=True)
    cnt2 = jnp.sum(oh2.astype(F32), axis=0, keepdims=True)
    before1 = jnp.dot(tri, oh1, preferred_element_type=F32) + run
    before2 = jnp.dot(tri, oh2, preferred_element_type=F32) + (run + cnt1)
    rank_ref[:, 0:1] = jnp.sum(jnp.where(sel1, before1, 0.0), axis=1, keepdims=True).astype(jnp.int32)
    rank_ref[:, 1:2] = jnp.sum(jnp.where(sel2, before2, 0.0), axis=1, keepdims=True).astype(jnp.int32)
    run = run + cnt1 + cnt2
    run_ref[...] = run
    cnt_ref[...] = run.astype(jnp.int32)


def _router(xs, g, w_rg, b_rg, w_re, b_re):
    d = xs[0].shape[1]
    n = sum(x.shape[0] for x in xs)
    n_slab = d // LANES
    tr = 256
    assert all(x.shape[0] % tr == 0 for x in xs)
    group_tiles = tuple(x.shape[0] // tr for x in xs)
    x_specs = []
    for gi, nt in enumerate(group_tiles):
        t0 = sum(group_tiles[:gi])
        x_specs.append(pl.BlockSpec((tr, d), lambda i, t0=t0, nt=nt: (jnp.clip(i - t0, 0, nt - 1), 0)))
    w = jnp.zeros((d, LANES), F32).at[:, :N_GROUPS].set(w_rg).at[:, N_GROUPS:N_GROUPS + N_EXPERTS].set(w_re)
    b = jnp.zeros((1, LANES), F32).at[0, :N_GROUPS].set(b_rg).at[0, N_GROUPS:N_GROUPS + N_EXPERTS].set(b_re)
    w_hi = w.astype(BF16)
    w_lo = (w - w_hi.astype(F32)).astype(BF16)
    small = lambda dt: jax.ShapeDtypeStruct((n, TOP_K), dt)
    small_spec = pl.BlockSpec((tr, TOP_K), lambda i: (i, 0))
    return pl.pallas_call(
        functools.partial(_router_kernel, tr=tr, n_slab=n_slab, group_tiles=group_tiles),
        grid=(n // tr,),
        in_specs=x_specs + [
                  pl.BlockSpec((1, d), lambda i: (0, 0)),
                  pl.BlockSpec((d, LANES), lambda i: (0, 0)),
                  pl.BlockSpec((d, LANES), lambda i: (0, 0)),
                  pl.BlockSpec((1, LANES), lambda i: (0, 0))],
        out_specs=[pl.BlockSpec((tr * n_slab, LANES), lambda i: (i, 0)), small_spec, small_spec, small_spec,
                   pl.BlockSpec((1, LANES), lambda i: (0, 0))],
        out_shape=[jax.ShapeDtypeStruct((n * n_slab, LANES), F32), small(jnp.int32), small(F32), small(jnp.int32),
                   jax.ShapeDtypeStruct((1, LANES), jnp.int32)],
        scratch_shapes=[pltpu.VMEM((1, LANES), F32)],
        compiler_params=_params("arbitrary"),
        name="moe_router",
    )(*xs, g.reshape(1, d).astype(F32), w_hi, w_lo, b)


def _dispatch_kernel(pos_ref, zstart_ref, zvalid_ref, n_active_ref, hn_ref, xs_out, zbuf, sem, zsem,
                     *, td, tm, n_slab, n_tiles):
    t = pl.program_id(0)
    zrows = tm * n_slab

    def zero_tile(row0):
        return pltpu.make_async_copy(zbuf, xs_out.at[pl.ds(pl.multiple_of(row0 * n_slab, n_slab), zrows), :], zsem)

    @pl.when(t == 0)
    def _():
        zbuf[...] = jnp.zeros(zbuf.shape, zbuf.dtype)

        def for_each_zero_tile(act):
            for e in range(N_EXPERTS):
                @pl.when(zvalid_ref[e] > 0)
                def _():
                    act(zero_tile(zstart_ref[e]))

            def unused(i, carry):
                act(zero_tile(i * tm))
                return carry
            lax.fori_loop(n_active_ref[0], n_tiles, unused, 0)

        for_each_zero_tile(lambda copy: copy.start())
        for_each_zero_tile(lambda copy: copy.wait())

    def body(r, carry):
        src = hn_ref.at[pl.ds(pl.multiple_of(r * n_slab, n_slab), n_slab), :]
        for kk in range(TOP_K):
            dst_row = pl.multiple_of(pos_ref[(t * td + r) * TOP_K + kk] * n_slab, n_slab)
            pltpu.make_async_copy(src, xs_out.at[pl.ds(dst_row, n_slab), :], sem).start()
        return carry
    lax.fori_loop(0, td, body, 0)
    for kk in range(TOP_K):
        pltpu.make_async_copy(hn_ref, xs_out.at[pl.ds(0, td * n_slab), :], sem).wait()


def _expert_kernel(tile_expert_ref, n_active_ref, x_ref, wg_ref, wu_ref, wd_ref, ys_ref, xb_ref, *, tm, n_slab):
    del tile_expert_ref
    active = pl.program_id(0) < n_active_ref[0]

    @pl.when(jnp.logical_not(active))
    def _():
        ys_ref[...] = jnp.zeros(ys_ref.shape, ys_ref.dtype)

    @pl.when(active)
    def _():
        for c in range(n_slab):
            xb_ref[:, c * LANES:(c + 1) * LANES] = _slab_chunk(x_ref, c, tm, n_slab).astype(BF16)
        x = xb_ref[...]
        g = jnp.dot(x, wg_ref[0], preferred_element_type=F32)
        u = jnp.dot(x, wu_ref[0], preferred_element_type=F32)
        act = (g * jax.nn.sigmoid(g) * u).astype(BF16)
        _to_slabs(ys_ref, jnp.dot(act, wd_ref[0], preferred_element_type=F32), tm, n_slab)


def _start_slab_gather(pos_ref, tok0, ys_hbm, ybuf, sem, slot, tc, n_slab):
    def body(r, carry):
        for kk in range(TOP_K):
            src_row = pl.multiple_of(pos_ref[(tok0 + r) * TOP_K + kk] * n_slab, n_slab)
            dst_row = pl.multiple_of((kk * tc + r) * n_slab, n_slab)
            pltpu.make_async_copy(ys_hbm.at[pl.ds(src_row, n_slab), :],
                                  ybuf.at[slot, pl.ds(dst_row, n_slab), :], sem.at[slot]).start()
        return carry
    lax.fori_loop(0, tc, body, 0)


def _combine_kernel(pos_ref, x_ref, w_ref, g_ref, ys_hbm, o_ref, ybuf, sem, *, tc, tok_base, n_slab):
    t = pl.program_id(0)
    nt = pl.num_programs(0)
    slot = lax.rem(t, 2)

    @pl.when(t == 0)
    def _():
        _start_slab_gather(pos_ref, tok_base, ys_hbm, ybuf, sem, 0, tc, n_slab)

    @pl.when(t + 1 < nt)
    def _():
        _start_slab_gather(pos_ref, tok_base + (t + 1) * tc, ys_hbm, ybuf, sem, 1 - slot, tc, n_slab)

    pltpu.make_async_copy(ys_hbm.at[pl.ds(0, TOP_K * tc * n_slab), :], ybuf.at[slot], sem.at[slot]).wait()
    yb = ybuf.at[slot]
    wk = [jnp.broadcast_to(w_ref[:, kk:kk + 1], (tc, LANES)) for kk in range(TOP_K)]
    sq = jnp.zeros((tc, LANES), F32)
    for c in range(n_slab):
        moe = wk[0] * _slab_chunk(yb, c, tc, n_slab)
        for kk in range(1, TOP_K):
            moe = moe + wk[kk] * _slab_chunk(yb, c, tc, n_slab, row0=kk * tc)
        y = x_ref[:, c * LANES:(c + 1) * LANES] + moe
        o_ref[:, c * LANES:(c + 1) * LANES] = y
        sq = sq + y * y
    d = n_slab * LANES
    scale = lax.rsqrt(jnp.sum(sq, axis=1, keepdims=True) / d + EPS)
    o_ref[...] = o_ref[...] * scale * g_ref[...]


def _hier_moe_final(xs, norm_ffn, w_rg, b_rg, w_re, b_re, w_g, w_u, w_d, norm_final):
    d = xs[0].shape[1]
    n = sum(x.shape[0] for x in xs)
    n_slab = d // LANES
    d_exp = w_g.shape[2]
    hn, eid, wts, rank, cnt = _router(xs, norm_ffn, w_rg, b_rg, w_re, b_re)

    tm = 256
    na = n * TOP_K
    n_tiles = (na + N_EXPERTS * (tm - 1)) // tm + 1
    n_rows = n_tiles * tm
    counts = cnt[0, N_GROUPS:N_GROUPS + N_EXPERTS]
    padded = ((counts + tm - 1) // tm) * tm
    pend = jnp.cumsum(padded)
    pstart = pend - padded
    experts = jnp.arange(N_EXPERTS, dtype=jnp.int32)
    pos = (jnp.sum(jnp.where(eid[:, :, None] == experts, pstart, 0), axis=-1) + rank).reshape(na)
    tile_start = jnp.arange(n_tiles, dtype=jnp.int32) * tm
    tile_expert = jnp.minimum(jnp.sum((pend[None, :] <= tile_start[:, None]).astype(jnp.int32), axis=1), N_EXPERTS - 1)
    n_active = (pend[-1] // tm).astype(jnp.int32).reshape(1)
    zstart = jnp.maximum(pend - tm, 0).astype(jnp.int32)
    zvalid = (counts > 0).astype(jnp.int32)

    td = _tile(n, 256)
    x_sorted = pl.pallas_call(
        functools.partial(_dispatch_kernel, td=td, tm=tm, n_slab=n_slab, n_tiles=n_tiles),
        grid_spec=pltpu.PrefetchScalarGridSpec(
            num_scalar_prefetch=4,
            grid=(n // td,),
            in_specs=[pl.BlockSpec((td * n_slab, LANES), lambda t, p, zs, zv, na_: (t, 0))],
            out_specs=pl.BlockSpec(memory_space=pl.ANY),
            scratch_shapes=[pltpu.VMEM((tm * n_slab, LANES), F32), pltpu.SemaphoreType.DMA(()),
                            pltpu.SemaphoreType.DMA(())]),
        out_shape=jax.ShapeDtypeStruct((n_rows * n_slab, LANES), F32),
        compiler_params=_params("arbitrary"),
        name="moe_dispatch",
    )(pos, zstart, zvalid, n_active, hn)

    active = lambda t, te, na_: (jnp.minimum(t, na_[0] - 1), 0)
    ys = pl.pallas_call(
        functools.partial(_expert_kernel, tm=tm, n_slab=n_slab),
        grid_spec=pltpu.PrefetchScalarGridSpec(
            num_scalar_prefetch=2,
            grid=(n_tiles,),
            in_specs=[pl.BlockSpec((tm * n_slab, LANES), active),
                      pl.BlockSpec((1, d, d_exp), lambda t, te, na_: (te[t], 0, 0)),
                      pl.BlockSpec((1, d, d_exp), lambda t, te, na_: (te[t], 0, 0)),
                      pl.BlockSpec((1, d_exp, d), lambda t, te, na_: (te[t], 0, 0))],
            out_specs=pl.BlockSpec((tm * n_slab, LANES), lambda t, te, na_: (t, 0)),
            scratch_shapes=[pltpu.VMEM((tm, d), BF16)]),
        out_shape=jax.ShapeDtypeStruct((n_rows * n_slab, LANES), F32),
        compiler_params=_params("arbitrary"),
        name="moe_experts",
    )(tile_expert, n_active, x_sorted, w_g, w_u, w_d)

    outs = []
    row0 = 0
    for x in xs:
        rows = x.shape[0]
        tc = _tile(rows, 256)
        assert row0 % tc == 0
        blk0 = row0 // tc
        outs.append(pl.pallas_call(
            functools.partial(_combine_kernel, tc=tc, tok_base=row0, n_slab=n_slab),
            grid_spec=pltpu.PrefetchScalarGridSpec(
                num_scalar_prefetch=1,
                grid=(rows // tc,),
                in_specs=[pl.BlockSpec((tc, d), lambda t, p: (t, 0)),
                          pl.BlockSpec((tc, TOP_K), lambda t, p, blk0=blk0: (blk0 + t, 0)),
                          pl.BlockSpec((1, d), lambda t, p: (0, 0)),
                          pl.BlockSpec(memory_space=pl.ANY)],
                out_specs=pl.BlockSpec((tc, d), lambda t, p: (t, 0)),
                scratch_shapes=[pltpu.VMEM((2, TOP_K * tc * n_slab, LANES), F32), pltpu.SemaphoreType.DMA((2,))]),
            out_shape=jax.ShapeDtypeStruct((rows, d), F32),
            compiler_params=_params("arbitrary"),
            name="moe_combine",
        )(pos, x, wts, norm_final.reshape(1, d).astype(F32), ys))
        row0 += rows
    return outs


def _mixer(x, conv_buf, k_past, v_past, mk, mv, lam_init, w):
    bsz, t_len, d = x.shape
    n = bsz * t_len
    d_conv = w["w_dw"].shape[1]
    d_attn = w["w_attn_out"].shape[0]
    d_mem = w["w_mem_out"].shape[0]
    c0 = 2 * d_conv
    c1, c2, c3 = c0 + d_attn, c0 + 2 * d_attn, c0 + 3 * d_attn
    c4 = c3 + d_mem
    x2d = x.reshape(n, d)

    h = _rmsnorm(x2d, w["norm_mix"], BF16)
    w_in = w["w_in"]
    u_in = _matmul(h, w_in, F32, b_col0=0, n_out=c0)
    q = _matmul(h, w_in, F32, b_col0=c0, n_out=d_attn)
    k_rows = _matmul(h, w_in, F32, b_col0=c1, n_out=d_attn)
    v_rows = _matmul(h, w_in, F32, b_col0=c2, n_out=d_attn)
    mq = _matmul(h, w_in, BF16, b_col0=c3, n_out=d_mem)
    gates = _matmul(h, w_in, BF16, b_col0=c4, n_out=N_BRANCH * d)

    conv_act, new_buf = _conv_branch(u_in.reshape(bsz, t_len, c0), conv_buf, w["w_dw"], w["b_dw"],
                                     w["conv_ln_g"], w["conv_ln_b"])
    qkv = [a.reshape(bsz, t_len, d_attn) for a in (q, k_rows, v_rows)]
    lam_args = (w["lambda_q1"], w["lambda_k1"], w["lambda_q2"], w["lambda_k2"], w["subln_g"], lam_init)
    if k_past is None:
        o = _diff_attention(*qkv, *lam_args)
    else:
        o = _cached_diff_attention(*qkv, k_past, v_past, *lam_args)
    mo = _mem_attention(mq.reshape(bsz, t_len, d_mem), mk, mv)

    merged = _matmul(conv_act.reshape(n, d_conv), w["w_conv_out"], BF16, gate=gates, gate_col0=0)
    merged = _matmul(o.reshape(n, d_attn), w["w_attn_out"], BF16, gate=gates, gate_col0=d, add=merged)
    merged = _matmul(mo.reshape(n, d_mem), w["w_mem_out"], BF16, gate=gates, gate_col0=2 * d, add=merged)
    x2 = _matmul(merged, w["w_out"], F32, add=x2d)
    hd2 = 2 * HEAD_DIM
    return x2, new_buf, k_rows.reshape(bsz, t_len, N_HEADS, hd2), v_rows.reshape(bsz, t_len, N_HEADS, hd2)


def kernel(x_prompt, x_sample, mem_prompt, cache_conv, cache_diff_k, cache_diff_v, cache_mem_k, cache_mem_v, norm_mix, w_in, w_dw, b_dw, conv_ln_g, conv_ln_b, w_conv_out, lambda_q1, lambda_k1, lambda_q2, lambda_k2, subln_g, w_attn_out, norm_mem, w_mem_k, w_mem_v, w_mem_out, w_out, norm_ffn, w_router_grp, b_router_grp, w_router_exp, b_router_exp, w_exp_gate, w_exp_up, w_exp_down, norm_final):
    depth = w_in.shape[0]
    assert depth == 1, "the final rmsnorm is fused into the last layer's MoE combine"
    bp, tp, d = x_prompt.shape
    bs, ts, _ = x_sample.shape
    n_mem = mem_prompt.shape[1]
    d_conv = w_dw.shape[2]
    d_mem = w_mem_k.shape[2]
    l = 0
    lam_init = 0.8 - 0.6 * math.exp(-0.3 * l)
    w = dict(norm_mix=norm_mix[l], w_in=w_in[l].astype(BF16), w_dw=w_dw[l], b_dw=b_dw[l], conv_ln_g=conv_ln_g[l],
             conv_ln_b=conv_ln_b[l], w_conv_out=w_conv_out[l].astype(BF16), lambda_q1=lambda_q1[l],
             lambda_k1=lambda_k1[l], lambda_q2=lambda_q2[l], lambda_k2=lambda_k2[l], subln_g=subln_g[l],
             w_attn_out=w_attn_out[l].astype(BF16), w_mem_out=w_mem_out[l].astype(BF16), w_out=w_out[l].astype(BF16))

    hm = _rmsnorm(mem_prompt.reshape(bp * n_mem, d), norm_mem[l], BF16)
    mk = _matmul(hm, w_mem_k[l].astype(BF16), F32).reshape(bp, n_mem, d_mem)
    mv = _matmul(hm, w_mem_v[l].astype(BF16), F32).reshape(bp, n_mem, d_mem)
    conv0 = jnp.zeros((bp, CONV_W - 1, d_conv), F32)
    x2_p, cb_p, k_p, v_p = _mixer(x_prompt, conv0, None, None, mk, mv, lam_init, w)

    x2_s, cb_s, k_s, v_s = _mixer(x_sample, cache_conv.reshape(cache_conv.shape[1:]),
                                  cache_diff_k.reshape(cache_diff_k.shape[1:]),
                                  cache_diff_v.reshape(cache_diff_v.shape[1:]),
                                  cache_mem_k.reshape(bs, n_mem, d_mem), cache_mem_v.reshape(bs, n_mem, d_mem),
                                  lam_init, w)

    y_p, y_s = _hier_moe_final([x2_p, x2_s], norm_ffn[l], w_router_grp[l], b_router_grp[l], w_router_exp[l],
                               b_router_exp[l], w_exp_gate[l].astype(BF16), w_exp_up[l].astype(BF16),
                               w_exp_down[l].astype(BF16), norm_final)
    mem_shape = (1, bp, n_mem, MEM_HEADS, MEM_HEAD_DIM)
    return (y_p.reshape(bp, tp, d), y_s.reshape(bs, ts, d), cb_p[None], k_p[None], v_p[None],
            mk.reshape(mem_shape), mv.reshape(mem_shape), cb_s[None], k_s[None], v_s[None])
```

```python
import functools
import math

import jax
import jax.numpy as jnp
from jax import lax
from jax.experimental import pallas as pl
from jax.experimental.pallas import tpu as pltpu

F32 = jnp.float32
BF16 = jnp.bfloat16

CHUNK = 64
EPS = 1e-6
SUBLN_EPS = 1e-5
NEG = -1e30
CONV_W = 31
N_HEADS = 16
HEAD_DIM = 64
MEM_HEADS = 4
MEM_HEAD_DIM = 256
N_BRANCH = 3
N_GROUPS = 4
EXPERTS_PER_GROUP = 8
N_EXPERTS = N_GROUPS * EXPERTS_PER_GROUP
TOP_K = 2

LANES = 128
SUBLANES = 8
VMEM_LIMIT_BYTES = 56 * 1024 * 1024
HALO = 32


def _params(*semantics):
    return pltpu.CompilerParams(dimension_semantics=semantics, vmem_limit_bytes=VMEM_LIMIT_BYTES)


def _tile(n, pref):
    if n <= pref:
        return n
    t = pref
    while n % t:
        t //= 2
    return t


def _rmsnorm_kernel(x_ref, g_ref, o_ref, *, eps):
    x = x_ref[...].astype(F32)
    ms = jnp.mean(x * x, axis=-1, keepdims=True)
    o_ref[...] = (x * lax.rsqrt(ms + eps) * g_ref[...]).astype(o_ref.dtype)


def _rmsnorm(x, g, out_dtype, eps=EPS):
    n, d = x.shape
    tr = _tile(n, 512)
    return pl.pallas_call(
        functools.partial(_rmsnorm_kernel, eps=eps),
        grid=(n // tr,),
        in_specs=[pl.BlockSpec((tr, d), lambda i: (i, 0)), pl.BlockSpec((1, d), lambda i: (0, 0))],
        out_specs=pl.BlockSpec((tr, d), lambda i: (i, 0)),
        out_shape=jax.ShapeDtypeStruct((n, d), out_dtype),
        compiler_params=_params("parallel"),
        name="rmsnorm",
    )(x, g.reshape(1, d).astype(F32))


def _mm_kernel(*refs, has_gate, has_add):
    a_ref, b_ref = refs[0], refs[1]
    o_ref, b_bf16 = refs[-2], refs[-1]

    @pl.when(pl.program_id(1) == 0)
    def _():
        b_bf16[...] = b_ref[...].astype(BF16)

    acc = jnp.dot(a_ref[...], b_bf16[...], preferred_element_type=F32)
    pos = 2
    if has_gate:
        acc = acc * jax.nn.sigmoid(refs[pos][...].astype(F32))
        pos += 1
    if has_add:
        acc = acc + refs[pos][...].astype(F32)
    o_ref[...] = acc.astype(o_ref.dtype)


def _matmul(a, b, out_dtype, *, b_col0=0, n_out=None, gate=None, gate_col0=0, add=None):
    m, k = a.shape
    assert a.dtype == BF16 and b.dtype == F32
    n_out = b.shape[1] if n_out is None else n_out
    tm = _tile(m, 1024)
    tn = _tile(n_out, 1024)
    assert b_col0 % tn == 0 and gate_col0 % tn == 0
    bj0, gj0 = b_col0 // tn, gate_col0 // tn
    in_specs = [pl.BlockSpec((tm, k), lambda j, i: (i, 0)),
                pl.BlockSpec((k, tn), lambda j, i: (0, bj0 + j))]
    args = [a, b]
    if gate is not None:
        in_specs.append(pl.BlockSpec((tm, tn), lambda j, i: (i, gj0 + j)))
        args.append(gate)
    if add is not None:
        in_specs.append(pl.BlockSpec((tm, tn), lambda j, i: (i, j)))
        args.append(add)
    return pl.pallas_call(
        functools.partial(_mm_kernel, has_gate=gate is not None, has_add=add is not None),
        grid=(n_out // tn, m // tm),
        in_specs=in_specs,
        out_specs=pl.BlockSpec((tm, tn), lambda j, i: (i, j)),
        out_shape=jax.ShapeDtypeStruct((m, n_out), out_dtype),
        scratch_shapes=[pltpu.VMEM((k, tn), BF16)],
        compiler_params=_params("parallel", "arbitrary"),
        name="matmul",
    )(*args)


def _conv_kernel(u_ref, buf_ref, w_ref, bdw_ref, g_ref, b_ref, act_ref, newbuf_ref, ext_ref, shift_ref, *, tt, dc):
    t = pl.program_id(1)
    hist = CONV_W - 1

    @pl.when(t == 0)
    def _():
        ext_ref[pl.ds(HALO - hist, hist), :] = buf_ref[0].astype(F32)

    @pl.when(t > 0)
    def _():
        ext_ref[pl.ds(0, HALO), :] = ext_ref[pl.ds(tt, HALO), :]

    a = u_ref[0, :, :dc].astype(F32)
    gt = u_ref[0, :, dc:].astype(F32)
    ext_ref[pl.ds(HALO, tt), :] = a * jax.nn.sigmoid(gt)

    n_ext = HALO + tt
    ext = ext_ref[...]
    for b in range(1, SUBLANES):
        shift_ref[b - 1] = pltpu.roll(ext, n_ext - b, axis=0)
    y = jnp.zeros((tt, dc), F32) + bdw_ref[...]
    for j in range(CONV_W):
        a, b = divmod(HALO - hist + j, SUBLANES)
        src = ext_ref if b == 0 else shift_ref.at[b - 1]
        y = y + src[pl.ds(a * SUBLANES, tt), :] * w_ref[pl.ds(j, 1), :]
    mu = jnp.mean(y, axis=-1, keepdims=True)
    var = jnp.mean(jnp.square(y - mu), axis=-1, keepdims=True)
    z = (y - mu) * lax.rsqrt(var + EPS) * g_ref[...] + b_ref[...]
    act_ref[0] = (z * jax.nn.sigmoid(z)).astype(act_ref.dtype)
    newbuf_ref[0] = ext_ref[pl.ds(HALO + tt - hist, hist), :]


def _conv_branch(u_in, conv_buf, w_dw, b_dw, ln_g, ln_b):
    bsz, t_len, two_dc = u_in.shape
    dc = two_dc // 2
    tt = _tile(t_len, 256)
    assert tt >= HALO and tt % SUBLANES == 0
    vec = lambda v: v.reshape(1, dc).astype(F32)
    return pl.pallas_call(
        functools.partial(_conv_kernel, tt=tt, dc=dc),
        grid=(bsz, t_len // tt),
        in_specs=[pl.BlockSpec((1, tt, two_dc), lambda b, t: (b, t, 0)),
                  pl.BlockSpec((1, CONV_W - 1, dc), lambda b, t: (b, 0, 0)),
                  pl.BlockSpec((CONV_W, dc), lambda b, t: (0, 0)),
                  pl.BlockSpec((1, dc), lambda b, t: (0, 0)),
                  pl.BlockSpec((1, dc), lambda b, t: (0, 0)),
                  pl.BlockSpec((1, dc), lambda b, t: (0, 0))],
        out_specs=[pl.BlockSpec((1, tt, dc), lambda b, t: (b, t, 0)),
                   pl.BlockSpec((1, CONV_W - 1, dc), lambda b, t: (b, 0, 0))],
        out_shape=[jax.ShapeDtypeStruct((bsz, t_len, dc), BF16),
                   jax.ShapeDtypeStruct((bsz, CONV_W - 1, dc), F32)],
        scratch_shapes=[pltpu.VMEM((HALO + tt, dc), F32), pltpu.VMEM((SUBLANES - 1, HALO + tt, dc), F32)],
        compiler_params=_params("parallel", "arbitrary"),
        name="conv_branch",
    )(u_in, conv_buf, w_dw.astype(F32), vec(b_dw), vec(ln_g), vec(ln_b))


ONES_ROWS = 16
ATTN_TQ = 1024
ATTN_LANE_GROUP = 512
ATTN_TK_CACHED = 512
ATTN_HEAD_GROUP = 4
LOG2E = 1.4426950408889634


def _stacked_qt(q):
    tq = q.shape[0]
    qt = (q * (HEAD_DIM ** -0.5 * LOG2E)).T
    z = jnp.zeros((HEAD_DIM, tq), F32)
    return jnp.concatenate([jnp.concatenate([qt[:HEAD_DIM], z], axis=0),
                            jnp.concatenate([z, qt[HEAD_DIM:]], axis=0)], axis=1).astype(BF16)


def _vt_ones(v):
    vt = v.astype(F32).T.astype(BF16)
    return jnp.concatenate([vt, jnp.ones((ONES_ROWS, v.shape[0]), BF16)], axis=0)


def _scores(k_blk, qqt):
    return jnp.dot(k_blk, qqt, preferred_element_type=F32)


def _softmax_step(m_prev, acc_prev, s, vt_blk):
    m_new = jnp.maximum(m_prev, jnp.max(s, axis=0, keepdims=True))
    alpha = jnp.exp2(m_prev - m_new)
    p = jnp.exp2(s - m_new).astype(BF16)
    return m_new, alpha * acc_prev + jnp.dot(vt_blk, p, preferred_element_type=F32)


def _mask_aligned_block(s, lane0, tq):
    q_chunk = lax.rem(lax.broadcasted_iota(jnp.int32, (1, s.shape[1]), 1) + lane0, tq) // CHUNK
    bands = [jnp.where(q_chunk >= c, s[c * CHUNK:(c + 1) * CHUNK, :], NEG) for c in range(s.shape[0] // CHUNK)]
    return jnp.concatenate(bands, axis=0)


def _diff_finalize(acc, lam, g_col, tq, lam_init):
    hd2 = 2 * HEAD_DIM
    inv_l = 1.0 / acc[hd2:hd2 + 1, :]
    ot = acc[:hd2, :tq] * inv_l[:, :tq] - lam * (acc[:hd2, tq:] * inv_l[:, tq:])
    ot = ot * lax.rsqrt(jnp.mean(ot * ot, axis=0, keepdims=True) + SUBLN_EPS) * g_col
    return (ot * (1.0 - lam_init)).T


def _lambda(lq1_ref, lk1_ref, lq2_ref, lk2_ref, lam_init):
    return (jnp.exp(jnp.sum(lq1_ref[...] * lk1_ref[...], axis=1, keepdims=True))
            - jnp.exp(jnp.sum(lq2_ref[...] * lk2_ref[...], axis=1, keepdims=True)) + lam_init)


def _diff_attn_kernel(q_ref, k_ref, v_ref, lq1_ref, lk1_ref, lq2_ref, lk2_ref, g_ref, o_ref, k_scr, vt_scr, qq_scr,
                      acc_ref, s_ref, *, n_blk, tq, lam_init):
    g = pl.program_id(2)

    @pl.when(g == 0)
    def _():
        def body(i, carry):
            off = pl.multiple_of(i * tq, tq)
            k_scr[i] = k_ref[0, pl.ds(off, tq), :].astype(BF16)
            vt_scr[i] = _vt_ones(v_ref[0, pl.ds(off, tq), :])
            return carry
        lax.fori_loop(0, n_blk, body, 0)

    for r in range(2):
        qq_scr[r] = _stacked_qt(q_ref[0, r * tq:(r + 1) * tq, :].astype(F32))
    acc_ref[...] = jnp.zeros(acc_ref.shape, F32)
    lg = min(ATTN_LANE_GROUP, tq)
    n_lg = 2 * tq // lg
    group = lambda c: slice(c * lg, (c + 1) * lg)

    def keys_needed(c, masked):
        return (c % (tq // lg) + 1) * lg if masked else tq

    def issue(kb, r, c, masked=False):
        nk = keys_needed(c, masked)
        s_ref[:nk, group(c)] = _scores(k_scr[kb, :nk], qq_scr[r, :, group(c)])

    def block(kb, r, m_prev, masked, then):
        m_new = []
        for c in range(n_lg):
            if c + 1 < n_lg:
                issue(kb, r, c + 1, masked)
            elif then is not None:
                issue(then[0], then[1], 0)
            nk = keys_needed(c, masked)
            s = s_ref[:nk, group(c)]
            if masked:
                s = _mask_aligned_block(s, c * lg, tq)
            m_c, acc = _softmax_step(m_prev[c], acc_ref[r, :, group(c)], s, vt_scr[kb, :, :nk])
            acc_ref[r, :, group(c)] = acc
            m_new.append(m_c)
        return tuple(m_new)

    def query_block(r, n_full, then):
        m_init = tuple(jnp.full((1, lg), NEG, F32) for _ in range(n_lg))
        m = lax.fori_loop(0, n_full, lambda kb, m_prev: block(kb, r, m_prev, False, (kb + 1, r)), m_init)
        block(n_full, r, m, True, then)
        lam = _lambda(lq1_ref, lk1_ref, lq2_ref, lk2_ref, lam_init)
        o_ref[0, r * tq:(r + 1) * tq, :] = _diff_finalize(acc_ref[r], lam, g_ref[...], tq, lam_init).astype(o_ref.dtype)

    issue(0, 0, 0)
    query_block(0, 2 * g, (0, 1))
    query_block(1, 2 * g + 1, None)


def _cached_attn_kernel(q_ref, kc_ref, vc_ref, kp_ref, vp_ref, lq1_ref, lk1_ref, lq2_ref, lk2_ref, g_ref, o_ref,
                        qq_scr, m_scr, acc_scr, *, n_heads, tq, tkp, lam_init):
    j = pl.program_id(1)
    hd2 = 2 * HEAD_DIM
    head = lambda h: slice(h * hd2, (h + 1) * hd2)

    @pl.when(j == 0)
    def _():
        for h in range(n_heads):
            qq_scr[h] = _stacked_qt(q_ref[0, :, head(h)].astype(F32))
        m_scr[...] = jnp.full(m_scr.shape, NEG, F32)
        acc_scr[...] = jnp.zeros(acc_scr.shape, F32)

    def update(heads, keys, values, masked):
        ss = [_scores(keys(h).astype(BF16), qq_scr[h]) for h in heads]
        if masked:
            ss = [_mask_aligned_block(s, 0, tq) for s in ss]
        vts = [_vt_ones(values(h)) for h in heads]
        res = [_softmax_step(m_scr[h], acc_scr[h], s, vt) for h, s, vt in zip(heads, ss, vts)]
        for h, (m_new, acc) in zip(heads, res):
            m_scr[h] = m_new
            acc_scr[h] = acc

    head_groups = [range(h0, min(h0 + ATTN_HEAD_GROUP, n_heads)) for h0 in range(0, n_heads, ATTN_HEAD_GROUP)]
    for heads in head_groups:
        update(heads, lambda h: kp_ref[0, pl.ds(h, tkp, stride=n_heads), :],
               lambda h: vp_ref[0, pl.ds(h, tkp, stride=n_heads), :], False)

    @pl.when(j == pl.num_programs(1) - 1)
    def _():
        lam = _lambda(lq1_ref, lk1_ref, lq2_ref, lk2_ref, lam_init)
        for heads in head_groups:
            update(heads, lambda h: kc_ref[0, :, head(h)], lambda h: vc_ref[0, :, head(h)], True)
            for h in heads:
                o_ref[0, :, head(h)] = _diff_finalize(acc_scr[h], lam, g_ref[...], tq, lam_init).astype(o_ref.dtype)


def _lambda_args(lq1, lk1, lq2, lk2, subln_g):
    vec = lambda v: v.reshape(1, -1).astype(F32)
    return [vec(lq1), vec(lk1), vec(lq2), vec(lk2), subln_g.reshape(-1, 1).astype(F32)]


def _diff_attention(q, k_cur, v_cur, lq1, lk1, lq2, lk2, subln_g, lam_init):
    bsz, t_len, d_attn = q.shape
    hd2 = 2 * HEAD_DIM
    tq = _tile(t_len // 2, ATTN_TQ)
    n_blk = t_len // tq
    assert tq % CHUNK == 0 and n_blk % 2 == 0 and tq % min(ATTN_LANE_GROUP, tq) == 0
    const = lambda shape: pl.BlockSpec(shape, lambda b, h, i: (0, 0))
    return pl.pallas_call(
        functools.partial(_diff_attn_kernel, n_blk=n_blk, tq=tq, lam_init=lam_init),
        grid=(bsz, d_attn // hd2, n_blk // 2),
        in_specs=[pl.BlockSpec((1, 2 * tq, hd2), lambda b, h, i: (b, i, h)),
                  pl.BlockSpec((1, t_len, hd2), lambda b, h, i: (b, 0, h)),
                  pl.BlockSpec((1, t_len, hd2), lambda b, h, i: (b, 0, h))]
        + [const((1, HEAD_DIM))] * 4 + [const((hd2, 1))],
        out_specs=pl.BlockSpec((1, 2 * tq, hd2), lambda b, h, i: (b, i, h)),
        out_shape=jax.ShapeDtypeStruct((bsz, t_len, d_attn), BF16),
        scratch_shapes=[pltpu.VMEM((n_blk, tq, hd2), BF16), pltpu.VMEM((n_blk, hd2 + ONES_ROWS, tq), BF16),
                        pltpu.VMEM((2, hd2, 2 * tq), BF16), pltpu.VMEM((2, hd2 + ONES_ROWS, 2 * tq), F32),
                        pltpu.VMEM((tq, 2 * tq), F32)],
        compiler_params=_params("parallel", "parallel", "arbitrary"),
        name="diff_attention",
    )(q, k_cur, v_cur, *_lambda_args(lq1, lk1, lq2, lk2, subln_g))


def _cached_diff_attention(q, k_cur, v_cur, k_past, v_past, lq1, lk1, lq2, lk2, subln_g, lam_init):
    bsz, t_len, d_attn = q.shape
    hd2 = 2 * HEAD_DIM
    past_len, n_heads = k_past.shape[1], k_past.shape[2]
    tq = t_len
    tkp = _tile(past_len, ATTN_TK_CACHED)
    assert tq % CHUNK == 0 and past_len % CHUNK == 0 and tq <= ATTN_TQ and n_heads * hd2 == d_attn
    assert n_heads % SUBLANES == 0
    flat = lambda c: c.reshape(bsz, past_len * n_heads, hd2)
    const = lambda shape: pl.BlockSpec(shape, lambda b, j: (0, 0))
    rows = pl.BlockSpec((1, t_len, d_attn), lambda b, j: (b, 0, 0))
    hist = pl.BlockSpec((1, tkp * n_heads, hd2), lambda b, j: (b, j, 0))
    return pl.pallas_call(
        functools.partial(_cached_attn_kernel, n_heads=n_heads, tq=tq, tkp=tkp, lam_init=lam_init),
        grid=(bsz, past_len // tkp),
        in_specs=[rows, rows, rows, hist, hist] + [const((1, HEAD_DIM))] * 4 + [const((hd2, 1))],
        out_specs=rows,
        out_shape=jax.ShapeDtypeStruct((bsz, t_len, d_attn), BF16),
        scratch_shapes=[pltpu.VMEM((n_heads, hd2, 2 * tq), BF16), pltpu.VMEM((n_heads, 1, 2 * tq), F32),
                        pltpu.VMEM((n_heads, hd2 + ONES_ROWS, 2 * tq), F32)],
        compiler_params=_params("parallel", "arbitrary"),
        name="cached_diff_attention",
    )(q, k_cur, v_cur, flat(k_past), flat(v_past), *_lambda_args(lq1, lk1, lq2, lk2, subln_g))


def _mem_attn_kernel(q_ref, k_ref, v_ref, o_ref):
    scale = jnp.asarray(MEM_HEAD_DIM ** -0.5, q_ref.dtype)
    for h in range(MEM_HEADS):
        sl = slice(h * MEM_HEAD_DIM, (h + 1) * MEM_HEAD_DIM)
        q = q_ref[0, :, sl] * scale
        s = lax.dot_general(q, k_ref[0, :, sl].astype(BF16), (((1,), (1,)), ((), ())), preferred_element_type=F32)
        p = jnp.exp(s - jnp.max(s, axis=1, keepdims=True))
        p = p / jnp.sum(p, axis=1, keepdims=True)
        o = jnp.dot(p.astype(BF16), v_ref[0, :, sl].astype(BF16), preferred_element_type=F32)
        o_ref[0, :, sl] = o.astype(o_ref.dtype)


def _mem_attention(mq, mk, mv):
    bsz, t_len, d_mem = mq.shape
    n_mem = mk.shape[1]
    tt = _tile(t_len, 512)
    return pl.pallas_call(
        _mem_attn_kernel,
        grid=(bsz, t_len // tt),
        in_specs=[pl.BlockSpec((1, tt, d_mem), lambda b, t: (b, t, 0)),
                  pl.BlockSpec((1, n_mem, d_mem), lambda b, t: (b, 0, 0)),
                  pl.BlockSpec((1, n_mem, d_mem), lambda b, t: (b, 0, 0))],
        out_specs=pl.BlockSpec((1, tt, d_mem), lambda b, t: (b, t, 0)),
        out_shape=jax.ShapeDtypeStruct((bsz, t_len, d_mem), BF16),
        compiler_params=_params("parallel", "parallel"),
        name="mem_attention",
    )(mq, mk, mv)


def _to_slabs(dst_ref, val, rows, n_slab):
    for c in range(n_slab):
        dst_ref[pl.ds(c, rows, stride=n_slab), :] = val[:, c * LANES:(c + 1) * LANES].astype(dst_ref.dtype)


def _slab_chunk(src_ref, c, rows, n_slab, row0=0):
    return src_ref[pl.ds(row0 * n_slab + c, rows, stride=n_slab), :]


def _router_kernel(*refs, tr, n_slab, group_tiles):
    n_grp = len(group_tiles)
    x_refs = refs[:n_grp]
    g_ref, whi_ref, wlo_ref, b_ref, hn_ref, eid_ref, wt_ref, rank_ref, cnt_ref, run_ref = refs[n_grp:]
    t = pl.program_id(0)
    x = x_refs[-1][...]
    first_tile = sum(group_tiles[:-1])
    for gi in range(n_grp - 2, -1, -1):
        x = jnp.where(t < first_tile, x_refs[gi][...], x)
        first_tile -= group_tiles[gi]
    hn = x * lax.rsqrt(jnp.mean(x * x, axis=-1, keepdims=True) + EPS) * g_ref[...]
    _to_slabs(hn_ref, hn, tr, n_slab)
    h_hi = hn.astype(BF16)
    h_lo = (hn - h_hi.astype(F32)).astype(BF16)
    logits = (jnp.dot(h_hi, whi_ref[...], preferred_element_type=F32)
              + jnp.dot(h_lo, whi_ref[...], preferred_element_type=F32)
              + jnp.dot(h_hi, wlo_ref[...], preferred_element_type=F32)) + b_ref[...]
    lane = lax.broadcasted_iota(jnp.int32, logits.shape, 1)
    first = lambda hit: jnp.min(jnp.where(hit, lane, LANES), axis=1, keepdims=True)

    gl = jnp.where(lane < N_GROUPS, logits, -jnp.inf)
    gmax = jnp.max(gl, axis=1, keepdims=True)
    g_idx = first(gl == gmax)
    p_g = 1.0 / jnp.sum(jnp.exp(gl - gmax), axis=1, keepdims=True)

    lo = N_GROUPS + EXPERTS_PER_GROUP * g_idx
    in_grp = (lane >= lo) & (lane < lo + EXPERTS_PER_GROUP)
    el = jnp.where(in_grp, logits, -jnp.inf)
    pe = jnp.exp(el - jnp.max(el, axis=1, keepdims=True))
    probs = jnp.where(in_grp, pe / jnp.sum(pe, axis=1, keepdims=True), -1.0)
    v1 = jnp.max(probs, axis=1, keepdims=True)
    i1 = first(probs == v1)
    probs2 = jnp.where(lane == i1, -1.0, probs)
    v2 = jnp.max(probs2, axis=1, keepdims=True)
    i2 = first(probs2 == v2)
    tot = v1 + v2
    eid_ref[:, 0:1] = i1 - N_GROUPS
    eid_ref[:, 1:2] = i2 - N_GROUPS
    wt_ref[:, 0:1] = p_g * (v1 / tot)
    wt_ref[:, 1:2] = p_g * (v2 / tot)

    @pl.when(t == 0)
    def _():
        run_ref[...] = jnp.zeros(run_ref.shape, F32)

    sel1 = lane == i1
    sel2 = lane == i2
    oh1 = jnp.where(sel1, 1.0, 0.0).astype(BF16)
    oh2 = jnp.where(sel2, 1.0, 0.0).astype(BF16)
    earlier = (lax.broadcasted_iota(jnp.int32, (tr, tr), 1) < lax.broadcasted_iota(jnp.int32, (tr, tr), 0))
    tri = jnp.where(earlier, 1.0, 0.0).astype(BF16)
    run = run_ref[...]
    cnt1 = jnp.sum(oh1.astype(F32), axis=0, keepdims=True)
    cnt2 = jnp.sum(oh2.astype(F32), axis=0, keepdims=True)
    before1 = jnp.dot(tri, oh1, preferred_element_type=F32) + run
    before2 = jnp.dot(tri, oh2, preferred_element_type=F32) + (run + cnt1)
    rank_ref[:, 0:1] = jnp.sum(jnp.where(sel1, before1, 0.0), axis=1, keepdims=True).astype(jnp.int32)
    rank_ref[:, 1:2] = jnp.sum(jnp.where(sel2, before2, 0.0), axis=1, keepdims=True).astype(jnp.int32)
    run = run + cnt1 + cnt2
    run_ref[...] = run
    cnt_ref[...] = run.astype(jnp.int32)


def _router(xs, g, w_rg, b_rg, w_re, b_re):
    d = xs[0].shape[1]
    n = sum(x.shape[0] for x in xs)
    n_slab = d // LANES
    tr = 256
    assert all(x.shape[0] % tr == 0 for x in xs)
    group_tiles = tuple(x.shape[0] // tr for x in xs)
    x_specs = []
    for gi, nt in enumerate(group_tiles):
        t0 = sum(group_tiles[:gi])
        x_specs.append(pl.BlockSpec((tr, d), lambda i, t0=t0, nt=nt: (jnp.clip(i - t0, 0, nt - 1), 0)))
    w = jnp.zeros((d, LANES), F32).at[:, :N_GROUPS].set(w_rg).at[:, N_GROUPS:N_GROUPS + N_EXPERTS].set(w_re)
    b = jnp.zeros((1, LANES), F32).at[0, :N_GROUPS].set(b_rg).at[0, N_GROUPS:N_GROUPS + N_EXPERTS].set(b_re)
    w_hi = w.astype(BF16)
    w_lo = (w - w_hi.astype(F32)).astype(BF16)
    small = lambda dt: jax.ShapeDtypeStruct((n, TOP_K), dt)
    small_spec = pl.BlockSpec((tr, TOP_K), lambda i: (i, 0))
    return pl.pallas_call(
        functools.partial(_router_kernel, tr=tr, n_slab=n_slab, group_tiles=group_tiles),
        grid=(n // tr,),
        in_specs=x_specs + [
                  pl.BlockSpec((1, d), lambda i: (0, 0)),
                  pl.BlockSpec((d, LANES), lambda i: (0, 0)),
                  pl.BlockSpec((d, LANES), lambda i: (0, 0)),
                  pl.BlockSpec((1, LANES), lambda i: (0, 0))],
        out_specs=[pl.BlockSpec((tr * n_slab, LANES), lambda i: (i, 0)), small_spec, small_spec, small_spec,
                   pl.BlockSpec((1, LANES), lambda i: (0, 0))],
        out_shape=[jax.ShapeDtypeStruct((n * n_slab, LANES), F32), small(jnp.int32), small(F32), small(jnp.int32),
                   jax.ShapeDtypeStruct((1, LANES), jnp.int32)],
        scratch_shapes=[pltpu.VMEM((1, LANES), F32)],
        compiler_params=_params("arbitrary"),
        name="moe_router",
    )(*xs, g.reshape(1, d).astype(F32), w_hi, w_lo, b)


def _dispatch_kernel(pos_ref, zstart_ref, zvalid_ref, n_active_ref, hn_ref, xs_out, zbuf, sem, zsem,
                     *, td, tm, n_slab, n_tiles):
    t = pl.program_id(0)
    zrows = tm * n_slab

    def zero_tile(row0):
        return pltpu.make_async_copy(zbuf, xs_out.at[pl.ds(pl.multiple_of(row0 * n_slab, n_slab), zrows), :], zsem)

    @pl.when(t == 0)
    def _():
        zbuf[...] = jnp.zeros(zbuf.shape, zbuf.dtype)

        def for_each_zero_tile(act):
            for e in range(N_EXPERTS):
                @pl.when(zvalid_ref[e] > 0)
                def _():
                    act(zero_tile(zstart_ref[e]))

            def unused(i, carry):
                act(zero_tile(i * tm))
                return carry
            lax.fori_loop(n_active_ref[0], n_tiles, unused, 0)

        for_each_zero_tile(lambda copy: copy.start())
        for_each_zero_tile(lambda copy: copy.wait())

    def body(r, carry):
        src = hn_ref.at[pl.ds(pl.multiple_of(r * n_slab, n_slab), n_slab), :]
        for kk in range(TOP_K):
            dst_row = pl.multiple_of(pos_ref[(t * td + r) * TOP_K + kk] * n_slab, n_slab)
            pltpu.make_async_copy(src, xs_out.at[pl.ds(dst_row, n_slab), :], sem).start(priority=kk % 2)
        return carry
    lax.fori_loop(0, td, body, 0)
    for kk in range(TOP_K):
        pltpu.make_async_copy(hn_ref, xs_out.at[pl.ds(0, td * n_slab), :], sem).wait()


def _expert_kernel(tile_expert_ref, n_active_ref, x_ref, wg_ref, wu_ref, wd_ref, ys_ref, xb_ref, wgb_ref, wub_ref,
                   wdb_ref, *, tm, n_slab):
    t = pl.program_id(0)
    active = t < n_active_ref[0]

    @pl.when(jnp.logical_not(active))
    def _():
        ys_ref[...] = jnp.zeros(ys_ref.shape, ys_ref.dtype)

    @pl.when(active & ((t == 0) | (tile_expert_ref[t] != tile_expert_ref[jnp.maximum(t - 1, 0)])))
    def _():
        wgb_ref[...] = wg_ref[0].astype(BF16)
        wub_ref[...] = wu_ref[0].astype(BF16)
        wdb_ref[...] = wd_ref[0].astype(BF16)

    @pl.when(active)
    def _():
        for c in range(n_slab):
            xb_ref[:, c * LANES:(c + 1) * LANES] = _slab_chunk(x_ref, c, tm, n_slab).astype(BF16)
        x = xb_ref[...]
        g = jnp.dot(x, wgb_ref[...], preferred_element_type=F32)
        u = jnp.dot(x, wub_ref[...], preferred_element_type=F32)
        act = (g * jax.nn.sigmoid(g) * u).astype(BF16)
        _to_slabs(ys_ref, jnp.dot(act, wdb_ref[...], preferred_element_type=F32), tm, n_slab)


def _start_slab_gather(pos_ref, tok0, ys_hbm, ybuf, sem, slot, tc, n_slab):
    def body(r, carry):
        for kk in range(TOP_K):
            src_row = pl.multiple_of(pos_ref[(tok0 + r) * TOP_K + kk] * n_slab, n_slab)
            dst_row = pl.multiple_of((kk * tc + r) * n_slab, n_slab)
            pltpu.make_async_copy(ys_hbm.at[pl.ds(src_row, n_slab), :],
                                  ybuf.at[slot, pl.ds(dst_row, n_slab), :], sem.at[slot]).start(priority=kk % 2)
        return carry
    lax.fori_loop(0, tc, body, 0)


def _combine_kernel(pos_ref, x_ref, w_ref, g_ref, ys_hbm, o_ref, ybuf, sem, *, tc, tok_base, n_slab):
    t = pl.program_id(0)
    nt = pl.num_programs(0)
    slot = lax.rem(t, 2)

    @pl.when(t == 0)
    def _():
        _start_slab_gather(pos_ref, tok_base, ys_hbm, ybuf, sem, 0, tc, n_slab)

    @pl.when(t + 1 < nt)
    def _():
        _start_slab_gather(pos_ref, tok_base + (t + 1) * tc, ys_hbm, ybuf, sem, 1 - slot, tc, n_slab)

    pltpu.make_async_copy(ys_hbm.at[pl.ds(0, TOP_K * tc * n_slab), :], ybuf.at[slot], sem.at[slot]).wait()
    yb = ybuf.at[slot]
    wk = [jnp.broadcast_to(w_ref[:, kk:kk + 1], (tc, LANES)) for kk in range(TOP_K)]
    sq = jnp.zeros((tc, LANES), F32)
    for c in range(n_slab):
        moe = wk[0] * _slab_chunk(yb, c, tc, n_slab)
        for kk in range(1, TOP_K):
            moe = moe + wk[kk] * _slab_chunk(yb, c, tc, n_slab, row0=kk * tc)
        y = x_ref[:, c * LANES:(c + 1) * LANES] + moe
        o_ref[:, c * LANES:(c + 1) * LANES] = y
        sq = sq + y * y
    d = n_slab * LANES
    scale = lax.rsqrt(jnp.sum(sq, axis=1, keepdims=True) / d + EPS)
    o_ref[...] = o_ref[...] * scale * g_ref[...]


def _hier_moe_final(xs, norm_ffn, w_rg, b_rg, w_re, b_re, w_g, w_u, w_d, norm_final):
    d = xs[0].shape[1]
    n = sum(x.shape[0] for x in xs)
    n_slab = d // LANES
    d_exp = w_g.shape[2]
    hn, eid, wts, rank, cnt = _router(xs, norm_ffn, w_rg, b_rg, w_re, b_re)

    tm = 256
    na = n * TOP_K
    n_tiles = (na + N_EXPERTS * (tm - 1)) // tm + 1
    n_rows = n_tiles * tm
    counts = cnt[0, N_GROUPS:N_GROUPS + N_EXPERTS]
    padded = ((counts + tm - 1) // tm) * tm
    pend = jnp.cumsum(padded)
    pstart = pend - padded
    experts = jnp.arange(N_EXPERTS, dtype=jnp.int32)
    pos = (jnp.sum(jnp.where(eid[:, :, None] == experts, pstart, 0), axis=-1) + rank).reshape(na)
    tile_start = jnp.arange(n_tiles, dtype=jnp.int32) * tm
    tile_expert = jnp.minimum(jnp.sum((pend[None, :] <= tile_start[:, None]).astype(jnp.int32), axis=1), N_EXPERTS - 1)
    n_active = (pend[-1] // tm).astype(jnp.int32).reshape(1)
    zstart = jnp.maximum(pend - tm, 0).astype(jnp.int32)
    zvalid = (counts > 0).astype(jnp.int32)

    td = _tile(n, 256)
    x_sorted = pl.pallas_call(
        functools.partial(_dispatch_kernel, td=td, tm=tm, n_slab=n_slab, n_tiles=n_tiles),
        grid_spec=pltpu.PrefetchScalarGridSpec(
            num_scalar_prefetch=4,
            grid=(n // td,),
            in_specs=[pl.BlockSpec((td * n_slab, LANES), lambda t, p, zs, zv, na_: (t, 0))],
            out_specs=pl.BlockSpec(memory_space=pl.ANY),
            scratch_shapes=[pltpu.VMEM((tm * n_slab, LANES), F32), pltpu.SemaphoreType.DMA(()),
                            pltpu.SemaphoreType.DMA(())]),
        out_shape=jax.ShapeDtypeStruct((n_rows * n_slab, LANES), F32),
        compiler_params=_params("arbitrary"),
        name="moe_dispatch",
    )(pos, zstart, zvalid, n_active, hn)

    active = lambda t, te, na_: (jnp.minimum(t, na_[0] - 1), 0)
    ys = pl.pallas_call(
        functools.partial(_expert_kernel, tm=tm, n_slab=n_slab),
        grid_spec=pltpu.PrefetchScalarGridSpec(
            num_scalar_prefetch=2,
            grid=(n_tiles,),
            in_specs=[pl.BlockSpec((tm * n_slab, LANES), active),
                      pl.BlockSpec((1, d, d_exp), lambda t, te, na_: (te[t], 0, 0)),
                      pl.BlockSpec((1, d, d_exp), lambda t, te, na_: (te[t], 0, 0)),
                      pl.BlockSpec((1, d_exp, d), lambda t, te, na_: (te[t], 0, 0))],
            out_specs=pl.BlockSpec((tm * n_slab, LANES), lambda t, te, na_: (t, 0)),
            scratch_shapes=[pltpu.VMEM((tm, d), BF16), pltpu.VMEM((d, d_exp), BF16), pltpu.VMEM((d, d_exp), BF16),
                            pltpu.VMEM((d_exp, d), BF16)]),
        out_shape=jax.ShapeDtypeStruct((n_rows * n_slab, LANES), F32),
        compiler_params=_params("arbitrary"),
        name="moe_experts",
    )(tile_expert, n_active, x_sorted, w_g, w_u, w_d)

    outs = []
    row0 = 0
    for x in xs:
        rows = x.shape[0]
        tc = _tile(rows, 256)
        assert row0 % tc == 0
        blk0 = row0 // tc
        outs.append(pl.pallas_call(
            functools.partial(_combine_kernel, tc=tc, tok_base=row0, n_slab=n_slab),
            grid_spec=pltpu.PrefetchScalarGridSpec(
                num_scalar_prefetch=1,
                grid=(rows // tc,),
                in_specs=[pl.BlockSpec((tc, d), lambda t, p: (t, 0)),
                          pl.BlockSpec((tc, TOP_K), lambda t, p, blk0=blk0: (blk0 + t, 0)),
                          pl.BlockSpec((1, d), lambda t, p: (0, 0)),
                          pl.BlockSpec(memory_space=pl.ANY)],
                out_specs=pl.BlockSpec((tc, d), lambda t, p: (t, 0)),
                scratch_shapes=[pltpu.VMEM((2, TOP_K * tc * n_slab, LANES), F32), pltpu.SemaphoreType.DMA((2,))]),
            out_shape=jax.ShapeDtypeStruct((rows, d), F32),
            compiler_params=_params("arbitrary"),
            name="moe_combine",
        )(pos, x, wts, norm_final.reshape(1, d).astype(F32), ys))
        row0 += rows
    return outs


def _mixer(x, conv_buf, k_past, v_past, mk, mv, lam_init, w):
    bsz, t_len, d = x.shape
    n = bsz * t_len
    d_conv = w["w_dw"].shape[1]
    d_attn = w["w_attn_out"].shape[0]
    d_mem = w["w_mem_out"].shape[0]
    c0 = 2 * d_conv
    c1, c2, c3 = c0 + d_attn, c0 + 2 * d_attn, c0 + 3 * d_attn
    c4 = c3 + d_mem
    x2d = x.reshape(n, d)

    h = _rmsnorm(x2d, w["norm_mix"], BF16)
    w_in = w["w_in"]
    u_in = _matmul(h, w_in, F32, b_col0=0, n_out=c0)
    q = _matmul(h, w_in, F32, b_col0=c0, n_out=d_attn)
    k_rows = _matmul(h, w_in, F32, b_col0=c1, n_out=d_attn)
    v_rows = _matmul(h, w_in, F32, b_col0=c2, n_out=d_attn)
    mq = _matmul(h, w_in, BF16, b_col0=c3, n_out=d_mem)
    gates = _matmul(h, w_in, BF16, b_col0=c4, n_out=N_BRANCH * d)

    conv_act, new_buf = _conv_branch(u_in.reshape(bsz, t_len, c0), conv_buf, w["w_dw"], w["b_dw"],
                                     w["conv_ln_g"], w["conv_ln_b"])
    qkv = [a.reshape(bsz, t_len, d_attn) for a in (q, k_rows, v_rows)]
    lam_args = (w["lambda_q1"], w["lambda_k1"], w["lambda_q2"], w["lambda_k2"], w["subln_g"], lam_init)
    if k_past is None:
        o = _diff_attention(*qkv, *lam_args)
    else:
        o = _cached_diff_attention(*qkv, k_past, v_past, *lam_args)
    mo = _mem_attention(mq.reshape(bsz, t_len, d_mem), mk, mv)

    merged = _matmul(conv_act.reshape(n, d_conv), w["w_conv_out"], BF16, gate=gates, gate_col0=0)
    merged = _matmul(o.reshape(n, d_attn), w["w_attn_out"], BF16, gate=gates, gate_col0=d, add=merged)
    merged = _matmul(mo.reshape(n, d_mem), w["w_mem_out"], BF16, gate=gates, gate_col0=2 * d, add=merged)
    x2 = _matmul(merged, w["w_out"], F32, add=x2d)
    hd2 = 2 * HEAD_DIM
    return x2, new_buf, k_rows.reshape(bsz, t_len, N_HEADS, hd2), v_rows.reshape(bsz, t_len, N_HEADS, hd2)


def kernel(x_prompt, x_sample, mem_prompt, cache_conv, cache_diff_k, cache_diff_v, cache_mem_k, cache_mem_v, norm_mix, w_in, w_dw, b_dw, conv_ln_g, conv_ln_b, w_conv_out, lambda_q1, lambda_k1, lambda_q2, lambda_k2, subln_g, w_attn_out, norm_mem, w_mem_k, w_mem_v, w_mem_out, w_out, norm_ffn, w_router_grp, b_router_grp, w_router_exp, b_router_exp, w_exp_gate, w_exp_up, w_exp_down, norm_final):
    depth = w_in.shape[0]
    assert depth == 1, "the final rmsnorm is fused into the last layer's MoE combine"
    bp, tp, d = x_prompt.shape
    bs, ts, _ = x_sample.shape
    n_mem = mem_prompt.shape[1]
    d_conv = w_dw.shape[2]
    d_mem = w_mem_k.shape[2]
    l = 0
    lam_init = 0.8 - 0.6 * math.exp(-0.3 * l)
    w = dict(norm_mix=norm_mix[l], w_in=w_in[l], w_dw=w_dw[l], b_dw=b_dw[l], conv_ln_g=conv_ln_g[l],
             conv_ln_b=conv_ln_b[l], w_conv_out=w_conv_out[l], lambda_q1=lambda_q1[l],
             lambda_k1=lambda_k1[l], lambda_q2=lambda_q2[l], lambda_k2=lambda_k2[l], subln_g=subln_g[l],
             w_attn_out=w_attn_out[l], w_mem_out=w_mem_out[l], w_out=w_out[l])

    hm = _rmsnorm(mem_prompt.reshape(bp * n_mem, d), norm_mem[l], BF16)
    mk = _matmul(hm, w_mem_k[l], F32).reshape(bp, n_mem, d_mem)
    mv = _matmul(hm, w_mem_v[l], F32).reshape(bp, n_mem, d_mem)
    conv0 = jnp.zeros((bp, CONV_W - 1, d_conv), F32)
    x2_p, cb_p, k_p, v_p = _mixer(x_prompt, conv0, None, None, mk, mv, lam_init, w)

    x2_s, cb_s, k_s, v_s = _mixer(x_sample, cache_conv.reshape(cache_conv.shape[1:]),
                                  cache_diff_k.reshape(cache_diff_k.shape[1:]),
                                  cache_diff_v.reshape(cache_diff_v.shape[1:]),
                                  cache_mem_k.reshape(bs, n_mem, d_mem), cache_mem_v.reshape(bs, n_mem, d_mem),
                                  lam_init, w)

    y_p, y_s = _hier_moe_final([x2_p, x2_s], norm_ffn[l], w_router_grp[l], b_router_grp[l], w_router_exp[l],
                               b_router_exp[l], w_exp_gate[l], w_exp_up[l], w_exp_down[l], norm_final)
    mem_shape = (1, bp, n_mem, MEM_HEADS, MEM_HEAD_DIM)
    return (y_p.reshape(bp, tp, d), y_s.reshape(bs, ts, d), cb_p[None], k_p[None], v_p[None],
            mk.reshape(mem_shape), mv.reshape(mem_shape), cb_s[None], k_s[None], v_s[None])
```

```python
import functools
import math

import jax
import jax.numpy as jnp
from jax import lax
from jax.experimental import pallas as pl
from jax.experimental.pallas import tpu as pltpu

F32 = jnp.float32
BF16 = jnp.bfloat16

CHUNK = 64
EPS = 1e-6
SUBLN_EPS = 1e-5
NEG = -1e30
CONV_W = 31
N_HEADS = 16
HEAD_DIM = 64
MEM_HEADS = 4
MEM_HEAD_DIM = 256
N_BRANCH = 3
N_GROUPS = 4
EXPERTS_PER_GROUP = 8
N_EXPERTS = N_GROUPS * EXPERTS_PER_GROUP
TOP_K = 2

LANES = 128
SUBLANES = 8
VMEM_LIMIT_BYTES = 56 * 1024 * 1024
HALO = 32


def _params(*semantics):
    return pltpu.CompilerParams(dimension_semantics=semantics, vmem_limit_bytes=VMEM_LIMIT_BYTES)


def _tile(n, pref):
    if n <= pref:
        return n
    t = pref
    while n % t:
        t //= 2
    return t


def _rmsnorm_kernel(x_ref, g_ref, o_ref, *, eps):
    x = x_ref[...].astype(F32)
    ms = jnp.mean(x * x, axis=-1, keepdims=True)
    o_ref[...] = (x * lax.rsqrt(ms + eps) * g_ref[...]).astype(o_ref.dtype)


def _rmsnorm(x, g, out_dtype, eps=EPS):
    n, d = x.shape
    tr = _tile(n, 512)
    return pl.pallas_call(
        functools.partial(_rmsnorm_kernel, eps=eps),
        grid=(n // tr,),
        in_specs=[pl.BlockSpec((tr, d), lambda i: (i, 0)), pl.BlockSpec((1, d), lambda i: (0, 0))],
        out_specs=pl.BlockSpec((tr, d), lambda i: (i, 0)),
        out_shape=jax.ShapeDtypeStruct((n, d), out_dtype),
        compiler_params=_params("parallel"),
        name="rmsnorm",
    )(x, g.reshape(1, d).astype(F32))


def _mm_kernel(*refs, n_grp, tile_ranges, has_gate, has_add):
    a_refs, b_ref = refs[:n_grp], refs[n_grp]
    pos = n_grp + 1
    gate_refs = refs[pos:pos + n_grp] if has_gate else None
    pos += n_grp * has_gate
    add_refs = refs[pos:pos + n_grp] if has_add else None
    o_refs, b_bf16 = refs[-1 - n_grp:-1], refs[-1]
    i = pl.program_id(1)

    @pl.when(i == 0)
    def _():
        b_bf16[...] = b_ref[...].astype(BF16)

    for g, (t0, t1) in enumerate(tile_ranges):
        @pl.when((i >= t0) & (i < t1))
        def _():
            acc = jnp.dot(a_refs[g][...], b_bf16[...], preferred_element_type=F32)
            if has_gate:
                acc = acc * jax.nn.sigmoid(gate_refs[g][...].astype(F32))
            if has_add:
                acc = acc + add_refs[g][...].astype(F32)
            o_refs[g][...] = acc.astype(o_refs[g].dtype)


def _matmul(a_groups, b, out_dtype, *, b_col0=0, n_out=None, gates=None, gate_col0=0, adds=None):
    k = a_groups[0].shape[1]
    assert all(a.dtype == BF16 for a in a_groups) and b.dtype == F32
    n_out = b.shape[1] if n_out is None else n_out
    tn = _tile(n_out, 1024)
    assert b_col0 % tn == 0 and gate_col0 % tn == 0
    bj0, gj0 = b_col0 // tn, gate_col0 // tn
    tms = [_tile(a.shape[0], 1024) for a in a_groups]
    tiles = [a.shape[0] // tm for a, tm in zip(a_groups, tms)]
    starts = [sum(tiles[:g]) for g in range(len(tiles))]
    tile_ranges = tuple((t0, t0 + nt) for t0, nt in zip(starts, tiles))

    def rows(g, col0=None):
        t0, nt = starts[g], tiles[g]
        if col0 is None:
            return lambda j, i: (jnp.clip(i - t0, 0, nt - 1), 0)
        return lambda j, i: (jnp.clip(i - t0, 0, nt - 1), col0 + j)

    n_grp = len(a_groups)
    in_specs = [pl.BlockSpec((tms[g], k), rows(g)) for g in range(n_grp)]
    in_specs.append(pl.BlockSpec((k, tn), lambda j, i: (0, bj0 + j), pipeline_mode=pl.Buffered(1)))
    args = list(a_groups) + [b]
    if gates is not None:
        in_specs += [pl.BlockSpec((tms[g], tn), rows(g, gj0)) for g in range(n_grp)]
        args += list(gates)
    if adds is not None:
        in_specs += [pl.BlockSpec((tms[g], tn), rows(g, 0)) for g in range(n_grp)]
        args += list(adds)
    return pl.pallas_call(
        functools.partial(_mm_kernel, n_grp=n_grp, tile_ranges=tile_ranges, has_gate=gates is not None,
                          has_add=adds is not None),
        grid=(n_out // tn, sum(tiles)),
        in_specs=in_specs,
        out_specs=[pl.BlockSpec((tms[g], tn), rows(g, 0)) for g in range(n_grp)],
        out_shape=[jax.ShapeDtypeStruct((a.shape[0], n_out), out_dtype) for a in a_groups],
        scratch_shapes=[pltpu.VMEM((k, tn), BF16)],
        compiler_params=_params("parallel", "arbitrary"),
        name="matmul",
    )(*args)


def _conv_kernel(u_ref, buf_ref, w_ref, bdw_ref, g_ref, b_ref, act_ref, newbuf_ref, ext_ref, shift_ref, *, tt, dc):
    t = pl.program_id(1)
    hist = CONV_W - 1

    @pl.when(t == 0)
    def _():
        ext_ref[pl.ds(HALO - hist, hist), :] = buf_ref[0].astype(F32)

    @pl.when(t > 0)
    def _():
        ext_ref[pl.ds(0, HALO), :] = ext_ref[pl.ds(tt, HALO), :]

    a = u_ref[0, :, :dc].astype(F32)
    gt = u_ref[0, :, dc:].astype(F32)
    ext_ref[pl.ds(HALO, tt), :] = a * jax.nn.sigmoid(gt)

    n_ext = HALO + tt
    ext = ext_ref[...]
    for b in range(1, SUBLANES):
        shift_ref[b - 1] = pltpu.roll(ext, n_ext - b, axis=0)
    y = jnp.zeros((tt, dc), F32) + bdw_ref[...]
    for j in range(CONV_W):
        a, b = divmod(HALO - hist + j, SUBLANES)
        src = ext_ref if b == 0 else shift_ref.at[b - 1]
        y = y + src[pl.ds(a * SUBLANES, tt), :] * w_ref[pl.ds(j, 1), :]
    mu = jnp.mean(y, axis=-1, keepdims=True)
    var = jnp.mean(jnp.square(y - mu), axis=-1, keepdims=True)
    z = (y - mu) * lax.rsqrt(var + EPS) * g_ref[...] + b_ref[...]
    act_ref[0] = (z * jax.nn.sigmoid(z)).astype(act_ref.dtype)
    newbuf_ref[0] = ext_ref[pl.ds(HALO + tt - hist, hist), :]


def _conv_branch(u_in, conv_buf, w_dw, b_dw, ln_g, ln_b):
    bsz, t_len, two_dc = u_in.shape
    dc = two_dc // 2
    tt = _tile(t_len, 256)
    assert tt >= HALO and tt % SUBLANES == 0
    vec = lambda v: v.reshape(1, dc).astype(F32)
    return pl.pallas_call(
        functools.partial(_conv_kernel, tt=tt, dc=dc),
        grid=(bsz, t_len // tt),
        in_specs=[pl.BlockSpec((1, tt, two_dc), lambda b, t: (b, t, 0)),
                  pl.BlockSpec((1, CONV_W - 1, dc), lambda b, t: (b, 0, 0)),
                  pl.BlockSpec((CONV_W, dc), lambda b, t: (0, 0)),
                  pl.BlockSpec((1, dc), lambda b, t: (0, 0)),
                  pl.BlockSpec((1, dc), lambda b, t: (0, 0)),
                  pl.BlockSpec((1, dc), lambda b, t: (0, 0))],
        out_specs=[pl.BlockSpec((1, tt, dc), lambda b, t: (b, t, 0)),
                   pl.BlockSpec((1, CONV_W - 1, dc), lambda b, t: (b, 0, 0))],
        out_shape=[jax.ShapeDtypeStruct((bsz, t_len, dc), BF16),
                   jax.ShapeDtypeStruct((bsz, CONV_W - 1, dc), F32)],
        scratch_shapes=[pltpu.VMEM((HALO + tt, dc), F32), pltpu.VMEM((SUBLANES - 1, HALO + tt, dc), F32)],
        compiler_params=_params("parallel", "arbitrary"),
        name="conv_branch",
    )(u_in, conv_buf, w_dw.astype(F32), vec(b_dw), vec(ln_g), vec(ln_b))


ONES_ROWS = 16
ATTN_TQ = 1024
ATTN_LANE_GROUP = 512
ATTN_TK_CACHED = 512
ATTN_HEAD_GROUP = 4
LOG2E = 1.4426950408889634


def _stacked_qt(q):
    tq = q.shape[0]
    qt = (q * (HEAD_DIM ** -0.5 * LOG2E)).T
    z = jnp.zeros((HEAD_DIM, tq), F32)
    return jnp.concatenate([jnp.concatenate([qt[:HEAD_DIM], z], axis=0),
                            jnp.concatenate([z, qt[HEAD_DIM:]], axis=0)], axis=1).astype(BF16)


def _vt_ones(v):
    vt = v.astype(F32).T.astype(BF16)
    return jnp.concatenate([vt, jnp.ones((ONES_ROWS, v.shape[0]), BF16)], axis=0)


def _scores(k_blk, qqt):
    return jnp.dot(k_blk, qqt, preferred_element_type=F32)


def _softmax_step(m_prev, acc_prev, s, vt_blk):
    m_new = jnp.maximum(m_prev, jnp.max(s, axis=0, keepdims=True))
    alpha = jnp.exp2(m_prev - m_new)
    p = jnp.exp2(s - m_new).astype(BF16)
    return m_new, alpha * acc_prev + jnp.dot(vt_blk, p, preferred_element_type=F32)


def _mask_aligned_block(s, lane0, tq):
    q_chunk = lax.rem(lax.broadcasted_iota(jnp.int32, (1, s.shape[1]), 1) + lane0, tq) // CHUNK
    bands = [jnp.where(q_chunk >= c, s[c * CHUNK:(c + 1) * CHUNK, :], NEG) for c in range(s.shape[0] // CHUNK)]
    return jnp.concatenate(bands, axis=0)


def _diff_finalize(acc, lam, g_col, tq, lam_init):
    hd2 = 2 * HEAD_DIM
    inv_l = 1.0 / acc[hd2:hd2 + 1, :]
    ot = acc[:hd2, :tq] * inv_l[:, :tq] - lam * (acc[:hd2, tq:] * inv_l[:, tq:])
    ot = ot * lax.rsqrt(jnp.mean(ot * ot, axis=0, keepdims=True) + SUBLN_EPS) * g_col
    return (ot * (1.0 - lam_init)).T


def _lambda(lq1_ref, lk1_ref, lq2_ref, lk2_ref, lam_init):
    return (jnp.exp(jnp.sum(lq1_ref[...] * lk1_ref[...], axis=1, keepdims=True))
            - jnp.exp(jnp.sum(lq2_ref[...] * lk2_ref[...], axis=1, keepdims=True)) + lam_init)


def _diff_attn_kernel(q_ref, k_ref, v_ref, lq1_ref, lk1_ref, lq2_ref, lk2_ref, g_ref, o_ref, k_scr, vt_scr, qq_scr,
                      acc_ref, s_ref, *, n_blk, tq, lam_init):
    g = pl.program_id(2)

    @pl.when(g == 0)
    def _():
        def body(i, carry):
            off = pl.multiple_of(i * tq, tq)
            k_scr[i] = k_ref[0, pl.ds(off, tq), :].astype(BF16)
            vt_scr[i] = _vt_ones(v_ref[0, pl.ds(off, tq), :])
            return carry
        lax.fori_loop(0, n_blk, body, 0)

    for r in range(2):
        qq_scr[r] = _stacked_qt(q_ref[0, r * tq:(r + 1) * tq, :].astype(F32))
    acc_ref[...] = jnp.zeros(acc_ref.shape, F32)
    lg = min(ATTN_LANE_GROUP, tq)
    n_lg = 2 * tq // lg
    group = lambda c: slice(c * lg, (c + 1) * lg)

    def keys_needed(c, masked):
        return (c % (tq // lg) + 1) * lg if masked else tq

    def issue(kb, r, c, masked=False):
        nk = keys_needed(c, masked)
        s_ref[:nk, group(c)] = _scores(k_scr[kb, :nk], qq_scr[r, :, group(c)])

    def block(kb, r, m_prev, masked, then):
        m_new = []
        for c in range(n_lg):
            if c + 1 < n_lg:
                issue(kb, r, c + 1, masked)
            elif then is not None:
                issue(then[0], then[1], 0)
            nk = keys_needed(c, masked)
            s = s_ref[:nk, group(c)]
            if masked:
                s = _mask_aligned_block(s, c * lg, tq)
            m_c, acc = _softmax_step(m_prev[c], acc_ref[r, :, group(c)], s, vt_scr[kb, :, :nk])
            acc_ref[r, :, group(c)] = acc
            m_new.append(m_c)
        return tuple(m_new)

    def query_block(r, n_full, then):
        m_init = tuple(jnp.full((1, lg), NEG, F32) for _ in range(n_lg))
        m = lax.fori_loop(0, n_full, lambda kb, m_prev: block(kb, r, m_prev, False, (kb + 1, r)), m_init)
        block(n_full, r, m, True, then)
        lam = _lambda(lq1_ref, lk1_ref, lq2_ref, lk2_ref, lam_init)
        o_ref[0, r * tq:(r + 1) * tq, :] = _diff_finalize(acc_ref[r], lam, g_ref[...], tq, lam_init).astype(o_ref.dtype)

    issue(0, 0, 0)
    query_block(0, 2 * g, (0, 1))
    query_block(1, 2 * g + 1, None)


def _cached_attn_kernel(q_ref, kc_ref, vc_ref, kp_ref, vp_ref, lq1_ref, lk1_ref, lq2_ref, lk2_ref, g_ref, o_ref,
                        qq_scr, m_scr, acc_scr, *, n_heads, tq, tkp, lam_init):
    j = pl.program_id(1)
    hd2 = 2 * HEAD_DIM
    head = lambda h: slice(h * hd2, (h + 1) * hd2)

    @pl.when(j == 0)
    def _():
        for h in range(n_heads):
            qq_scr[h] = _stacked_qt(q_ref[0, :, head(h)].astype(F32))
        m_scr[...] = jnp.full(m_scr.shape, NEG, F32)
        acc_scr[...] = jnp.zeros(acc_scr.shape, F32)

    def update(heads, keys, values, masked):
        ss = [_scores(keys(h).astype(BF16), qq_scr[h]) for h in heads]
        if masked:
            ss = [_mask_aligned_block(s, 0, tq) for s in ss]
        vts = [_vt_ones(values(h)) for h in heads]
        res = [_softmax_step(m_scr[h], acc_scr[h], s, vt) for h, s, vt in zip(heads, ss, vts)]
        for h, (m_new, acc) in zip(heads, res):
            m_scr[h] = m_new
            acc_scr[h] = acc

    head_groups = [range(h0, min(h0 + ATTN_HEAD_GROUP, n_heads)) for h0 in range(0, n_heads, ATTN_HEAD_GROUP)]
    for heads in head_groups:
        update(heads, lambda h: kp_ref[0, pl.ds(h, tkp, stride=n_heads), :],
               lambda h: vp_ref[0, pl.ds(h, tkp, stride=n_heads), :], False)

    @pl.when(j == pl.num_programs(1) - 1)
    def _():
        lam = _lambda(lq1_ref, lk1_ref, lq2_ref, lk2_ref, lam_init)
        for heads in head_groups:
            update(heads, lambda h: kc_ref[0, :, head(h)], lambda h: vc_ref[0, :, head(h)], True)
            for h in heads:
                o_ref[0, :, head(h)] = _diff_finalize(acc_scr[h], lam, g_ref[...], tq, lam_init).astype(o_ref.dtype)


def _lambda_args(lq1, lk1, lq2, lk2, subln_g):
    vec = lambda v: v.reshape(1, -1).astype(F32)
    return [vec(lq1), vec(lk1), vec(lq2), vec(lk2), subln_g.reshape(-1, 1).astype(F32)]


def _diff_attention(q, k_cur, v_cur, lq1, lk1, lq2, lk2, subln_g, lam_init):
    bsz, t_len, d_attn = q.shape
    hd2 = 2 * HEAD_DIM
    tq = _tile(t_len // 2, ATTN_TQ)
    n_blk = t_len // tq
    assert tq % CHUNK == 0 and n_blk % 2 == 0 and tq % min(ATTN_LANE_GROUP, tq) == 0
    const = lambda shape: pl.BlockSpec(shape, lambda b, h, i: (0, 0))
    return pl.pallas_call(
        functools.partial(_diff_attn_kernel, n_blk=n_blk, tq=tq, lam_init=lam_init),
        grid=(bsz, d_attn // hd2, n_blk // 2),
        in_specs=[pl.BlockSpec((1, 2 * tq, hd2), lambda b, h, i: (b, i, h)),
                  pl.BlockSpec((1, t_len, hd2), lambda b, h, i: (b, 0, h)),
                  pl.BlockSpec((1, t_len, hd2), lambda b, h, i: (b, 0, h))]
        + [const((1, HEAD_DIM))] * 4 + [const((hd2, 1))],
        out_specs=pl.BlockSpec((1, 2 * tq, hd2), lambda b, h, i: (b, i, h)),
        out_shape=jax.ShapeDtypeStruct((bsz, t_len, d_attn), BF16),
        scratch_shapes=[pltpu.VMEM((n_blk, tq, hd2), BF16), pltpu.VMEM((n_blk, hd2 + ONES_ROWS, tq), BF16),
                        pltpu.VMEM((2, hd2, 2 * tq), BF16), pltpu.VMEM((2, hd2 + ONES_ROWS, 2 * tq), F32),
                        pltpu.VMEM((tq, 2 * tq), F32)],
        compiler_params=_params("parallel", "parallel", "arbitrary"),
        name="diff_attention",
    )(q, k_cur, v_cur, *_lambda_args(lq1, lk1, lq2, lk2, subln_g))


def _cached_diff_attention(q, k_cur, v_cur, k_past, v_past, lq1, lk1, lq2, lk2, subln_g, lam_init):
    bsz, t_len, d_attn = q.shape
    hd2 = 2 * HEAD_DIM
    past_len, n_heads = k_past.shape[1], k_past.shape[2]
    tq = t_len
    tkp = _tile(past_len, ATTN_TK_CACHED)
    assert tq % CHUNK == 0 and past_len % CHUNK == 0 and tq <= ATTN_TQ and n_heads * hd2 == d_attn
    assert n_heads % SUBLANES == 0
    flat = lambda c: c.reshape(bsz, past_len * n_heads, hd2)
    const = lambda shape: pl.BlockSpec(shape, lambda b, j: (0, 0))
    rows = pl.BlockSpec((1, t_len, d_attn), lambda b, j: (b, 0, 0))
    hist = pl.BlockSpec((1, tkp * n_heads, hd2), lambda b, j: (b, j, 0))
    return pl.pallas_call(
        functools.partial(_cached_attn_kernel, n_heads=n_heads, tq=tq, tkp=tkp, lam_init=lam_init),
        grid=(bsz, past_len // tkp),
        in_specs=[rows, rows, rows, hist, hist] + [const((1, HEAD_DIM))] * 4 + [const((hd2, 1))],
        out_specs=rows,
        out_shape=jax.ShapeDtypeStruct((bsz, t_len, d_attn), BF16),
        scratch_shapes=[pltpu.VMEM((n_heads, hd2, 2 * tq), BF16), pltpu.VMEM((n_heads, 1, 2 * tq), F32),
                        pltpu.VMEM((n_heads, hd2 + ONES_ROWS, 2 * tq), F32)],
        compiler_params=_params("parallel", "arbitrary"),
        name="cached_diff_attention",
    )(q, k_cur, v_cur, flat(k_past), flat(v_past), *_lambda_args(lq1, lk1, lq2, lk2, subln_g))


def _mem_attn_kernel(q_ref, k_ref, v_ref, o_ref):
    scale = jnp.asarray(MEM_HEAD_DIM ** -0.5, q_ref.dtype)
    for h in range(MEM_HEADS):
        sl = slice(h * MEM_HEAD_DIM, (h + 1) * MEM_HEAD_DIM)
        q = q_ref[0, :, sl] * scale
        s = lax.dot_general(q, k_ref[0, :, sl].astype(BF16), (((1,), (1,)), ((), ())), preferred_element_type=F32)
        p = jnp.exp(s - jnp.max(s, axis=1, keepdims=True))
        p = p / jnp.sum(p, axis=1, keepdims=True)
        o = jnp.dot(p.astype(BF16), v_ref[0, :, sl].astype(BF16), preferred_element_type=F32)
        o_ref[0, :, sl] = o.astype(o_ref.dtype)


def _mem_attention(mq, mk, mv):
    bsz, t_len, d_mem = mq.shape
    n_mem = mk.shape[1]
    tt = _tile(t_len, 512)
    return pl.pallas_call(
        _mem_attn_kernel,
        grid=(bsz, t_len // tt),
        in_specs=[pl.BlockSpec((1, tt, d_mem), lambda b, t: (b, t, 0)),
                  pl.BlockSpec((1, n_mem, d_mem), lambda b, t: (b, 0, 0)),
                  pl.BlockSpec((1, n_mem, d_mem), lambda b, t: (b, 0, 0))],
        out_specs=pl.BlockSpec((1, tt, d_mem), lambda b, t: (b, t, 0)),
        out_shape=jax.ShapeDtypeStruct((bsz, t_len, d_mem), BF16),
        compiler_params=_params("parallel", "parallel"),
        name="mem_attention",
    )(mq, mk, mv)


SLAB_DTYPE = F32


def _to_slabs(dst_ref, val, rows, n_slab):
    for c in range(n_slab):
        dst_ref[pl.ds(c, rows, stride=n_slab), :] = val[:, c * LANES:(c + 1) * LANES].astype(dst_ref.dtype)


def _slab_chunk(src_ref, c, rows, n_slab, row0=0):
    return src_ref[pl.ds(row0 * n_slab + c, rows, stride=n_slab), :]


def _router_kernel(*refs, tr, n_slab, group_tiles):
    n_grp = len(group_tiles)
    x_refs = refs[:n_grp]
    g_ref, whi_ref, wlo_ref, b_ref, hn_ref, eid_ref, wt_ref, rank_ref, cnt_ref, run_ref = refs[n_grp:]
    t = pl.program_id(0)
    x = x_refs[-1][...]
    first_tile = sum(group_tiles[:-1])
    for gi in range(n_grp - 2, -1, -1):
        x = jnp.where(t < first_tile, x_refs[gi][...], x)
        first_tile -= group_tiles[gi]
    hn = x * lax.rsqrt(jnp.mean(x * x, axis=-1, keepdims=True) + EPS) * g_ref[...]
    _to_slabs(hn_ref, hn, tr, n_slab)
    h_hi = hn.astype(BF16)
    h_lo = (hn - h_hi.astype(F32)).astype(BF16)
    logits = (jnp.dot(h_hi, whi_ref[...], preferred_element_type=F32)
              + jnp.dot(h_lo, whi_ref[...], preferred_element_type=F32)
              + jnp.dot(h_hi, wlo_ref[...], preferred_element_type=F32)) + b_ref[...]
    lane = lax.broadcasted_iota(jnp.int32, logits.shape, 1)
    first = lambda hit: jnp.min(jnp.where(hit, lane, LANES), axis=1, keepdims=True)

    gl = jnp.where(lane < N_GROUPS, logits, -jnp.inf)
    gmax = jnp.max(gl, axis=1, keepdims=True)
    g_idx = first(gl == gmax)
    p_g = 1.0 / jnp.sum(jnp.exp(gl - gmax), axis=1, keepdims=True)

    lo = N_GROUPS + EXPERTS_PER_GROUP * g_idx
    in_grp = (lane >= lo) & (lane < lo + EXPERTS_PER_GROUP)
    el = jnp.where(in_grp, logits, -jnp.inf)
    pe = jnp.exp(el - jnp.max(el, axis=1, keepdims=True))
    probs = jnp.where(in_grp, pe / jnp.sum(pe, axis=1, keepdims=True), -1.0)
    v1 = jnp.max(probs, axis=1, keepdims=True)
    i1 = first(probs == v1)
    probs2 = jnp.where(lane == i1, -1.0, probs)
    v2 = jnp.max(probs2, axis=1, keepdims=True)
    i2 = first(probs2 == v2)
    tot = v1 + v2
    eid_ref[:, 0:1] = i1 - N_GROUPS
    eid_ref[:, 1:2] = i2 - N_GROUPS
    wt_ref[:, 0:1] = p_g * (v1 / tot)
    wt_ref[:, 1:2] = p_g * (v2 / tot)

    @pl.when(t == 0)
    def _():
        run_ref[...] = jnp.zeros(run_ref.shape, F32)

    sel1 = lane == i1
    sel2 = lane == i2
    oh1 = jnp.where(sel1, 1.0, 0.0).astype(BF16)
    oh2 = jnp.where(sel2, 1.0, 0.0).astype(BF16)
    earlier = (lax.broadcasted_iota(jnp.int32, (tr, tr), 1) < lax.broadcasted_iota(jnp.int32, (tr, tr), 0))
    tri = jnp.where(earlier, 1.0, 0.0).astype(BF16)
    run = run_ref[...]
    cnt1 = jnp.sum(oh1.astype(F32), axis=0, keepdims=True)
    cnt2 = jnp.sum(oh2.astype(F32), axis=0, keepdims=True)
    before1 = jnp.dot(tri, oh1, preferred_element_type=F32) + run
    before2 = jnp.dot(tri, oh2, preferred_element_type=F32) + (run + cnt1)
    rank_ref[:, 0:1] = jnp.sum(jnp.where(sel1, before1, 0.0), axis=1, keepdims=True).astype(jnp.int32)
    rank_ref[:, 1:2] = jnp.sum(jnp.where(sel2, before2, 0.0), axis=1, keepdims=True).astype(jnp.int32)
    run = run + cnt1 + cnt2
    run_ref[...] = run
    cnt_ref[...] = run.astype(jnp.int32)


def _router(xs, g, w_rg, b_rg, w_re, b_re):
    d = xs[0].shape[1]
    n = sum(x.shape[0] for x in xs)
    n_slab = d // LANES
    tr = 256
    assert all(x.shape[0] % tr == 0 for x in xs)
    group_tiles = tuple(x.shape[0] // tr for x in xs)
    x_specs = []
    for gi, nt in enumerate(group_tiles):
        t0 = sum(group_tiles[:gi])
        x_specs.append(pl.BlockSpec((tr, d), lambda i, t0=t0, nt=nt: (jnp.clip(i - t0, 0, nt - 1), 0)))
    w = jnp.zeros((d, LANES), F32).at[:, :N_GROUPS].set(w_rg).at[:, N_GROUPS:N_GROUPS + N_EXPERTS].set(w_re)
    b = jnp.zeros((1, LANES), F32).at[0, :N_GROUPS].set(b_rg).at[0, N_GROUPS:N_GROUPS + N_EXPERTS].set(b_re)
    w_hi = w.astype(BF16)
    w_lo = (w - w_hi.astype(F32)).astype(BF16)
    small = lambda dt: jax.ShapeDtypeStruct((n, TOP_K), dt)
    small_spec = pl.BlockSpec((tr, TOP_K), lambda i: (i, 0))
    return pl.pallas_call(
        functools.partial(_router_kernel, tr=tr, n_slab=n_slab, group_tiles=group_tiles),
        grid=(n // tr,),
        in_specs=x_specs + [
                  pl.BlockSpec((1, d), lambda i: (0, 0)),
                  pl.BlockSpec((d, LANES), lambda i: (0, 0)),
                  pl.BlockSpec((d, LANES), lambda i: (0, 0)),
                  pl.BlockSpec((1, LANES), lambda i: (0, 0))],
        out_specs=[pl.BlockSpec((tr * n_slab, LANES), lambda i: (i, 0)), small_spec, small_spec, small_spec,
                   pl.BlockSpec((1, LANES), lambda i: (0, 0))],
        out_shape=[jax.ShapeDtypeStruct((n * n_slab, LANES), SLAB_DTYPE), small(jnp.int32), small(F32), small(jnp.int32),
                   jax.ShapeDtypeStruct((1, LANES), jnp.int32)],
        scratch_shapes=[pltpu.VMEM((1, LANES), F32)],
        compiler_params=_params("arbitrary"),
        name="moe_router",
    )(*xs, g.reshape(1, d).astype(F32), w_hi, w_lo, b)


def _dispatch_kernel(pos_ref, zstart_ref, zvalid_ref, n_active_ref, hn_ref, xs_out, zbuf, sem, zsem,
                     *, td, tm, n_slab, n_tiles):
    t = pl.program_id(0)
    zrows = tm * n_slab

    def zero_tile(row0):
        return pltpu.make_async_copy(zbuf, xs_out.at[pl.ds(pl.multiple_of(row0 * n_slab, n_slab), zrows), :], zsem)

    @pl.when(t == 0)
    def _():
        zbuf[...] = jnp.zeros(zbuf.shape, zbuf.dtype)

        def for_each_zero_tile(act):
            for e in range(N_EXPERTS):
                @pl.when(zvalid_ref[e] > 0)
                def _():
                    act(zero_tile(zstart_ref[e]))

            def unused(i, carry):
                act(zero_tile(i * tm))
                return carry
            lax.fori_loop(n_active_ref[0], n_tiles, unused, 0)

        for_each_zero_tile(lambda copy: copy.start())
        for_each_zero_tile(lambda copy: copy.wait())

    def body(r, carry):
        src = hn_ref.at[pl.ds(pl.multiple_of(r * n_slab, n_slab), n_slab), :]
        for kk in range(TOP_K):
            dst_row = pl.multiple_of(pos_ref[(t * td + r) * TOP_K + kk] * n_slab, n_slab)
            pltpu.make_async_copy(src, xs_out.at[pl.ds(dst_row, n_slab), :], sem).start(priority=kk % 2)
        return carry
    lax.fori_loop(0, td, body, 0)
    for kk in range(TOP_K):
        pltpu.make_async_copy(hn_ref, xs_out.at[pl.ds(0, td * n_slab), :], sem).wait()


def _expert_kernel(tile_expert_ref, n_active_ref, x_ref, wg_ref, wu_ref, wd_ref, ys_ref, xb_ref, wgb_ref, wub_ref,
                   wdb_ref, *, tm, n_slab):
    t = pl.program_id(0)
    active = t < n_active_ref[0]

    @pl.when(jnp.logical_not(active))
    def _():
        ys_ref[...] = jnp.zeros(ys_ref.shape, ys_ref.dtype)

    @pl.when(active & ((t == 0) | (tile_expert_ref[t] != tile_expert_ref[jnp.maximum(t - 1, 0)])))
    def _():
        wgb_ref[...] = wg_ref[0].astype(BF16)
        wub_ref[...] = wu_ref[0].astype(BF16)
        wdb_ref[...] = wd_ref[0].astype(BF16)

    @pl.when(active)
    def _():
        for c in range(n_slab):
            xb_ref[:, c * LANES:(c + 1) * LANES] = _slab_chunk(x_ref, c, tm, n_slab).astype(BF16)
        x = xb_ref[...]
        g = jnp.dot(x, wgb_ref[...], preferred_element_type=F32)
        u = jnp.dot(x, wub_ref[...], preferred_element_type=F32)
        act = (g * jax.nn.sigmoid(g) * u).astype(BF16)
        _to_slabs(ys_ref, jnp.dot(act, wdb_ref[...], preferred_element_type=F32), tm, n_slab)


def _start_slab_gather(pos_ref, tok0, ys_hbm, ybuf, sem, slot, tc, n_slab):
    def body(r, carry):
        for kk in range(TOP_K):
            src_row = pl.multiple_of(pos_ref[(tok0 + r) * TOP_K + kk] * n_slab, n_slab)
            dst_row = pl.multiple_of((kk * tc + r) * n_slab, n_slab)
            pltpu.make_async_copy(ys_hbm.at[pl.ds(src_row, n_slab), :],
                                  ybuf.at[slot, pl.ds(dst_row, n_slab), :], sem.at[slot]).start(priority=kk % 2)
        return carry
    lax.fori_loop(0, tc, body, 0)


def _combine_kernel(pos_ref, x_ref, w_ref, g_ref, ys_hbm, o_ref, ybuf, sem, *, tc, tok_base, n_slab):
    t = pl.program_id(0)
    nt = pl.num_programs(0)
    slot = lax.rem(t, 2)

    @pl.when(t == 0)
    def _():
        _start_slab_gather(pos_ref, tok_base, ys_hbm, ybuf, sem, 0, tc, n_slab)

    @pl.when(t + 1 < nt)
    def _():
        _start_slab_gather(pos_ref, tok_base + (t + 1) * tc, ys_hbm, ybuf, sem, 1 - slot, tc, n_slab)

    pltpu.make_async_copy(ys_hbm.at[pl.ds(0, TOP_K * tc * n_slab), :], ybuf.at[slot], sem.at[slot]).wait()
    yb = ybuf.at[slot]
    wk = [jnp.broadcast_to(w_ref[:, kk:kk + 1], (tc, LANES)) for kk in range(TOP_K)]
    sq = jnp.zeros((tc, LANES), F32)
    for c in range(n_slab):
        moe = wk[0] * _slab_chunk(yb, c, tc, n_slab)
        for kk in range(1, TOP_K):
            moe = moe + wk[kk] * _slab_chunk(yb, c, tc, n_slab, row0=kk * tc)
        y = x_ref[:, c * LANES:(c + 1) * LANES] + moe
        o_ref[:, c * LANES:(c + 1) * LANES] = y
        sq = sq + y * y
    d = x_ref.shape[1]
    scale = lax.rsqrt(jnp.sum(sq, axis=1, keepdims=True) / d + EPS)
    o_ref[...] = o_ref[...] * scale * g_ref[...]


def _hier_moe_final(xs, norm_ffn, w_rg, b_rg, w_re, b_re, w_g, w_u, w_d, norm_final):
    d = xs[0].shape[1]
    n = sum(x.shape[0] for x in xs)
    n_slab = d // LANES
    d_exp = w_g.shape[2]
    hn, eid, wts, rank, cnt = _router(xs, norm_ffn, w_rg, b_rg, w_re, b_re)

    tm = 256
    na = n * TOP_K
    n_tiles = (na + N_EXPERTS * (tm - 1)) // tm + 1
    n_rows = n_tiles * tm
    counts = cnt[0, N_GROUPS:N_GROUPS + N_EXPERTS]
    padded = ((counts + tm - 1) // tm) * tm
    pend = jnp.cumsum(padded)
    pstart = pend - padded
    experts = jnp.arange(N_EXPERTS, dtype=jnp.int32)
    pos = (jnp.sum(jnp.where(eid[:, :, None] == experts, pstart, 0), axis=-1) + rank).reshape(na)
    tile_start = jnp.arange(n_tiles, dtype=jnp.int32) * tm
    tile_expert = jnp.minimum(jnp.sum((pend[None, :] <= tile_start[:, None]).astype(jnp.int32), axis=1), N_EXPERTS - 1)
    n_active = (pend[-1] // tm).astype(jnp.int32).reshape(1)
    zstart = jnp.maximum(pend - tm, 0).astype(jnp.int32)
    zvalid = (counts > 0).astype(jnp.int32)

    td = _tile(n, 256)
    x_sorted = pl.pallas_call(
        functools.partial(_dispatch_kernel, td=td, tm=tm, n_slab=n_slab, n_tiles=n_tiles),
        grid_spec=pltpu.PrefetchScalarGridSpec(
            num_scalar_prefetch=4,
            grid=(n // td,),
            in_specs=[pl.BlockSpec((td * n_slab, LANES), lambda t, p, zs, zv, na_: (t, 0))],
            out_specs=pl.BlockSpec(memory_space=pl.ANY),
            scratch_shapes=[pltpu.VMEM((tm * n_slab, LANES), SLAB_DTYPE), pltpu.SemaphoreType.DMA(()),
                            pltpu.SemaphoreType.DMA(())]),
        out_shape=jax.ShapeDtypeStruct((n_rows * n_slab, LANES), SLAB_DTYPE),
        compiler_params=_params("arbitrary"),
        name="moe_dispatch",
    )(pos, zstart, zvalid, n_active, hn)

    active = lambda t, te, na_: (jnp.minimum(t, na_[0] - 1), 0)
    ys = pl.pallas_call(
        functools.partial(_expert_kernel, tm=tm, n_slab=n_slab),
        grid_spec=pltpu.PrefetchScalarGridSpec(
            num_scalar_prefetch=2,
            grid=(n_tiles,),
            in_specs=[pl.BlockSpec((tm * n_slab, LANES), active),
                      pl.BlockSpec((1, d, d_exp), lambda t, te, na_: (te[t], 0, 0)),
                      pl.BlockSpec((1, d, d_exp), lambda t, te, na_: (te[t], 0, 0)),
                      pl.BlockSpec((1, d_exp, d), lambda t, te, na_: (te[t], 0, 0))],
            out_specs=pl.BlockSpec((tm * n_slab, LANES), lambda t, te, na_: (t, 0)),
            scratch_shapes=[pltpu.VMEM((tm, d), BF16), pltpu.VMEM((d, d_exp), BF16), pltpu.VMEM((d, d_exp), BF16),
                            pltpu.VMEM((d_exp, d), BF16)]),
        out_shape=jax.ShapeDtypeStruct((n_rows * n_slab, LANES), SLAB_DTYPE),
        compiler_params=_params("arbitrary"),
        name="moe_experts",
    )(tile_expert, n_active, x_sorted, w_g, w_u, w_d)

    outs = []
    row0 = 0
    for x in xs:
        rows = x.shape[0]
        tc = _tile(rows, 256)
        assert row0 % tc == 0
        blk0 = row0 // tc
        outs.append(pl.pallas_call(
            functools.partial(_combine_kernel, tc=tc, tok_base=row0, n_slab=n_slab),
            grid_spec=pltpu.PrefetchScalarGridSpec(
                num_scalar_prefetch=1,
                grid=(rows // tc,),
                in_specs=[pl.BlockSpec((tc, d), lambda t, p: (t, 0)),
                          pl.BlockSpec((tc, TOP_K), lambda t, p, blk0=blk0: (blk0 + t, 0)),
                          pl.BlockSpec((1, d), lambda t, p: (0, 0)),
                          pl.BlockSpec(memory_space=pl.ANY)],
                out_specs=pl.BlockSpec((tc, d), lambda t, p: (t, 0)),
                scratch_shapes=[pltpu.VMEM((2, TOP_K * tc * n_slab, LANES), SLAB_DTYPE),
                                pltpu.SemaphoreType.DMA((2,))]),
            out_shape=jax.ShapeDtypeStruct((rows, d), F32),
            compiler_params=_params("arbitrary"),
            name="moe_combine",
        )(pos, x, wts, norm_final.reshape(1, d).astype(F32), ys))
        row0 += rows
    return outs


def _mixer(groups, lam_init, w):
    d = groups[0][0].shape[2]
    d_conv = w["w_dw"].shape[1]
    d_attn = w["w_attn_out"].shape[0]
    d_mem = w["w_mem_out"].shape[0]
    c0 = 2 * d_conv
    c1, c2, c3 = c0 + d_attn, c0 + 2 * d_attn, c0 + 3 * d_attn
    c4 = c3 + d_mem
    x2ds = [x.reshape(-1, d) for x, *_ in groups]

    hs = [_rmsnorm(x2d, w["norm_mix"], BF16) for x2d in x2ds]
    w_in = w["w_in"]
    u_ins = _matmul(hs, w_in, F32, b_col0=0, n_out=c0)
    qs = _matmul(hs, w_in, F32, b_col0=c0, n_out=d_attn)
    k_rows = _matmul(hs, w_in, F32, b_col0=c1, n_out=d_attn)
    v_rows = _matmul(hs, w_in, F32, b_col0=c2, n_out=d_attn)
    mqs = _matmul(hs, w_in, BF16, b_col0=c3, n_out=d_mem)
    gates = _matmul(hs, w_in, BF16, b_col0=c4, n_out=N_BRANCH * d)

    lam_args = (w["lambda_q1"], w["lambda_k1"], w["lambda_q2"], w["lambda_k2"], w["subln_g"], lam_init)
    conv_acts, new_bufs, os_, mos = [], [], [], []
    for g, (x, conv_buf, k_past, v_past, mk, mv) in enumerate(groups):
        bsz, t_len, _ = x.shape
        conv_act, new_buf = _conv_branch(u_ins[g].reshape(bsz, t_len, c0), conv_buf, w["w_dw"], w["b_dw"],
                                         w["conv_ln_g"], w["conv_ln_b"])
        qkv = [a.reshape(bsz, t_len, d_attn) for a in (qs[g], k_rows[g], v_rows[g])]
        if k_past is None:
            o = _diff_attention(*qkv, *lam_args)
        else:
            o = _cached_diff_attention(*qkv, k_past, v_past, *lam_args)
        mo = _mem_attention(mqs[g].reshape(bsz, t_len, d_mem), mk, mv)
        conv_acts.append(conv_act.reshape(-1, d_conv))
        new_bufs.append(new_buf)
        os_.append(o.reshape(-1, d_attn))
        mos.append(mo.reshape(-1, d_mem))

    merged = _matmul(conv_acts, w["w_conv_out"], BF16, gates=gates, gate_col0=0)
    merged = _matmul(os_, w["w_attn_out"], BF16, gates=gates, gate_col0=d, adds=merged)
    merged = _matmul(mos, w["w_mem_out"], BF16, gates=gates, gate_col0=2 * d, adds=merged)
    x2s = _matmul(merged, w["w_out"], F32, adds=x2ds)
    hd2 = 2 * HEAD_DIM
    rows4d = lambda r, x: r.reshape(x.shape[0], x.shape[1], N_HEADS, hd2)
    return [(x2s[g], new_bufs[g], rows4d(k_rows[g], x), rows4d(v_rows[g], x)) for g, (x, *_) in enumerate(groups)]


def kernel(x_prompt, x_sample, mem_prompt, cache_conv, cache_diff_k, cache_diff_v, cache_mem_k, cache_mem_v, norm_mix, w_in, w_dw, b_dw, conv_ln_g, conv_ln_b, w_conv_out, lambda_q1, lambda_k1, lambda_q2, lambda_k2, subln_g, w_attn_out, norm_mem, w_mem_k, w_mem_v, w_mem_out, w_out, norm_ffn, w_router_grp, b_router_grp, w_router_exp, b_router_exp, w_exp_gate, w_exp_up, w_exp_down, norm_final):
    depth = w_in.shape[0]
    assert depth == 1, "the final rmsnorm is fused into the last layer's MoE combine"
    bp, tp, d = x_prompt.shape
    bs, ts, _ = x_sample.shape
    n_mem = mem_prompt.shape[1]
    d_conv = w_dw.shape[2]
    d_mem = w_mem_k.shape[2]
    l = 0
    lam_init = 0.8 - 0.6 * math.exp(-0.3 * l)
    w = dict(norm_mix=norm_mix[l], w_in=w_in[l], w_dw=w_dw[l], b_dw=b_dw[l], conv_ln_g=conv_ln_g[l],
             conv_ln_b=conv_ln_b[l], w_conv_out=w_conv_out[l], lambda_q1=lambda_q1[l],
             lambda_k1=lambda_k1[l], lambda_q2=lambda_q2[l], lambda_k2=lambda_k2[l], subln_g=subln_g[l],
             w_attn_out=w_attn_out[l], w_mem_out=w_mem_out[l], w_out=w_out[l])

    hm = _rmsnorm(mem_prompt.reshape(bp * n_mem, d), norm_mem[l], BF16)
    mk = _matmul([hm], w_mem_k[l], F32)[0].reshape(bp, n_mem, d_mem)
    mv = _matmul([hm], w_mem_v[l], F32)[0].reshape(bp, n_mem, d_mem)
    conv0 = jnp.zeros((bp, CONV_W - 1, d_conv), F32)
    prompt = (x_prompt, conv0, None, None, mk, mv)
    sample = (x_sample, cache_conv.reshape(cache_conv.shape[1:]), cache_diff_k.reshape(cache_diff_k.shape[1:]),
              cache_diff_v.reshape(cache_diff_v.shape[1:]), cache_mem_k.reshape(bs, n_mem, d_mem),
              cache_mem_v.reshape(bs, n_mem, d_mem))
    (x2_p, cb_p, k_p, v_p), (x2_s, cb_s, k_s, v_s) = _mixer([prompt, sample], lam_init, w)

    y_p, y_s = _hier_moe_final([x2_p, x2_s], norm_ffn[l], w_router_grp[l], b_router_grp[l], w_router_exp[l],
                               b_router_exp[l], w_exp_gate[l], w_exp_up[l], w_exp_down[l], norm_final)
    mem_shape = (1, bp, n_mem, MEM_HEADS, MEM_HEAD_DIM)
    return (y_p.reshape(bp, tp, d), y_s.reshape(bs, ts, d), cb_p[None], k_p[None], v_p[None],
            mk.reshape(mem_shape), mv.reshape(mem_shape), cb_s[None], k_s[None], v_s[None])
```

```python
import functools
import math

import jax
import jax.numpy as jnp
from jax import lax
from jax.experimental import pallas as pl
from jax.experimental.pallas import tpu as pltpu

F32 = jnp.float32
BF16 = jnp.bfloat16

CHUNK = 64
EPS = 1e-6
SUBLN_EPS = 1e-5
NEG = -1e30
CONV_W = 31
N_HEADS = 16
HEAD_DIM = 64
MEM_HEADS = 4
MEM_HEAD_DIM = 256
N_BRANCH = 3
N_GROUPS = 4
EXPERTS_PER_GROUP = 8
N_EXPERTS = N_GROUPS * EXPERTS_PER_GROUP
TOP_K = 2

LANES = 128
SUBLANES = 8
VMEM_LIMIT_BYTES = 56 * 1024 * 1024
HALO = 32


def _params(*semantics):
    return pltpu.CompilerParams(dimension_semantics=semantics, vmem_limit_bytes=VMEM_LIMIT_BYTES)


def _tile(n, pref):
    if n <= pref:
        return n
    t = pref
    while n % t:
        t //= 2
    return t


def _rmsnorm_kernel(x_ref, g_ref, o_ref, *, eps):
    x = x_ref[...].astype(F32)
    ms = jnp.mean(x * x, axis=-1, keepdims=True)
    o_ref[...] = (x * lax.rsqrt(ms + eps) * g_ref[...]).astype(o_ref.dtype)


def _rmsnorm(x, g, out_dtype, eps=EPS):
    n, d = x.shape
    tr = _tile(n, 512)
    return pl.pallas_call(
        functools.partial(_rmsnorm_kernel, eps=eps),
        grid=(n // tr,),
        in_specs=[pl.BlockSpec((tr, d), lambda i: (i, 0)), pl.BlockSpec((1, d), lambda i: (0, 0))],
        out_specs=pl.BlockSpec((tr, d), lambda i: (i, 0)),
        out_shape=jax.ShapeDtypeStruct((n, d), out_dtype),
        compiler_params=_params("parallel"),
        name="rmsnorm",
    )(x, g.reshape(1, d).astype(F32))


def _mm_kernel(*refs, n_grp, tile_ranges, has_gate, has_add):
    a_refs, b_ref = refs[:n_grp], refs[n_grp]
    pos = n_grp + 1
    gate_refs = refs[pos:pos + n_grp] if has_gate else None
    pos += n_grp * has_gate
    add_refs = refs[pos:pos + n_grp] if has_add else None
    o_refs, b_bf16 = refs[-1 - n_grp:-1], refs[-1]
    i = pl.program_id(1)

    @pl.when(i == 0)
    def _():
        b_bf16[...] = b_ref[...].astype(BF16)

    for g, (t0, t1) in enumerate(tile_ranges):
        @pl.when((i >= t0) & (i < t1))
        def _():
            acc = jnp.dot(a_refs[g][...], b_bf16[...], preferred_element_type=F32)
            if has_gate:
                acc = acc * jax.nn.sigmoid(gate_refs[g][...].astype(F32))
            if has_add:
                acc = acc + add_refs[g][...].astype(F32)
            o_refs[g][...] = acc.astype(o_refs[g].dtype)


def _matmul(a_groups, b, out_dtype, *, b_col0=0, n_out=None, gates=None, gate_col0=0, adds=None):
    k = a_groups[0].shape[1]
    assert all(a.dtype == BF16 for a in a_groups) and b.dtype == F32
    n_out = b.shape[1] if n_out is None else n_out
    tn = _tile(n_out, 1024)
    assert b_col0 % tn == 0 and gate_col0 % tn == 0
    bj0, gj0 = b_col0 // tn, gate_col0 // tn
    tms = [_tile(a.shape[0], 1024) for a in a_groups]
    tiles = [a.shape[0] // tm for a, tm in zip(a_groups, tms)]
    starts = [sum(tiles[:g]) for g in range(len(tiles))]
    tile_ranges = tuple((t0, t0 + nt) for t0, nt in zip(starts, tiles))

    def rows(g, col0=None):
        t0, nt = starts[g], tiles[g]
        if col0 is None:
            return lambda j, i: (jnp.clip(i - t0, 0, nt - 1), 0)
        return lambda j, i: (jnp.clip(i - t0, 0, nt - 1), col0 + j)

    n_grp = len(a_groups)
    in_specs = [pl.BlockSpec((tms[g], k), rows(g)) for g in range(n_grp)]
    size = lambda dt: jnp.dtype(dt).itemsize
    per_row = k * size(BF16) + tn * (size(out_dtype) + (size(gates[0].dtype) if gates is not None else 0)
                                     + (size(adds[0].dtype) if adds is not None else 0))
    estimate = 2 * sum(tms) * per_row + k * tn * (2 * size(F32) + size(BF16)) + max(tms) * tn * size(F32)
    weight_buffers = 2 if estimate <= VMEM_LIMIT_BYTES else 1
    in_specs.append(pl.BlockSpec((k, tn), lambda j, i: (0, bj0 + j), pipeline_mode=pl.Buffered(weight_buffers)))
    args = list(a_groups) + [b]
    if gates is not None:
        in_specs += [pl.BlockSpec((tms[g], tn), rows(g, gj0)) for g in range(n_grp)]
        args += list(gates)
    if adds is not None:
        in_specs += [pl.BlockSpec((tms[g], tn), rows(g, 0)) for g in range(n_grp)]
        args += list(adds)
    return pl.pallas_call(
        functools.partial(_mm_kernel, n_grp=n_grp, tile_ranges=tile_ranges, has_gate=gates is not None,
                          has_add=adds is not None),
        grid=(n_out // tn, sum(tiles)),
        in_specs=in_specs,
        out_specs=[pl.BlockSpec((tms[g], tn), rows(g, 0)) for g in range(n_grp)],
        out_shape=[jax.ShapeDtypeStruct((a.shape[0], n_out), out_dtype) for a in a_groups],
        scratch_shapes=[pltpu.VMEM((k, tn), BF16)],
        compiler_params=_params("parallel", "arbitrary"),
        name="matmul",
    )(*args)


def _conv_kernel(u_ref, buf_ref, w_ref, bdw_ref, g_ref, b_ref, act_ref, newbuf_ref, ext_ref, shift_ref, *, tt, dc):
    t = pl.program_id(1)
    hist = CONV_W - 1

    @pl.when(t == 0)
    def _():
        ext_ref[pl.ds(HALO - hist, hist), :] = buf_ref[0].astype(F32)

    @pl.when(t > 0)
    def _():
        ext_ref[pl.ds(0, HALO), :] = ext_ref[pl.ds(tt, HALO), :]

    a = u_ref[0, :, :dc].astype(F32)
    gt = u_ref[0, :, dc:].astype(F32)
    ext_ref[pl.ds(HALO, tt), :] = a * jax.nn.sigmoid(gt)

    n_ext = HALO + tt
    ext = ext_ref[...]
    for b in range(1, SUBLANES):
        shift_ref[b - 1] = pltpu.roll(ext, n_ext - b, axis=0)
    y = jnp.zeros((tt, dc), F32) + bdw_ref[...]
    for j in range(CONV_W):
        a, b = divmod(HALO - hist + j, SUBLANES)
        src = ext_ref if b == 0 else shift_ref.at[b - 1]
        y = y + src[pl.ds(a * SUBLANES, tt), :] * w_ref[pl.ds(j, 1), :]
    mu = jnp.mean(y, axis=-1, keepdims=True)
    var = jnp.mean(jnp.square(y - mu), axis=-1, keepdims=True)
    z = (y - mu) * lax.rsqrt(var + EPS) * g_ref[...] + b_ref[...]
    act_ref[0] = (z * jax.nn.sigmoid(z)).astype(act_ref.dtype)
    newbuf_ref[0] = ext_ref[pl.ds(HALO + tt - hist, hist), :]


def _conv_branch(u_in, conv_buf, w_dw, b_dw, ln_g, ln_b):
    bsz, t_len, two_dc = u_in.shape
    dc = two_dc // 2
    tt = _tile(t_len, 256)
    assert tt >= HALO and tt % SUBLANES == 0
    vec = lambda v: v.reshape(1, dc).astype(F32)
    return pl.pallas_call(
        functools.partial(_conv_kernel, tt=tt, dc=dc),
        grid=(bsz, t_len // tt),
        in_specs=[pl.BlockSpec((1, tt, two_dc), lambda b, t: (b, t, 0)),
                  pl.BlockSpec((1, CONV_W - 1, dc), lambda b, t: (b, 0, 0)),
                  pl.BlockSpec((CONV_W, dc), lambda b, t: (0, 0)),
                  pl.BlockSpec((1, dc), lambda b, t: (0, 0)),
                  pl.BlockSpec((1, dc), lambda b, t: (0, 0)),
                  pl.BlockSpec((1, dc), lambda b, t: (0, 0))],
        out_specs=[pl.BlockSpec((1, tt, dc), lambda b, t: (b, t, 0)),
                   pl.BlockSpec((1, CONV_W - 1, dc), lambda b, t: (b, 0, 0))],
        out_shape=[jax.ShapeDtypeStruct((bsz, t_len, dc), BF16),
                   jax.ShapeDtypeStruct((bsz, CONV_W - 1, dc), F32)],
        scratch_shapes=[pltpu.VMEM((HALO + tt, dc), F32), pltpu.VMEM((SUBLANES - 1, HALO + tt, dc), F32)],
        compiler_params=_params("parallel", "arbitrary"),
        name="conv_branch",
    )(u_in, conv_buf, w_dw.astype(F32), vec(b_dw), vec(ln_g), vec(ln_b))


ONES_ROWS = 16
ATTN_TQ = 1024
ATTN_LANE_GROUP = 512
ATTN_TK_CACHED = 512
ATTN_HEAD_GROUP = 4
LOG2E = 1.4426950408889634


def _stacked_qt(q):
    tq = q.shape[0]
    qt = (q * (HEAD_DIM ** -0.5 * LOG2E)).T
    z = jnp.zeros((HEAD_DIM, tq), F32)
    return jnp.concatenate([jnp.concatenate([qt[:HEAD_DIM], z], axis=0),
                            jnp.concatenate([z, qt[HEAD_DIM:]], axis=0)], axis=1).astype(BF16)


def _vt_ones(v):
    vt = v.astype(F32).T.astype(BF16)
    return jnp.concatenate([vt, jnp.ones((ONES_ROWS, v.shape[0]), BF16)], axis=0)


def _scores(k_blk, qqt):
    return jnp.dot(k_blk, qqt, preferred_element_type=F32)


def _softmax_step(m_prev, acc_prev, s, vt_blk):
    m_new = jnp.maximum(m_prev, jnp.max(s, axis=0, keepdims=True))
    alpha = jnp.exp2(m_prev - m_new)
    p = jnp.exp2(s - m_new).astype(BF16)
    return m_new, alpha * acc_prev + jnp.dot(vt_blk, p, preferred_element_type=F32)


def _mask_aligned_block(s, lane0, tq):
    q_chunk = lax.rem(lax.broadcasted_iota(jnp.int32, (1, s.shape[1]), 1) + lane0, tq) // CHUNK
    bands = [jnp.where(q_chunk >= c, s[c * CHUNK:(c + 1) * CHUNK, :], NEG) for c in range(s.shape[0] // CHUNK)]
    return jnp.concatenate(bands, axis=0)


def _diff_finalize(acc, lam, g_col, tq, lam_init):
    hd2 = 2 * HEAD_DIM
    inv_l = 1.0 / acc[hd2:hd2 + 1, :]
    ot = acc[:hd2, :tq] * inv_l[:, :tq] - lam * (acc[:hd2, tq:] * inv_l[:, tq:])
    ot = ot * lax.rsqrt(jnp.mean(ot * ot, axis=0, keepdims=True) + SUBLN_EPS) * g_col
    return (ot * (1.0 - lam_init)).T


def _lambda(lq1_ref, lk1_ref, lq2_ref, lk2_ref, lam_init):
    return (jnp.exp(jnp.sum(lq1_ref[...] * lk1_ref[...], axis=1, keepdims=True))
            - jnp.exp(jnp.sum(lq2_ref[...] * lk2_ref[...], axis=1, keepdims=True)) + lam_init)


def _diff_attn_kernel(q_ref, k_ref, v_ref, lq1_ref, lk1_ref, lq2_ref, lk2_ref, g_ref, o_ref, k_scr, vt_scr, qq_scr,
                      acc_ref, s_ref, *, n_blk, tq, lam_init):
    g = pl.program_id(2)

    @pl.when(g == 0)
    def _():
        def body(i, carry):
            off = pl.multiple_of(i * tq, tq)
            k_scr[i] = k_ref[0, pl.ds(off, tq), :].astype(BF16)
            vt_scr[i] = _vt_ones(v_ref[0, pl.ds(off, tq), :])
            return carry
        lax.fori_loop(0, n_blk, body, 0)

    for r in range(2):
        qq_scr[r] = _stacked_qt(q_ref[0, r * tq:(r + 1) * tq, :].astype(F32))
    acc_ref[...] = jnp.zeros(acc_ref.shape, F32)
    lg = min(ATTN_LANE_GROUP, tq)
    n_lg = 2 * tq // lg
    group = lambda c: slice(c * lg, (c + 1) * lg)

    def keys_needed(c, masked):
        return (c % (tq // lg) + 1) * lg if masked else tq

    def issue(kb, r, c, masked=False):
        nk = keys_needed(c, masked)
        s_ref[:nk, group(c)] = _scores(k_scr[kb, :nk], qq_scr[r, :, group(c)])

    def block(kb, r, m_prev, masked, then):
        m_new = []
        for c in range(n_lg):
            if c + 1 < n_lg:
                issue(kb, r, c + 1, masked)
            elif then is not None:
                issue(then[0], then[1], 0)
            nk = keys_needed(c, masked)
            s = s_ref[:nk, group(c)]
            if masked:
                s = _mask_aligned_block(s, c * lg, tq)
            m_c, acc = _softmax_step(m_prev[c], acc_ref[r, :, group(c)], s, vt_scr[kb, :, :nk])
            acc_ref[r, :, group(c)] = acc
            m_new.append(m_c)
        return tuple(m_new)

    def query_block(r, n_full_pairs, odd_full, then):
        def two_blocks(i2, m_prev):
            m_mid = block(2 * i2, r, m_prev, False, (2 * i2 + 1, r))
            return block(2 * i2 + 1, r, m_mid, False, (2 * i2 + 2, r))
        m = lax.fori_loop(0, n_full_pairs, two_blocks, tuple(jnp.full((1, lg), NEG, F32) for _ in range(n_lg)))
        if odd_full:
            m = block(2 * n_full_pairs, r, m, False, (2 * n_full_pairs + 1, r))
        block(2 * n_full_pairs + odd_full, r, m, True, then)
        lam = _lambda(lq1_ref, lk1_ref, lq2_ref, lk2_ref, lam_init)
        o_ref[0, r * tq:(r + 1) * tq, :] = _diff_finalize(acc_ref[r], lam, g_ref[...], tq, lam_init).astype(o_ref.dtype)

    issue(0, 0, 0)
    query_block(0, g, 0, (0, 1))
    query_block(1, g, 1, None)


def _cached_attn_kernel(q_ref, kc_ref, vc_ref, kp_ref, vp_ref, lq1_ref, lk1_ref, lq2_ref, lk2_ref, g_ref, o_ref,
                        qq_scr, m_scr, acc_scr, *, n_heads, tq, tkp, lam_init):
    j = pl.program_id(1)
    hd2 = 2 * HEAD_DIM
    head = lambda h: slice(h * hd2, (h + 1) * hd2)

    @pl.when(j == 0)
    def _():
        for h in range(n_heads):
            qq_scr[h] = _stacked_qt(q_ref[0, :, head(h)].astype(F32))
        m_scr[...] = jnp.full(m_scr.shape, NEG, F32)
        acc_scr[...] = jnp.zeros(acc_scr.shape, F32)

    def update(heads, keys, values, masked):
        ss = [_scores(keys(h).astype(BF16), qq_scr[h]) for h in heads]
        if masked:
            ss = [_mask_aligned_block(s, 0, tq) for s in ss]
        vts = [_vt_ones(values(h)) for h in heads]
        res = [_softmax_step(m_scr[h], acc_scr[h], s, vt) for h, s, vt in zip(heads, ss, vts)]
        for h, (m_new, acc) in zip(heads, res):
            m_scr[h] = m_new
            acc_scr[h] = acc

    head_groups = [range(h0, min(h0 + ATTN_HEAD_GROUP, n_heads)) for h0 in range(0, n_heads, ATTN_HEAD_GROUP)]
    for heads in head_groups:
        update(heads, lambda h: kp_ref[0, pl.ds(h, tkp, stride=n_heads), :],
               lambda h: vp_ref[0, pl.ds(h, tkp, stride=n_heads), :], False)

    @pl.when(j == pl.num_programs(1) - 1)
    def _():
        lam = _lambda(lq1_ref, lk1_ref, lq2_ref, lk2_ref, lam_init)
        for heads in head_groups:
            update(heads, lambda h: kc_ref[0, :, head(h)], lambda h: vc_ref[0, :, head(h)], True)
            for h in heads:
                o_ref[0, :, head(h)] = _diff_finalize(acc_scr[h], lam, g_ref[...], tq, lam_init).astype(o_ref.dtype)


def _lambda_args(lq1, lk1, lq2, lk2, subln_g):
    vec = lambda v: v.reshape(1, -1).astype(F32)
    return [vec(lq1), vec(lk1), vec(lq2), vec(lk2), subln_g.reshape(-1, 1).astype(F32)]


def _diff_attention(q, k_cur, v_cur, lq1, lk1, lq2, lk2, subln_g, lam_init):
    bsz, t_len, d_attn = q.shape
    hd2 = 2 * HEAD_DIM
    tq = _tile(t_len // 2, ATTN_TQ)
    n_blk = t_len // tq
    assert tq % CHUNK == 0 and n_blk % 2 == 0 and tq % min(ATTN_LANE_GROUP, tq) == 0
    const = lambda shape: pl.BlockSpec(shape, lambda b, h, i: (0, 0))
    return pl.pallas_call(
        functools.partial(_diff_attn_kernel, n_blk=n_blk, tq=tq, lam_init=lam_init),
        grid=(bsz, d_attn // hd2, n_blk // 2),
        in_specs=[pl.BlockSpec((1, 2 * tq, hd2), lambda b, h, i: (b, i, h)),
                  pl.BlockSpec((1, t_len, hd2), lambda b, h, i: (b, 0, h)),
                  pl.BlockSpec((1, t_len, hd2), lambda b, h, i: (b, 0, h))]
        + [const((1, HEAD_DIM))] * 4 + [const((hd2, 1))],
        out_specs=pl.BlockSpec((1, 2 * tq, hd2), lambda b, h, i: (b, i, h)),
        out_shape=jax.ShapeDtypeStruct((bsz, t_len, d_attn), BF16),
        scratch_shapes=[pltpu.VMEM((n_blk, tq, hd2), BF16), pltpu.VMEM((n_blk, hd2 + ONES_ROWS, tq), BF16),
                        pltpu.VMEM((2, hd2, 2 * tq), BF16), pltpu.VMEM((2, hd2 + ONES_ROWS, 2 * tq), F32),
                        pltpu.VMEM((tq, 2 * tq), F32)],
        compiler_params=_params("parallel", "parallel", "arbitrary"),
        name="diff_attention",
    )(q, k_cur, v_cur, *_lambda_args(lq1, lk1, lq2, lk2, subln_g))


def _cached_diff_attention(q, k_cur, v_cur, k_past, v_past, lq1, lk1, lq2, lk2, subln_g, lam_init):
    bsz, t_len, d_attn = q.shape
    hd2 = 2 * HEAD_DIM
    past_len, n_heads = k_past.shape[1], k_past.shape[2]
    tq = t_len
    tkp = _tile(past_len, ATTN_TK_CACHED)
    assert tq % CHUNK == 0 and past_len % CHUNK == 0 and tq <= ATTN_TQ and n_heads * hd2 == d_attn
    assert n_heads % SUBLANES == 0
    flat = lambda c: c.reshape(bsz, past_len * n_heads, hd2)
    const = lambda shape: pl.BlockSpec(shape, lambda b, j: (0, 0))
    rows = pl.BlockSpec((1, t_len, d_attn), lambda b, j: (b, 0, 0))
    hist = pl.BlockSpec((1, tkp * n_heads, hd2), lambda b, j: (b, j, 0))
    return pl.pallas_call(
        functools.partial(_cached_attn_kernel, n_heads=n_heads, tq=tq, tkp=tkp, lam_init=lam_init),
        grid=(bsz, past_len // tkp),
        in_specs=[rows, rows, rows, hist, hist] + [const((1, HEAD_DIM))] * 4 + [const((hd2, 1))],
        out_specs=rows,
        out_shape=jax.ShapeDtypeStruct((bsz, t_len, d_attn), BF16),
        scratch_shapes=[pltpu.VMEM((n_heads, hd2, 2 * tq), BF16), pltpu.VMEM((n_heads, 1, 2 * tq), F32),
                        pltpu.VMEM((n_heads, hd2 + ONES_ROWS, 2 * tq), F32)],
        compiler_params=_params("parallel", "arbitrary"),
        name="cached_diff_attention",
    )(q, k_cur, v_cur, flat(k_past), flat(v_past), *_lambda_args(lq1, lk1, lq2, lk2, subln_g))


def _mem_attn_kernel(q_ref, k_ref, v_ref, o_ref):
    scale = jnp.asarray(MEM_HEAD_DIM ** -0.5, q_ref.dtype)
    for h in range(MEM_HEADS):
        sl = slice(h * MEM_HEAD_DIM, (h + 1) * MEM_HEAD_DIM)
        q = q_ref[0, :, sl] * scale
        s = lax.dot_general(q, k_ref[0, :, sl].astype(BF16), (((1,), (1,)), ((), ())), preferred_element_type=F32)
        p = jnp.exp(s - jnp.max(s, axis=1, keepdims=True))
        p = p / jnp.sum(p, axis=1, keepdims=True)
        o = jnp.dot(p.astype(BF16), v_ref[0, :, sl].astype(BF16), preferred_element_type=F32)
        o_ref[0, :, sl] = o.astype(o_ref.dtype)


def _mem_attention(mq, mk, mv):
    bsz, t_len, d_mem = mq.shape
    n_mem = mk.shape[1]
    tt = _tile(t_len, 512)
    return pl.pallas_call(
        _mem_attn_kernel,
        grid=(bsz, t_len // tt),
        in_specs=[pl.BlockSpec((1, tt, d_mem), lambda b, t: (b, t, 0)),
                  pl.BlockSpec((1, n_mem, d_mem), lambda b, t: (b, 0, 0)),
                  pl.BlockSpec((1, n_mem, d_mem), lambda b, t: (b, 0, 0))],
        out_specs=pl.BlockSpec((1, tt, d_mem), lambda b, t: (b, t, 0)),
        out_shape=jax.ShapeDtypeStruct((bsz, t_len, d_mem), BF16),
        compiler_params=_params("parallel", "parallel"),
        name="mem_attention",
    )(mq, mk, mv)


SLAB_DTYPE = F32


def _to_slabs(dst_ref, val, rows, n_slab):
    for c in range(n_slab):
        dst_ref[pl.ds(c, rows, stride=n_slab), :] = val[:, c * LANES:(c + 1) * LANES].astype(dst_ref.dtype)


def _slab_chunk(src_ref, c, rows, n_slab, row0=0):
    return src_ref[pl.ds(row0 * n_slab + c, rows, stride=n_slab), :]


def _router_kernel(*refs, tr, n_slab, group_tiles):
    n_grp = len(group_tiles)
    x_refs = refs[:n_grp]
    g_ref, whi_ref, wlo_ref, b_ref, hn_ref, eid_ref, wt_ref, rank_ref, cnt_ref, run_ref = refs[n_grp:]
    t = pl.program_id(0)
    x = x_refs[-1][...]
    first_tile = sum(group_tiles[:-1])
    for gi in range(n_grp - 2, -1, -1):
        x = jnp.where(t < first_tile, x_refs[gi][...], x)
        first_tile -= group_tiles[gi]
    hn = x * lax.rsqrt(jnp.mean(x * x, axis=-1, keepdims=True) + EPS) * g_ref[...]
    _to_slabs(hn_ref, hn, tr, n_slab)
    h_hi = hn.astype(BF16)
    h_lo = (hn - h_hi.astype(F32)).astype(BF16)
    logits = (jnp.dot(h_hi, whi_ref[...], preferred_element_type=F32)
              + jnp.dot(h_lo, whi_ref[...], preferred_element_type=F32)
              + jnp.dot(h_hi, wlo_ref[...], preferred_element_type=F32)) + b_ref[...]
    lane = lax.broadcasted_iota(jnp.int32, logits.shape, 1)
    first = lambda hit: jnp.min(jnp.where(hit, lane, LANES), axis=1, keepdims=True)

    gl = jnp.where(lane < N_GROUPS, logits, -jnp.inf)
    gmax = jnp.max(gl, axis=1, keepdims=True)
    g_idx = first(gl == gmax)
    p_g = 1.0 / jnp.sum(jnp.exp(gl - gmax), axis=1, keepdims=True)

    lo = N_GROUPS + EXPERTS_PER_GROUP * g_idx
    in_grp = (lane >= lo) & (lane < lo + EXPERTS_PER_GROUP)
    el = jnp.where(in_grp, logits, -jnp.inf)
    pe = jnp.exp(el - jnp.max(el, axis=1, keepdims=True))
    probs = jnp.where(in_grp, pe / jnp.sum(pe, axis=1, keepdims=True), -1.0)
    v1 = jnp.max(probs, axis=1, keepdims=True)
    i1 = first(probs == v1)
    probs2 = jnp.where(lane == i1, -1.0, probs)
    v2 = jnp.max(probs2, axis=1, keepdims=True)
    i2 = first(probs2 == v2)
    tot = v1 + v2
    eid_ref[:, 0:1] = i1 - N_GROUPS
    eid_ref[:, 1:2] = i2 - N_GROUPS
    wt_ref[:, 0:1] = p_g * (v1 / tot)
    wt_ref[:, 1:2] = p_g * (v2 / tot)

    @pl.when(t == 0)
    def _():
        run_ref[...] = jnp.zeros(run_ref.shape, F32)

    sel1 = lane == i1
    sel2 = lane == i2
    oh1 = jnp.where(sel1, 1.0, 0.0).astype(BF16)
    oh2 = jnp.where(sel2, 1.0, 0.0).astype(BF16)
    earlier = (lax.broadcasted_iota(jnp.int32, (tr, tr), 1) < lax.broadcasted_iota(jnp.int32, (tr, tr), 0))
    tri = jnp.where(earlier, 1.0, 0.0).astype(BF16)
    run = run_ref[...]
    cnt1 = jnp.sum(oh1.astype(F32), axis=0, keepdims=True)
    cnt2 = jnp.sum(oh2.astype(F32), axis=0, keepdims=True)
    before1 = jnp.dot(tri, oh1, preferred_element_type=F32) + run
    before2 = jnp.dot(tri, oh2, preferred_element_type=F32) + (run + cnt1)
    rank_ref[:, 0:1] = jnp.sum(jnp.where(sel1, before1, 0.0), axis=1, keepdims=True).astype(jnp.int32)
    rank_ref[:, 1:2] = jnp.sum(jnp.where(sel2, before2, 0.0), axis=1, keepdims=True).astype(jnp.int32)
    run = run + cnt1 + cnt2
    run_ref[...] = run
    cnt_ref[...] = run.astype(jnp.int32)


def _router(xs, g, w_rg, b_rg, w_re, b_re):
    d = xs[0].shape[1]
    n = sum(x.shape[0] for x in xs)
    n_slab = d // LANES
    tr = 256
    assert all(x.shape[0] % tr == 0 for x in xs)
    group_tiles = tuple(x.shape[0] // tr for x in xs)
    x_specs = []
    for gi, nt in enumerate(group_tiles):
        t0 = sum(group_tiles[:gi])
        x_specs.append(pl.BlockSpec((tr, d), lambda i, t0=t0, nt=nt: (jnp.clip(i - t0, 0, nt - 1), 0)))
    w = jnp.zeros((d, LANES), F32).at[:, :N_GROUPS].set(w_rg).at[:, N_GROUPS:N_GROUPS + N_EXPERTS].set(w_re)
    b = jnp.zeros((1, LANES), F32).at[0, :N_GROUPS].set(b_rg).at[0, N_GROUPS:N_GROUPS + N_EXPERTS].set(b_re)
    w_hi = w.astype(BF16)
    w_lo = (w - w_hi.astype(F32)).astype(BF16)
    small = lambda dt: jax.ShapeDtypeStruct((n, TOP_K), dt)
    small_spec = pl.BlockSpec((tr, TOP_K), lambda i: (i, 0))
    return pl.pallas_call(
        functools.partial(_router_kernel, tr=tr, n_slab=n_slab, group_tiles=group_tiles),
        grid=(n // tr,),
        in_specs=x_specs + [
                  pl.BlockSpec((1, d), lambda i: (0, 0)),
                  pl.BlockSpec((d, LANES), lambda i: (0, 0)),
                  pl.BlockSpec((d, LANES), lambda i: (0, 0)),
                  pl.BlockSpec((1, LANES), lambda i: (0, 0))],
        out_specs=[pl.BlockSpec((tr * n_slab, LANES), lambda i: (i, 0)), small_spec, small_spec, small_spec,
                   pl.BlockSpec((1, LANES), lambda i: (0, 0))],
        out_shape=[jax.ShapeDtypeStruct((n * n_slab, LANES), SLAB_DTYPE), small(jnp.int32), small(F32), small(jnp.int32),
                   jax.ShapeDtypeStruct((1, LANES), jnp.int32)],
        scratch_shapes=[pltpu.VMEM((1, LANES), F32)],
        compiler_params=_params("arbitrary"),
        name="moe_router",
    )(*xs, g.reshape(1, d).astype(F32), w_hi, w_lo, b)


def _dispatch_kernel(pos_ref, zstart_ref, zvalid_ref, n_active_ref, hn_ref, xs_out, zbuf, sem, zsem,
                     *, td, tm, n_slab, n_tiles):
    t = pl.program_id(0)
    zrows = tm * n_slab

    def zero_tile(row0):
        return pltpu.make_async_copy(zbuf, xs_out.at[pl.ds(pl.multiple_of(row0 * n_slab, n_slab), zrows), :], zsem)

    @pl.when(t == 0)
    def _():
        zbuf[...] = jnp.zeros(zbuf.shape, zbuf.dtype)

        def for_each_zero_tile(act):
            for e in range(N_EXPERTS):
                @pl.when(zvalid_ref[e] > 0)
                def _():
                    act(zero_tile(zstart_ref[e]))

            def unused(i, carry):
                act(zero_tile(i * tm))
                return carry
            lax.fori_loop(n_active_ref[0], n_tiles, unused, 0)

        for_each_zero_tile(lambda copy: copy.start())
        for_each_zero_tile(lambda copy: copy.wait())

    def body(r, carry):
        src = hn_ref.at[pl.ds(pl.multiple_of(r * n_slab, n_slab), n_slab), :]
        for kk in range(TOP_K):
            dst_row = pl.multiple_of(pos_ref[(t * td + r) * TOP_K + kk] * n_slab, n_slab)
            pltpu.make_async_copy(src, xs_out.at[pl.ds(dst_row, n_slab), :], sem).start(priority=kk % 2)
        return carry
    lax.fori_loop(0, td, body, 0)
    for kk in range(TOP_K):
        pltpu.make_async_copy(hn_ref, xs_out.at[pl.ds(0, td * n_slab), :], sem).wait()


def _expert_kernel(tile_expert_ref, n_active_ref, x_ref, wg_ref, wu_ref, wd_ref, ys_ref, xb_ref, wgb_ref, wub_ref,
                   wdb_ref, *, tm, n_slab):
    t = pl.program_id(0)
    active = t < n_active_ref[0]

    @pl.when(jnp.logical_not(active))
    def _():
        ys_ref[...] = jnp.zeros(ys_ref.shape, ys_ref.dtype)

    @pl.when(active & ((t == 0) | (tile_expert_ref[t] != tile_expert_ref[jnp.maximum(t - 1, 0)])))
    def _():
        wgb_ref[...] = wg_ref[0].astype(BF16)
        wub_ref[...] = wu_ref[0].astype(BF16)
        wdb_ref[...] = wd_ref[0].astype(BF16)

    @pl.when(active)
    def _():
        for c in range(n_slab):
            xb_ref[:, c * LANES:(c + 1) * LANES] = _slab_chunk(x_ref, c, tm, n_slab).astype(BF16)
        x = xb_ref[...]
        g = jnp.dot(x, wgb_ref[...], preferred_element_type=F32)
        u = jnp.dot(x, wub_ref[...], preferred_element_type=F32)
        act = (g * jax.nn.sigmoid(g) * u).astype(BF16)
        _to_slabs(ys_ref, jnp.dot(act, wdb_ref[...], preferred_element_type=F32), tm, n_slab)


def _start_slab_gather(pos_ref, tok0, ys_hbm, ybuf, sem, slot, tc, n_slab):
    def body(r, carry):
        for kk in range(TOP_K):
            src_row = pl.multiple_of(pos_ref[(tok0 + r) * TOP_K + kk] * n_slab, n_slab)
            dst_row = pl.multiple_of((kk * tc + r) * n_slab, n_slab)
            pltpu.make_async_copy(ys_hbm.at[pl.ds(src_row, n_slab), :],
                                  ybuf.at[slot, pl.ds(dst_row, n_slab), :], sem.at[slot]).start(priority=kk % 2)
        return carry
    lax.fori_loop(0, tc, body, 0)


def _combine_kernel(pos_ref, x_ref, w_ref, g_ref, ys_hbm, o_ref, ybuf, sem, *, tc, tok_base, n_slab):
    t = pl.program_id(0)
    nt = pl.num_programs(0)
    slot = lax.rem(t, 2)

    @pl.when(t == 0)
    def _():
        _start_slab_gather(pos_ref, tok_base, ys_hbm, ybuf, sem, 0, tc, n_slab)

    @pl.when(t + 1 < nt)
    def _():
        _start_slab_gather(pos_ref, tok_base + (t + 1) * tc, ys_hbm, ybuf, sem, 1 - slot, tc, n_slab)

    pltpu.make_async_copy(ys_hbm.at[pl.ds(0, TOP_K * tc * n_slab), :], ybuf.at[slot], sem.at[slot]).wait()
    yb = ybuf.at[slot]
    wk = [jnp.broadcast_to(w_ref[:, kk:kk + 1], (tc, LANES)) for kk in range(TOP_K)]
    sq = jnp.zeros((tc, LANES), F32)
    for c in range(n_slab):
        moe = wk[0] * _slab_chunk(yb, c, tc, n_slab)
        for kk in range(1, TOP_K):
            moe = moe + wk[kk] * _slab_chunk(yb, c, tc, n_slab, row0=kk * tc)
        y = x_ref[:, c * LANES:(c + 1) * LANES] + moe
        o_ref[:, c * LANES:(c + 1) * LANES] = y
        sq = sq + y * y
    d = x_ref.shape[1]
    scale = lax.rsqrt(jnp.sum(sq, axis=1, keepdims=True) / d + EPS)
    o_ref[...] = o_ref[...] * scale * g_ref[...]


def _hier_moe_final(xs, norm_ffn, w_rg, b_rg, w_re, b_re, w_g, w_u, w_d, norm_final):
    d = xs[0].shape[1]
    n = sum(x.shape[0] for x in xs)
    n_slab = d // LANES
    d_exp = w_g.shape[2]
    hn, eid, wts, rank, cnt = _router(xs, norm_ffn, w_rg, b_rg, w_re, b_re)

    tm = 256
    na = n * TOP_K
    n_tiles = (na + N_EXPERTS * (tm - 1)) // tm + 1
    n_rows = n_tiles * tm
    counts = cnt[0, N_GROUPS:N_GROUPS + N_EXPERTS]
    padded = ((counts + tm - 1) // tm) * tm
    pend = jnp.cumsum(padded)
    pstart = pend - padded
    experts = jnp.arange(N_EXPERTS, dtype=jnp.int32)
    pos = (jnp.sum(jnp.where(eid[:, :, None] == experts, pstart, 0), axis=-1) + rank).reshape(na)
    tile_start = jnp.arange(n_tiles, dtype=jnp.int32) * tm
    tile_expert = jnp.minimum(jnp.sum((pend[None, :] <= tile_start[:, None]).astype(jnp.int32), axis=1), N_EXPERTS - 1)
    n_active = (pend[-1] // tm).astype(jnp.int32).reshape(1)
    zstart = jnp.maximum(pend - tm, 0).astype(jnp.int32)
    zvalid = (counts > 0).astype(jnp.int32)

    td = _tile(n, 256)
    x_sorted = pl.pallas_call(
        functools.partial(_dispatch_kernel, td=td, tm=tm, n_slab=n_slab, n_tiles=n_tiles),
        grid_spec=pltpu.PrefetchScalarGridSpec(
            num_scalar_prefetch=4,
            grid=(n // td,),
            in_specs=[pl.BlockSpec((td * n_slab, LANES), lambda t, p, zs, zv, na_: (t, 0))],
            out_specs=pl.BlockSpec(memory_space=pl.ANY),
            scratch_shapes=[pltpu.VMEM((tm * n_slab, LANES), SLAB_DTYPE), pltpu.SemaphoreType.DMA(()),
                            pltpu.SemaphoreType.DMA(())]),
        out_shape=jax.ShapeDtypeStruct((n_rows * n_slab, LANES), SLAB_DTYPE),
        compiler_params=_params("arbitrary"),
        name="moe_dispatch",
    )(pos, zstart, zvalid, n_active, hn)

    active = lambda t, te, na_: (jnp.minimum(t, na_[0] - 1), 0)
    ys = pl.pallas_call(
        functools.partial(_expert_kernel, tm=tm, n_slab=n_slab),
        grid_spec=pltpu.PrefetchScalarGridSpec(
            num_scalar_prefetch=2,
            grid=(n_tiles,),
            in_specs=[pl.BlockSpec((tm * n_slab, LANES), active),
                      pl.BlockSpec((1, d, d_exp), lambda t, te, na_: (te[t], 0, 0)),
                      pl.BlockSpec((1, d, d_exp), lambda t, te, na_: (te[t], 0, 0)),
                      pl.BlockSpec((1, d_exp, d), lambda t, te, na_: (te[t], 0, 0))],
            out_specs=pl.BlockSpec((tm * n_slab, LANES), lambda t, te, na_: (t, 0)),
            scratch_shapes=[pltpu.VMEM((tm, d), BF16), pltpu.VMEM((d, d_exp), BF16), pltpu.VMEM((d, d_exp), BF16),
                            pltpu.VMEM((d_exp, d), BF16)]),
        out_shape=jax.ShapeDtypeStruct((n_rows * n_slab, LANES), SLAB_DTYPE),
        compiler_params=_params("arbitrary"),
        name="moe_experts",
    )(tile_expert, n_active, x_sorted, w_g, w_u, w_d)

    outs = []
    row0 = 0
    for x in xs:
        rows = x.shape[0]
        tc = _tile(rows, 256)
        assert row0 % tc == 0
        blk0 = row0 // tc
        outs.append(pl.pallas_call(
            functools.partial(_combine_kernel, tc=tc, tok_base=row0, n_slab=n_slab),
            grid_spec=pltpu.PrefetchScalarGridSpec(
                num_scalar_prefetch=1,
                grid=(rows // tc,),
                in_specs=[pl.BlockSpec((tc, d), lambda t, p: (t, 0)),
                          pl.BlockSpec((tc, TOP_K), lambda t, p, blk0=blk0: (blk0 + t, 0)),
                          pl.BlockSpec((1, d), lambda t, p: (0, 0)),
                          pl.BlockSpec(memory_space=pl.ANY)],
                out_specs=pl.BlockSpec((tc, d), lambda t, p: (t, 0)),
                scratch_shapes=[pltpu.VMEM((2, TOP_K * tc * n_slab, LANES), SLAB_DTYPE),
                                pltpu.SemaphoreType.DMA((2,))]),
            out_shape=jax.ShapeDtypeStruct((rows, d), F32),
            compiler_params=_params("arbitrary"),
            name="moe_combine",
        )(pos, x, wts, norm_final.reshape(1, d).astype(F32), ys))
        row0 += rows
    return outs


def _mixer(groups, lam_init, w):
    d = groups[0][0].shape[2]
    d_conv = w["w_dw"].shape[1]
    d_attn = w["w_attn_out"].shape[0]
    d_mem = w["w_mem_out"].shape[0]
    c0 = 2 * d_conv
    c1, c2, c3 = c0 + d_attn, c0 + 2 * d_attn, c0 + 3 * d_attn
    c4 = c3 + d_mem
    x2ds = [x.reshape(-1, d) for x, *_ in groups]

    hs = [_rmsnorm(x2d, w["norm_mix"], BF16) for x2d in x2ds]
    w_in = w["w_in"]
    u_ins = _matmul(hs, w_in, F32, b_col0=0, n_out=c0)
    qs = _matmul(hs, w_in, F32, b_col0=c0, n_out=d_attn)
    k_rows = _matmul(hs, w_in, F32, b_col0=c1, n_out=d_attn)
    v_rows = _matmul(hs, w_in, F32, b_col0=c2, n_out=d_attn)
    mqs = _matmul(hs, w_in, BF16, b_col0=c3, n_out=d_mem)
    gates = _matmul(hs, w_in, BF16, b_col0=c4, n_out=N_BRANCH * d)

    lam_args = (w["lambda_q1"], w["lambda_k1"], w["lambda_q2"], w["lambda_k2"], w["subln_g"], lam_init)
    conv_acts, new_bufs, os_, mos = [], [], [], []
    for g, (x, conv_buf, k_past, v_past, mk, mv) in enumerate(groups):
        bsz, t_len, _ = x.shape
        conv_act, new_buf = _conv_branch(u_ins[g].reshape(bsz, t_len, c0), conv_buf, w["w_dw"], w["b_dw"],
                                         w["conv_ln_g"], w["conv_ln_b"])
        qkv = [a.reshape(bsz, t_len, d_attn) for a in (qs[g], k_rows[g], v_rows[g])]
        if k_past is None:
            o = _diff_attention(*qkv, *lam_args)
        else:
            o = _cached_diff_attention(*qkv, k_past, v_past, *lam_args)
        mo = _mem_attention(mqs[g].reshape(bsz, t_len, d_mem), mk, mv)
        conv_acts.append(conv_act.reshape(-1, d_conv))
        new_bufs.append(new_buf)
        os_.append(o.reshape(-1, d_attn))
        mos.append(mo.reshape(-1, d_mem))

    merged = _matmul(conv_acts, w["w_conv_out"], BF16, gates=gates, gate_col0=0)
    merged = _matmul(os_, w["w_attn_out"], BF16, gates=gates, gate_col0=d, adds=merged)
    merged = _matmul(mos, w["w_mem_out"], BF16, gates=gates, gate_col0=2 * d, adds=merged)
    x2s = _matmul(merged, w["w_out"], F32, adds=x2ds)
    hd2 = 2 * HEAD_DIM
    rows4d = lambda r, x: r.reshape(x.shape[0], x.shape[1], N_HEADS, hd2)
    return [(x2s[g], new_bufs[g], rows4d(k_rows[g], x), rows4d(v_rows[g], x)) for g, (x, *_) in enumerate(groups)]


def kernel(x_prompt, x_sample, mem_prompt, cache_conv, cache_diff_k, cache_diff_v, cache_mem_k, cache_mem_v, norm_mix, w_in, w_dw, b_dw, conv_ln_g, conv_ln_b, w_conv_out, lambda_q1, lambda_k1, lambda_q2, lambda_k2, subln_g, w_attn_out, norm_mem, w_mem_k, w_mem_v, w_mem_out, w_out, norm_ffn, w_router_grp, b_router_grp, w_router_exp, b_router_exp, w_exp_gate, w_exp_up, w_exp_down, norm_final):
    depth = w_in.shape[0]
    assert depth == 1, "the final rmsnorm is fused into the last layer's MoE combine"
    bp, tp, d = x_prompt.shape
    bs, ts, _ = x_sample.shape
    n_mem = mem_prompt.shape[1]
    d_conv = w_dw.shape[2]
    d_mem = w_mem_k.shape[2]
    l = 0
    lam_init = 0.8 - 0.6 * math.exp(-0.3 * l)
    w = dict(norm_mix=norm_mix[l], w_in=w_in[l], w_dw=w_dw[l], b_dw=b_dw[l], conv_ln_g=conv_ln_g[l],
             conv_ln_b=conv_ln_b[l], w_conv_out=w_conv_out[l], lambda_q1=lambda_q1[l],
             lambda_k1=lambda_k1[l], lambda_q2=lambda_q2[l], lambda_k2=lambda_k2[l], subln_g=subln_g[l],
             w_attn_out=w_attn_out[l], w_mem_out=w_mem_out[l], w_out=w_out[l])

    hm = _rmsnorm(mem_prompt.reshape(bp * n_mem, d), norm_mem[l], BF16)
    mk = _matmul([hm], w_mem_k[l], F32)[0].reshape(bp, n_mem, d_mem)
    mv = _matmul([hm], w_mem_v[l], F32)[0].reshape(bp, n_mem, d_mem)
    conv0 = jnp.zeros((bp, CONV_W - 1, d_conv), F32)
    prompt = (x_prompt, conv0, None, None, mk, mv)
    sample = (x_sample, cache_conv.reshape(cache_conv.shape[1:]), cache_diff_k.reshape(cache_diff_k.shape[1:]),
              cache_diff_v.reshape(cache_diff_v.shape[1:]), cache_mem_k.reshape(bs, n_mem, d_mem),
              cache_mem_v.reshape(bs, n_mem, d_mem))
    (x2_p, cb_p, k_p, v_p), (x2_s, cb_s, k_s, v_s) = _mixer([prompt, sample], lam_init, w)

    y_p, y_s = _hier_moe_final([x2_p, x2_s], norm_ffn[l], w_router_grp[l], b_router_grp[l], w_router_exp[l],
                               b_router_exp[l], w_exp_gate[l], w_exp_up[l], w_exp_down[l], norm_final)
    mem_shape = (1, bp, n_mem, MEM_HEADS, MEM_HEAD_DIM)
    return (y_p.reshape(bp, tp, d), y_s.reshape(bs, ts, d), cb_p[None], k_p[None], v_p[None],
            mk.reshape(mem_shape), mv.reshape(mem_shape), cb_s[None], k_s[None], v_s[None])
```

```python
import functools
import math

import jax
import jax.numpy as jnp
from jax import lax
from jax.experimental import pallas as pl
from jax.experimental.pallas import tpu as pltpu

F32 = jnp.float32
BF16 = jnp.bfloat16

CHUNK = 64
EPS = 1e-6
SUBLN_EPS = 1e-5
NEG = -1e30
CONV_W = 31
N_HEADS = 16
HEAD_DIM = 64
MEM_HEADS = 4
MEM_HEAD_DIM = 256
N_BRANCH = 3
N_GROUPS = 4
EXPERTS_PER_GROUP = 8
N_EXPERTS = N_GROUPS * EXPERTS_PER_GROUP
TOP_K = 2

LANES = 128
SUBLANES = 8
VMEM_LIMIT_BYTES = 56 * 1024 * 1024
HALO = 32


def _params(*semantics):
    return pltpu.CompilerParams(dimension_semantics=semantics, vmem_limit_bytes=VMEM_LIMIT_BYTES)


def _tile(n, pref):
    if n <= pref:
        return n
    t = pref
    while n % t:
        t //= 2
    return t


def _rmsnorm_kernel(x_ref, g_ref, o_ref, *, eps):
    x = x_ref[...].astype(F32)
    ms = jnp.mean(x * x, axis=-1, keepdims=True)
    o_ref[...] = (x * lax.rsqrt(ms + eps) * g_ref[...]).astype(o_ref.dtype)


def _rmsnorm(x, g, out_dtype, eps=EPS):
    n, d = x.shape
    tr = _tile(n, 512)
    return pl.pallas_call(
        functools.partial(_rmsnorm_kernel, eps=eps),
        grid=(n // tr,),
        in_specs=[pl.BlockSpec((tr, d), lambda i: (i, 0)), pl.BlockSpec((1, d), lambda i: (0, 0))],
        out_specs=pl.BlockSpec((tr, d), lambda i: (i, 0)),
        out_shape=jax.ShapeDtypeStruct((n, d), out_dtype),
        compiler_params=_params("parallel"),
        name="rmsnorm",
    )(x, g.reshape(1, d).astype(F32))


def _mm_kernel(*refs, n_grp, tile_ranges, has_gate, has_add):
    a_refs, b_ref = refs[:n_grp], refs[n_grp]
    pos = n_grp + 1
    gate_refs = refs[pos:pos + n_grp] if has_gate else None
    pos += n_grp * has_gate
    add_refs = refs[pos:pos + n_grp] if has_add else None
    o_refs, b_bf16 = refs[-1 - n_grp:-1], refs[-1]
    i = pl.program_id(1)

    @pl.when(i == 0)
    def _():
        b_bf16[...] = b_ref[...].astype(BF16)

    for g, (t0, t1) in enumerate(tile_ranges):
        @pl.when((i >= t0) & (i < t1))
        def _():
            acc = jnp.dot(a_refs[g][...], b_bf16[...], preferred_element_type=F32)
            if has_gate:
                acc = acc * jax.nn.sigmoid(gate_refs[g][...].astype(F32))
            if has_add:
                acc = acc + add_refs[g][...].astype(F32)
            o_refs[g][...] = acc.astype(o_refs[g].dtype)


def _matmul(a_groups, b, out_dtype, *, b_col0=0, n_out=None, gates=None, gate_col0=0, adds=None):
    k = a_groups[0].shape[1]
    assert all(a.dtype == BF16 for a in a_groups) and b.dtype == F32
    n_out = b.shape[1] if n_out is None else n_out
    tn = _tile(n_out, 1024)
    assert b_col0 % tn == 0 and gate_col0 % tn == 0
    bj0, gj0 = b_col0 // tn, gate_col0 // tn
    tms = [_tile(a.shape[0], 1024) for a in a_groups]
    tiles = [a.shape[0] // tm for a, tm in zip(a_groups, tms)]
    starts = [sum(tiles[:g]) for g in range(len(tiles))]
    tile_ranges = tuple((t0, t0 + nt) for t0, nt in zip(starts, tiles))

    def rows(g, col0=None):
        t0, nt = starts[g], tiles[g]
        if col0 is None:
            return lambda j, i: (jnp.clip(i - t0, 0, nt - 1), 0)
        return lambda j, i: (jnp.clip(i - t0, 0, nt - 1), col0 + j)

    n_grp = len(a_groups)
    in_specs = [pl.BlockSpec((tms[g], k), rows(g)) for g in range(n_grp)]
    size = lambda dt: jnp.dtype(dt).itemsize
    per_row = k * size(BF16) + tn * (size(out_dtype) + (size(gates[0].dtype) if gates is not None else 0)
                                     + (size(adds[0].dtype) if adds is not None else 0))
    estimate = 2 * sum(tms) * per_row + k * tn * (2 * size(F32) + size(BF16)) + max(tms) * tn * size(F32)
    weight_buffers = 2 if estimate <= VMEM_LIMIT_BYTES else 1
    in_specs.append(pl.BlockSpec((k, tn), lambda j, i: (0, bj0 + j), pipeline_mode=pl.Buffered(weight_buffers)))
    args = list(a_groups) + [b]
    if gates is not None:
        in_specs += [pl.BlockSpec((tms[g], tn), rows(g, gj0)) for g in range(n_grp)]
        args += list(gates)
    if adds is not None:
        in_specs += [pl.BlockSpec((tms[g], tn), rows(g, 0)) for g in range(n_grp)]
        args += list(adds)
    return pl.pallas_call(
        functools.partial(_mm_kernel, n_grp=n_grp, tile_ranges=tile_ranges, has_gate=gates is not None,
                          has_add=adds is not None),
        grid=(n_out // tn, sum(tiles)),
        in_specs=in_specs,
        out_specs=[pl.BlockSpec((tms[g], tn), rows(g, 0)) for g in range(n_grp)],
        out_shape=[jax.ShapeDtypeStruct((a.shape[0], n_out), out_dtype) for a in a_groups],
        scratch_shapes=[pltpu.VMEM((k, tn), BF16)],
        compiler_params=_params("parallel", "arbitrary"),
        name="matmul",
    )(*args)


def _conv_kernel(u_ref, buf_ref, w_ref, bdw_ref, g_ref, b_ref, act_ref, newbuf_ref, ext_ref, shift_ref, *, tt, dc):
    t = pl.program_id(1)
    hist = CONV_W - 1

    @pl.when(t == 0)
    def _():
        ext_ref[pl.ds(HALO - hist, hist), :] = buf_ref[0].astype(F32)

    @pl.when(t > 0)
    def _():
        ext_ref[pl.ds(0, HALO), :] = ext_ref[pl.ds(tt, HALO), :]

    a = u_ref[0, :, :dc].astype(F32)
    gt = u_ref[0, :, dc:].astype(F32)
    ext_ref[pl.ds(HALO, tt), :] = a * jax.nn.sigmoid(gt)

    n_ext = HALO + tt
    ext = ext_ref[...]
    for b in range(1, SUBLANES):
        shift_ref[b - 1] = pltpu.roll(ext, n_ext - b, axis=0)
    y = jnp.zeros((tt, dc), F32) + bdw_ref[...]
    for j in range(CONV_W):
        a, b = divmod(HALO - hist + j, SUBLANES)
        src = ext_ref if b == 0 else shift_ref.at[b - 1]
        y = y + src[pl.ds(a * SUBLANES, tt), :] * w_ref[pl.ds(j, 1), :]
    mu = jnp.mean(y, axis=-1, keepdims=True)
    var = jnp.mean(jnp.square(y - mu), axis=-1, keepdims=True)
    z = (y - mu) * lax.rsqrt(var + EPS) * g_ref[...] + b_ref[...]
    act_ref[0] = (z * jax.nn.sigmoid(z)).astype(act_ref.dtype)
    newbuf_ref[0] = ext_ref[pl.ds(HALO + tt - hist, hist), :]


def _conv_branch(u_in, conv_buf, w_dw, b_dw, ln_g, ln_b):
    bsz, t_len, two_dc = u_in.shape
    dc = two_dc // 2
    tt = _tile(t_len, 256)
    assert tt >= HALO and tt % SUBLANES == 0
    vec = lambda v: v.reshape(1, dc).astype(F32)
    return pl.pallas_call(
        functools.partial(_conv_kernel, tt=tt, dc=dc),
        grid=(bsz, t_len // tt),
        in_specs=[pl.BlockSpec((1, tt, two_dc), lambda b, t: (b, t, 0)),
                  pl.BlockSpec((1, CONV_W - 1, dc), lambda b, t: (b, 0, 0)),
                  pl.BlockSpec((CONV_W, dc), lambda b, t: (0, 0)),
                  pl.BlockSpec((1, dc), lambda b, t: (0, 0)),
                  pl.BlockSpec((1, dc), lambda b, t: (0, 0)),
                  pl.BlockSpec((1, dc), lambda b, t: (0, 0))],
        out_specs=[pl.BlockSpec((1, tt, dc), lambda b, t: (b, t, 0)),
                   pl.BlockSpec((1, CONV_W - 1, dc), lambda b, t: (b, 0, 0))],
        out_shape=[jax.ShapeDtypeStruct((bsz, t_len, dc), BF16),
                   jax.ShapeDtypeStruct((bsz, CONV_W - 1, dc), F32)],
        scratch_shapes=[pltpu.VMEM((HALO + tt, dc), F32), pltpu.VMEM((SUBLANES - 1, HALO + tt, dc), F32)],
        compiler_params=_params("parallel", "arbitrary"),
        name="conv_branch",
    )(u_in, conv_buf, w_dw.astype(F32), vec(b_dw), vec(ln_g), vec(ln_b))


ONES_ROWS = 16
ATTN_TQ = 1024
ATTN_LANE_GROUP = 512
ATTN_TK_CACHED = 512
ATTN_HEAD_GROUP = 4
LOG2E = 1.4426950408889634


def _stacked_qt(q):
    tq = q.shape[0]
    qt = (q * (HEAD_DIM ** -0.5 * LOG2E)).T
    z = jnp.zeros((HEAD_DIM, tq), F32)
    return jnp.concatenate([jnp.concatenate([qt[:HEAD_DIM], z], axis=0),
                            jnp.concatenate([z, qt[HEAD_DIM:]], axis=0)], axis=1).astype(BF16)


def _vt_ones(v):
    vt = v.astype(F32).T.astype(BF16)
    return jnp.concatenate([vt, jnp.ones((ONES_ROWS, v.shape[0]), BF16)], axis=0)


def _scores(k_blk, qqt):
    return jnp.dot(k_blk, qqt, preferred_element_type=F32)


def _softmax_step(m_prev, acc_prev, s, vt_blk):
    m_new = jnp.maximum(m_prev, jnp.max(s, axis=0, keepdims=True))
    alpha = jnp.exp2(m_prev - m_new)
    p = jnp.exp2(s - m_new).astype(BF16)
    return m_new, alpha * acc_prev + jnp.dot(vt_blk, p, preferred_element_type=F32)


def _mask_aligned_block(s, lane0, tq):
    q_chunk = lax.rem(lax.broadcasted_iota(jnp.int32, (1, s.shape[1]), 1) + lane0, tq) // CHUNK
    bands = [jnp.where(q_chunk >= c, s[c * CHUNK:(c + 1) * CHUNK, :], NEG) for c in range(s.shape[0] // CHUNK)]
    return jnp.concatenate(bands, axis=0)


def _diff_finalize(acc, lam, g_col, tq, lam_init):
    hd2 = 2 * HEAD_DIM
    inv_l = 1.0 / acc[hd2:hd2 + 1, :]
    ot = acc[:hd2, :tq] * inv_l[:, :tq] - lam * (acc[:hd2, tq:] * inv_l[:, tq:])
    ot = ot * lax.rsqrt(jnp.mean(ot * ot, axis=0, keepdims=True) + SUBLN_EPS) * g_col
    return (ot * (1.0 - lam_init)).T


def _lambda(lq1_ref, lk1_ref, lq2_ref, lk2_ref, lam_init):
    return (jnp.exp(jnp.sum(lq1_ref[...] * lk1_ref[...], axis=1, keepdims=True))
            - jnp.exp(jnp.sum(lq2_ref[...] * lk2_ref[...], axis=1, keepdims=True)) + lam_init)


def _diff_attn_kernel(q_ref, k_ref, v_ref, lq1_ref, lk1_ref, lq2_ref, lk2_ref, g_ref, o_ref, k_scr, vt_scr, qq_scr,
                      acc_ref, s_ref, *, n_blk, tq, lam_init):
    g = pl.program_id(2)

    @pl.when(g == 0)
    def _():
        def body(i, carry):
            off = pl.multiple_of(i * tq, tq)
            k_scr[i] = k_ref[0, pl.ds(off, tq), :].astype(BF16)
            vt_scr[i] = _vt_ones(v_ref[0, pl.ds(off, tq), :])
            return carry
        lax.fori_loop(0, n_blk, body, 0)

    for r in range(2):
        qq_scr[r] = _stacked_qt(q_ref[0, r * tq:(r + 1) * tq, :].astype(F32))
    acc_ref[...] = jnp.zeros(acc_ref.shape, F32)
    lg = min(ATTN_LANE_GROUP, tq)
    n_lg = 2 * tq // lg
    group = lambda c: slice(c * lg, (c + 1) * lg)

    def keys_needed(c, masked):
        return (c % (tq // lg) + 1) * lg if masked else tq

    def issue(kb, r, c, masked=False):
        nk = keys_needed(c, masked)
        s_ref[:nk, group(c)] = _scores(k_scr[kb, :nk], qq_scr[r, :, group(c)])

    def block(kb, r, m_prev, masked, then):
        m_new = []
        for c in range(n_lg):
            if c + 1 < n_lg:
                issue(kb, r, c + 1, masked)
            elif then is not None:
                issue(then[0], then[1], 0)
            nk = keys_needed(c, masked)
            s = s_ref[:nk, group(c)]
            if masked:
                s = _mask_aligned_block(s, c * lg, tq)
            m_c, acc = _softmax_step(m_prev[c], acc_ref[r, :, group(c)], s, vt_scr[kb, :, :nk])
            acc_ref[r, :, group(c)] = acc
            m_new.append(m_c)
        return tuple(m_new)

    def query_block(r, n_full_pairs, odd_full, then):
        def two_blocks(i2, m_prev):
            m_mid = block(2 * i2, r, m_prev, False, (2 * i2 + 1, r))
            return block(2 * i2 + 1, r, m_mid, False, (2 * i2 + 2, r))
        m = lax.fori_loop(0, n_full_pairs, two_blocks, tuple(jnp.full((1, lg), NEG, F32) for _ in range(n_lg)))
        if odd_full:
            m = block(2 * n_full_pairs, r, m, False, (2 * n_full_pairs + 1, r))
        block(2 * n_full_pairs + odd_full, r, m, True, then)
        lam = _lambda(lq1_ref, lk1_ref, lq2_ref, lk2_ref, lam_init)
        o_ref[0, r * tq:(r + 1) * tq, :] = _diff_finalize(acc_ref[r], lam, g_ref[...], tq, lam_init).astype(o_ref.dtype)

    issue(0, 0, 0)
    query_block(0, g, 0, (0, 1))
    query_block(1, g, 1, None)


def _cached_attn_kernel(q_ref, kc_ref, vc_ref, kp_ref, vp_ref, lq1_ref, lk1_ref, lq2_ref, lk2_ref, g_ref, o_ref,
                        qq_scr, m_scr, acc_scr, *, n_heads, tq, tkp, lam_init):
    j = pl.program_id(1)
    hd2 = 2 * HEAD_DIM
    head = lambda h: slice(h * hd2, (h + 1) * hd2)

    @pl.when(j == 0)
    def _():
        for h in range(n_heads):
            qq_scr[h] = _stacked_qt(q_ref[0, :, head(h)].astype(F32))
        m_scr[...] = jnp.full(m_scr.shape, NEG, F32)
        acc_scr[...] = jnp.zeros(acc_scr.shape, F32)

    def update(heads, keys, values, masked):
        ss = [_scores(keys(h).astype(BF16), qq_scr[h]) for h in heads]
        if masked:
            ss = [_mask_aligned_block(s, 0, tq) for s in ss]
        vts = [_vt_ones(values(h)) for h in heads]
        res = [_softmax_step(m_scr[h], acc_scr[h], s, vt) for h, s, vt in zip(heads, ss, vts)]
        for h, (m_new, acc) in zip(heads, res):
            m_scr[h] = m_new
            acc_scr[h] = acc

    head_groups = [range(h0, min(h0 + ATTN_HEAD_GROUP, n_heads)) for h0 in range(0, n_heads, ATTN_HEAD_GROUP)]
    for heads in head_groups:
        update(heads, lambda h: kp_ref[0, pl.ds(h, tkp, stride=n_heads), :],
               lambda h: vp_ref[0, pl.ds(h, tkp, stride=n_heads), :], False)

    @pl.when(j == pl.num_programs(1) - 1)
    def _():
        lam = _lambda(lq1_ref, lk1_ref, lq2_ref, lk2_ref, lam_init)
        for heads in head_groups:
            update(heads, lambda h: kc_ref[0, :, head(h)], lambda h: vc_ref[0, :, head(h)], True)
            for h in heads:
                o_ref[0, :, head(h)] = _diff_finalize(acc_scr[h], lam, g_ref[...], tq, lam_init).astype(o_ref.dtype)


def _lambda_args(lq1, lk1, lq2, lk2, subln_g):
    vec = lambda v: v.reshape(1, -1).astype(F32)
    return [vec(lq1), vec(lk1), vec(lq2), vec(lk2), subln_g.reshape(-1, 1).astype(F32)]


def _diff_attention(q, k_cur, v_cur, lq1, lk1, lq2, lk2, subln_g, lam_init):
    bsz, t_len, d_attn = q.shape
    hd2 = 2 * HEAD_DIM
    tq = _tile(t_len // 2, ATTN_TQ)
    n_blk = t_len // tq
    assert tq % CHUNK == 0 and n_blk % 2 == 0 and tq % min(ATTN_LANE_GROUP, tq) == 0
    const = lambda shape: pl.BlockSpec(shape, lambda b, h, i: (0, 0))
    return pl.pallas_call(
        functools.partial(_diff_attn_kernel, n_blk=n_blk, tq=tq, lam_init=lam_init),
        grid=(bsz, d_attn // hd2, n_blk // 2),
        in_specs=[pl.BlockSpec((1, 2 * tq, hd2), lambda b, h, i: (b, i, h)),
                  pl.BlockSpec((1, t_len, hd2), lambda b, h, i: (b, 0, h)),
                  pl.BlockSpec((1, t_len, hd2), lambda b, h, i: (b, 0, h))]
        + [const((1, HEAD_DIM))] * 4 + [const((hd2, 1))],
        out_specs=pl.BlockSpec((1, 2 * tq, hd2), lambda b, h, i: (b, i, h)),
        out_shape=jax.ShapeDtypeStruct((bsz, t_len, d_attn), BF16),
        scratch_shapes=[pltpu.VMEM((n_blk, tq, hd2), BF16), pltpu.VMEM((n_blk, hd2 + ONES_ROWS, tq), BF16),
                        pltpu.VMEM((2, hd2, 2 * tq), BF16), pltpu.VMEM((2, hd2 + ONES_ROWS, 2 * tq), F32),
                        pltpu.VMEM((tq, 2 * tq), F32)],
        compiler_params=_params("parallel", "parallel", "arbitrary"),
        name="diff_attention",
    )(q, k_cur, v_cur, *_lambda_args(lq1, lk1, lq2, lk2, subln_g))


def _cached_diff_attention(q, k_cur, v_cur, k_past, v_past, lq1, lk1, lq2, lk2, subln_g, lam_init):
    bsz, t_len, d_attn = q.shape
    hd2 = 2 * HEAD_DIM
    past_len, n_heads = k_past.shape[1], k_past.shape[2]
    tq = t_len
    tkp = _tile(past_len, ATTN_TK_CACHED)
    assert tq % CHUNK == 0 and past_len % CHUNK == 0 and tq <= ATTN_TQ and n_heads * hd2 == d_attn
    assert n_heads % SUBLANES == 0
    flat = lambda c: c.reshape(bsz, past_len * n_heads, hd2)
    const = lambda shape: pl.BlockSpec(shape, lambda b, j: (0, 0))
    rows = pl.BlockSpec((1, t_len, d_attn), lambda b, j: (b, 0, 0))
    hist = pl.BlockSpec((1, tkp * n_heads, hd2), lambda b, j: (b, j, 0))
    return pl.pallas_call(
        functools.partial(_cached_attn_kernel, n_heads=n_heads, tq=tq, tkp=tkp, lam_init=lam_init),
        grid=(bsz, past_len // tkp),
        in_specs=[rows, rows, rows, hist, hist] + [const((1, HEAD_DIM))] * 4 + [const((hd2, 1))],
        out_specs=rows,
        out_shape=jax.ShapeDtypeStruct((bsz, t_len, d_attn), BF16),
        scratch_shapes=[pltpu.VMEM((n_heads, hd2, 2 * tq), BF16), pltpu.VMEM((n_heads, 1, 2 * tq), F32),
                        pltpu.VMEM((n_heads, hd2 + ONES_ROWS, 2 * tq), F32)],
        compiler_params=_params("parallel", "arbitrary"),
        name="cached_diff_attention",
    )(q, k_cur, v_cur, flat(k_past), flat(v_past), *_lambda_args(lq1, lk1, lq2, lk2, subln_g))


def _mem_attn_kernel(q_ref, k_ref, v_ref, o_ref):
    scale = jnp.asarray(MEM_HEAD_DIM ** -0.5, q_ref.dtype)
    for h in range(MEM_HEADS):
        sl = slice(h * MEM_HEAD_DIM, (h + 1) * MEM_HEAD_DIM)
        q = q_ref[0, :, sl] * scale
        s = lax.dot_general(q, k_ref[0, :, sl].astype(BF16), (((1,), (1,)), ((), ())), preferred_element_type=F32)
        p = jnp.exp(s - jnp.max(s, axis=1, keepdims=True))
        p = p / jnp.sum(p, axis=1, keepdims=True)
        o = jnp.dot(p.astype(BF16), v_ref[0, :, sl].astype(BF16), preferred_element_type=F32)
        o_ref[0, :, sl] = o.astype(o_ref.dtype)


def _mem_attention(mq, mk, mv):
    bsz, t_len, d_mem = mq.shape
    n_mem = mk.shape[1]
    tt = _tile(t_len, 512)
    return pl.pallas_call(
        _mem_attn_kernel,
        grid=(bsz, t_len // tt),
        in_specs=[pl.BlockSpec((1, tt, d_mem), lambda b, t: (b, t, 0)),
                  pl.BlockSpec((1, n_mem, d_mem), lambda b, t: (b, 0, 0)),
                  pl.BlockSpec((1, n_mem, d_mem), lambda b, t: (b, 0, 0))],
        out_specs=pl.BlockSpec((1, tt, d_mem), lambda b, t: (b, t, 0)),
        out_shape=jax.ShapeDtypeStruct((bsz, t_len, d_mem), BF16),
        compiler_params=_params("parallel", "parallel"),
        name="mem_attention",
    )(mq, mk, mv)


SLAB_DTYPE = F32
ROW_COPY_UNROLL = 4


def _to_slabs(dst_ref, val, rows, n_slab):
    for c in range(n_slab):
        dst_ref[pl.ds(c, rows, stride=n_slab), :] = val[:, c * LANES:(c + 1) * LANES].astype(dst_ref.dtype)


def _slab_chunk(src_ref, c, rows, n_slab, row0=0):
    return src_ref[pl.ds(row0 * n_slab + c, rows, stride=n_slab), :]


def _router_kernel(*refs, tr, n_slab, group_tiles):
    n_grp = len(group_tiles)
    x_refs = refs[:n_grp]
    g_ref, whi_ref, wlo_ref, b_ref, hn_ref, eid_ref, wt_ref, rank_ref, cnt_ref, run_ref = refs[n_grp:]
    t = pl.program_id(0)
    x = x_refs[-1][...]
    first_tile = sum(group_tiles[:-1])
    for gi in range(n_grp - 2, -1, -1):
        x = jnp.where(t < first_tile, x_refs[gi][...], x)
        first_tile -= group_tiles[gi]
    hn = x * lax.rsqrt(jnp.mean(x * x, axis=-1, keepdims=True) + EPS) * g_ref[...]
    _to_slabs(hn_ref, hn, tr, n_slab)
    h_hi = hn.astype(BF16)
    h_lo = (hn - h_hi.astype(F32)).astype(BF16)
    logits = (jnp.dot(h_hi, whi_ref[...], preferred_element_type=F32)
              + jnp.dot(h_lo, whi_ref[...], preferred_element_type=F32)
              + jnp.dot(h_hi, wlo_ref[...], preferred_element_type=F32)) + b_ref[...]
    lane = lax.broadcasted_iota(jnp.int32, logits.shape, 1)
    first = lambda hit: jnp.min(jnp.where(hit, lane, LANES), axis=1, keepdims=True)

    gl = jnp.where(lane < N_GROUPS, logits, -jnp.inf)
    gmax = jnp.max(gl, axis=1, keepdims=True)
    g_idx = first(gl == gmax)
    p_g = 1.0 / jnp.sum(jnp.exp(gl - gmax), axis=1, keepdims=True)

    lo = N_GROUPS + EXPERTS_PER_GROUP * g_idx
    in_grp = (lane >= lo) & (lane < lo + EXPERTS_PER_GROUP)
    el = jnp.where(in_grp, logits, -jnp.inf)
    pe = jnp.exp(el - jnp.max(el, axis=1, keepdims=True))
    probs = jnp.where(in_grp, pe / jnp.sum(pe, axis=1, keepdims=True), -1.0)
    v1 = jnp.max(probs, axis=1, keepdims=True)
    i1 = first(probs == v1)
    probs2 = jnp.where(lane == i1, -1.0, probs)
    v2 = jnp.max(probs2, axis=1, keepdims=True)
    i2 = first(probs2 == v2)
    tot = v1 + v2
    eid_ref[:, 0:1] = i1 - N_GROUPS
    eid_ref[:, 1:2] = i2 - N_GROUPS
    wt_ref[:, 0:1] = p_g * (v1 / tot)
    wt_ref[:, 1:2] = p_g * (v2 / tot)

    @pl.when(t == 0)
    def _():
        run_ref[...] = jnp.zeros(run_ref.shape, F32)

    sel1 = lane == i1
    sel2 = lane == i2
    oh1 = jnp.where(sel1, 1.0, 0.0).astype(BF16)
    oh2 = jnp.where(sel2, 1.0, 0.0).astype(BF16)
    earlier = (lax.broadcasted_iota(jnp.int32, (tr, tr), 1) < lax.broadcasted_iota(jnp.int32, (tr, tr), 0))
    tri = jnp.where(earlier, 1.0, 0.0).astype(BF16)
    run = run_ref[...]
    cnt1 = jnp.sum(oh1.astype(F32), axis=0, keepdims=True)
    cnt2 = jnp.sum(oh2.astype(F32), axis=0, keepdims=True)
    before1 = jnp.dot(tri, oh1, preferred_element_type=F32) + run
    before2 = jnp.dot(tri, oh2, preferred_element_type=F32) + (run + cnt1)
    rank_ref[:, 0:1] = jnp.sum(jnp.where(sel1, before1, 0.0), axis=1, keepdims=True).astype(jnp.int32)
    rank_ref[:, 1:2] = jnp.sum(jnp.where(sel2, before2, 0.0), axis=1, keepdims=True).astype(jnp.int32)
    run = run + cnt1 + cnt2
    run_ref[...] = run
    cnt_ref[...] = run.astype(jnp.int32)


def _router(xs, g, w_rg, b_rg, w_re, b_re):
    d = xs[0].shape[1]
    n = sum(x.shape[0] for x in xs)
    n_slab = d // LANES
    tr = 256
    assert all(x.shape[0] % tr == 0 for x in xs)
    group_tiles = tuple(x.shape[0] // tr for x in xs)
    x_specs = []
    for gi, nt in enumerate(group_tiles):
        t0 = sum(group_tiles[:gi])
        x_specs.append(pl.BlockSpec((tr, d), lambda i, t0=t0, nt=nt: (jnp.clip(i - t0, 0, nt - 1), 0)))
    w = jnp.zeros((d, LANES), F32).at[:, :N_GROUPS].set(w_rg).at[:, N_GROUPS:N_GROUPS + N_EXPERTS].set(w_re)
    b = jnp.zeros((1, LANES), F32).at[0, :N_GROUPS].set(b_rg).at[0, N_GROUPS:N_GROUPS + N_EXPERTS].set(b_re)
    w_hi = w.astype(BF16)
    w_lo = (w - w_hi.astype(F32)).astype(BF16)
    small = lambda dt: jax.ShapeDtypeStruct((n, TOP_K), dt)
    small_spec = pl.BlockSpec((tr, TOP_K), lambda i: (i, 0))
    return pl.pallas_call(
        functools.partial(_router_kernel, tr=tr, n_slab=n_slab, group_tiles=group_tiles),
        grid=(n // tr,),
        in_specs=x_specs + [
                  pl.BlockSpec((1, d), lambda i: (0, 0)),
                  pl.BlockSpec((d, LANES), lambda i: (0, 0)),
                  pl.BlockSpec((d, LANES), lambda i: (0, 0)),
                  pl.BlockSpec((1, LANES), lambda i: (0, 0))],
        out_specs=[pl.BlockSpec((tr * n_slab, LANES), lambda i: (i, 0)), small_spec, small_spec, small_spec,
                   pl.BlockSpec((1, LANES), lambda i: (0, 0))],
        out_shape=[jax.ShapeDtypeStruct((n * n_slab, LANES), SLAB_DTYPE), small(jnp.int32), small(F32), small(jnp.int32),
                   jax.ShapeDtypeStruct((1, LANES), jnp.int32)],
        scratch_shapes=[pltpu.VMEM((1, LANES), F32)],
        compiler_params=_params("arbitrary"),
        name="moe_router",
    )(*xs, g.reshape(1, d).astype(F32), w_hi, w_lo, b)


def _dispatch_kernel(pos_ref, zstart_ref, zvalid_ref, n_active_ref, hn_ref, xs_out, zbuf, sem, zsem,
                     *, td, tm, n_slab, n_tiles):
    t = pl.program_id(0)
    zrows = tm * n_slab

    def zero_tile(row0):
        return pltpu.make_async_copy(zbuf, xs_out.at[pl.ds(pl.multiple_of(row0 * n_slab, n_slab), zrows), :], zsem)

    @pl.when(t == 0)
    def _():
        zbuf[...] = jnp.zeros(zbuf.shape, zbuf.dtype)

        def for_each_zero_tile(act):
            for e in range(N_EXPERTS):
                @pl.when(zvalid_ref[e] > 0)
                def _():
                    act(zero_tile(zstart_ref[e]))

            def unused(i, carry):
                act(zero_tile(i * tm))
                return carry
            lax.fori_loop(n_active_ref[0], n_tiles, unused, 0)

        for_each_zero_tile(lambda copy: copy.start())
        for_each_zero_tile(lambda copy: copy.wait())

    def body(r, carry):
        src = hn_ref.at[pl.ds(pl.multiple_of(r * n_slab, n_slab), n_slab), :]
        for kk in range(TOP_K):
            dst_row = pl.multiple_of(pos_ref[(t * td + r) * TOP_K + kk] * n_slab, n_slab)
            pltpu.make_async_copy(src, xs_out.at[pl.ds(dst_row, n_slab), :], sem).start(priority=kk % 2)
        return carry
    lax.fori_loop(0, td, body, 0, unroll=ROW_COPY_UNROLL)
    for kk in range(TOP_K):
        pltpu.make_async_copy(hn_ref, xs_out.at[pl.ds(0, td * n_slab), :], sem).wait()


def _expert_kernel(tile_expert_ref, n_active_ref, x_ref, wg_ref, wu_ref, wd_ref, ys_ref, xb_ref, wgb_ref, wub_ref,
                   wdb_ref, *, tm, n_slab):
    t = pl.program_id(0)
    active = t < n_active_ref[0]

    @pl.when(jnp.logical_not(active))
    def _():
        ys_ref[...] = jnp.zeros(ys_ref.shape, ys_ref.dtype)

    @pl.when(active & ((t == 0) | (tile_expert_ref[t] != tile_expert_ref[jnp.maximum(t - 1, 0)])))
    def _():
        wgb_ref[...] = wg_ref[0].astype(BF16)
        wub_ref[...] = wu_ref[0].astype(BF16)
        wdb_ref[...] = wd_ref[0].astype(BF16)

    @pl.when(active)
    def _():
        for c in range(n_slab):
            xb_ref[:, c * LANES:(c + 1) * LANES] = _slab_chunk(x_ref, c, tm, n_slab).astype(BF16)
        x = xb_ref[...]
        g = jnp.dot(x, wgb_ref[...], preferred_element_type=F32)
        u = jnp.dot(x, wub_ref[...], preferred_element_type=F32)
        act = (g * jax.nn.sigmoid(g) * u).astype(BF16)
        _to_slabs(ys_ref, jnp.dot(act, wdb_ref[...], preferred_element_type=F32), tm, n_slab)


def _start_slab_gather(pos_ref, tok0, ys_hbm, ybuf, sem, slot, tc, n_slab):
    def body(r, carry):
        for kk in range(TOP_K):
            src_row = pl.multiple_of(pos_ref[(tok0 + r) * TOP_K + kk] * n_slab, n_slab)
            dst_row = pl.multiple_of((kk * tc + r) * n_slab, n_slab)
            pltpu.make_async_copy(ys_hbm.at[pl.ds(src_row, n_slab), :],
                                  ybuf.at[slot, pl.ds(dst_row, n_slab), :], sem.at[slot]).start(priority=kk % 2)
        return carry
    lax.fori_loop(0, tc, body, 0, unroll=ROW_COPY_UNROLL)


def _combine_kernel(pos_ref, x_ref, w_ref, g_ref, ys_hbm, o_ref, ybuf, sem, *, tc, tok_base, n_slab):
    t = pl.program_id(0)
    nt = pl.num_programs(0)
    slot = lax.rem(t, 2)

    @pl.when(t == 0)
    def _():
        _start_slab_gather(pos_ref, tok_base, ys_hbm, ybuf, sem, 0, tc, n_slab)

    @pl.when(t + 1 < nt)
    def _():
        _start_slab_gather(pos_ref, tok_base + (t + 1) * tc, ys_hbm, ybuf, sem, 1 - slot, tc, n_slab)

    pltpu.make_async_copy(ys_hbm.at[pl.ds(0, TOP_K * tc * n_slab), :], ybuf.at[slot], sem.at[slot]).wait()
    yb = ybuf.at[slot]
    wk = [jnp.broadcast_to(w_ref[:, kk:kk + 1], (tc, LANES)) for kk in range(TOP_K)]
    sq = jnp.zeros((tc, LANES), F32)
    for c in range(n_slab):
        moe = wk[0] * _slab_chunk(yb, c, tc, n_slab)
        for kk in range(1, TOP_K):
            moe = moe + wk[kk] * _slab_chunk(yb, c, tc, n_slab, row0=kk * tc)
        y = x_ref[:, c * LANES:(c + 1) * LANES] + moe
        o_ref[:, c * LANES:(c + 1) * LANES] = y
        sq = sq + y * y
    d = x_ref.shape[1]
    scale = lax.rsqrt(jnp.sum(sq, axis=1, keepdims=True) / d + EPS)
    o_ref[...] = o_ref[...] * scale * g_ref[...]


def _hier_moe_final(xs, norm_ffn, w_rg, b_rg, w_re, b_re, w_g, w_u, w_d, norm_final):
    d = xs[0].shape[1]
    n = sum(x.shape[0] for x in xs)
    n_slab = d // LANES
    d_exp = w_g.shape[2]
    hn, eid, wts, rank, cnt = _router(xs, norm_ffn, w_rg, b_rg, w_re, b_re)

    tm = 256
    na = n * TOP_K
    n_tiles = (na + N_EXPERTS * (tm - 1)) // tm + 1
    n_rows = n_tiles * tm
    counts = cnt[0, N_GROUPS:N_GROUPS + N_EXPERTS]
    padded = ((counts + tm - 1) // tm) * tm
    pend = jnp.cumsum(padded)
    pstart = pend - padded
    experts = jnp.arange(N_EXPERTS, dtype=jnp.int32)
    pos = (jnp.sum(jnp.where(eid[:, :, None] == experts, pstart, 0), axis=-1) + rank).reshape(na)
    tile_start = jnp.arange(n_tiles, dtype=jnp.int32) * tm
    tile_expert = jnp.minimum(jnp.sum((pend[None, :] <= tile_start[:, None]).astype(jnp.int32), axis=1), N_EXPERTS - 1)
    n_active = (pend[-1] // tm).astype(jnp.int32).reshape(1)
    zstart = jnp.maximum(pend - tm, 0).astype(jnp.int32)
    zvalid = (counts > 0).astype(jnp.int32)

    td = _tile(n, 256)
    x_sorted = pl.pallas_call(
        functools.partial(_dispatch_kernel, td=td, tm=tm, n_slab=n_slab, n_tiles=n_tiles),
        grid_spec=pltpu.PrefetchScalarGridSpec(
            num_scalar_prefetch=4,
            grid=(n // td,),
            in_specs=[pl.BlockSpec((td * n_slab, LANES), lambda t, p, zs, zv, na_: (t, 0))],
            out_specs=pl.BlockSpec(memory_space=pl.ANY),
            scratch_shapes=[pltpu.VMEM((tm * n_slab, LANES), SLAB_DTYPE), pltpu.SemaphoreType.DMA(()),
                            pltpu.SemaphoreType.DMA(())]),
        out_shape=jax.ShapeDtypeStruct((n_rows * n_slab, LANES), SLAB_DTYPE),
        compiler_params=_params("arbitrary"),
        name="moe_dispatch",
    )(pos, zstart, zvalid, n_active, hn)

    active = lambda t, te, na_: (jnp.minimum(t, na_[0] - 1), 0)
    ys = pl.pallas_call(
        functools.partial(_expert_kernel, tm=tm, n_slab=n_slab),
        grid_spec=pltpu.PrefetchScalarGridSpec(
            num_scalar_prefetch=2,
            grid=(n_tiles,),
            in_specs=[pl.BlockSpec((tm * n_slab, LANES), active),
                      pl.BlockSpec((1, d, d_exp), lambda t, te, na_: (te[t], 0, 0)),
                      pl.BlockSpec((1, d, d_exp), lambda t, te, na_: (te[t], 0, 0)),
                      pl.BlockSpec((1, d_exp, d), lambda t, te, na_: (te[t], 0, 0))],
            out_specs=pl.BlockSpec((tm * n_slab, LANES), lambda t, te, na_: (t, 0)),
            scratch_shapes=[pltpu.VMEM((tm, d), BF16), pltpu.VMEM((d, d_exp), BF16), pltpu.VMEM((d, d_exp), BF16),
                            pltpu.VMEM((d_exp, d), BF16)]),
        out_shape=jax.ShapeDtypeStruct((n_rows * n_slab, LANES), SLAB_DTYPE),
        compiler_params=_params("arbitrary"),
        name="moe_experts",
    )(tile_expert, n_active, x_sorted, w_g, w_u, w_d)

    outs = []
    row0 = 0
    for x in xs:
        rows = x.shape[0]
        tc = _tile(rows, 256)
        assert row0 % tc == 0
        blk0 = row0 // tc
        outs.append(pl.pallas_call(
            functools.partial(_combine_kernel, tc=tc, tok_base=row0, n_slab=n_slab),
            grid_spec=pltpu.PrefetchScalarGridSpec(
                num_scalar_prefetch=1,
                grid=(rows // tc,),
                in_specs=[pl.BlockSpec((tc, d), lambda t, p: (t, 0)),
                          pl.BlockSpec((tc, TOP_K), lambda t, p, blk0=blk0: (blk0 + t, 0)),
                          pl.BlockSpec((1, d), lambda t, p: (0, 0)),
                          pl.BlockSpec(memory_space=pl.ANY)],
                out_specs=pl.BlockSpec((tc, d), lambda t, p: (t, 0)),
                scratch_shapes=[pltpu.VMEM((2, TOP_K * tc * n_slab, LANES), SLAB_DTYPE),
                                pltpu.SemaphoreType.DMA((2,))]),
            out_shape=jax.ShapeDtypeStruct((rows, d), F32),
            compiler_params=_params("arbitrary"),
            name="moe_combine",
        )(pos, x, wts, norm_final.reshape(1, d).astype(F32), ys))
        row0 += rows
    return outs


def _mixer(groups, lam_init, w):
    d = groups[0][0].shape[2]
    d_conv = w["w_dw"].shape[1]
    d_attn = w["w_attn_out"].shape[0]
    d_mem = w["w_mem_out"].shape[0]
    c0 = 2 * d_conv
    c1, c2, c3 = c0 + d_attn, c0 + 2 * d_attn, c0 + 3 * d_attn
    c4 = c3 + d_mem
    x2ds = [x.reshape(-1, d) for x, *_ in groups]

    hs = [_rmsnorm(x2d, w["norm_mix"], BF16) for x2d in x2ds]
    w_in = w["w_in"]
    u_ins = _matmul(hs, w_in, F32, b_col0=0, n_out=c0)
    qs = _matmul(hs, w_in, F32, b_col0=c0, n_out=d_attn)
    k_rows = _matmul(hs, w_in, F32, b_col0=c1, n_out=d_attn)
    v_rows = _matmul(hs, w_in, F32, b_col0=c2, n_out=d_attn)
    mqs = _matmul(hs, w_in, BF16, b_col0=c3, n_out=d_mem)
    gates = _matmul(hs, w_in, BF16, b_col0=c4, n_out=N_BRANCH * d)

    lam_args = (w["lambda_q1"], w["lambda_k1"], w["lambda_q2"], w["lambda_k2"], w["subln_g"], lam_init)
    conv_acts, new_bufs, os_, mos = [], [], [], []
    for g, (x, conv_buf, k_past, v_past, mk, mv) in enumerate(groups):
        bsz, t_len, _ = x.shape
        conv_act, new_buf = _conv_branch(u_ins[g].reshape(bsz, t_len, c0), conv_buf, w["w_dw"], w["b_dw"],
                                         w["conv_ln_g"], w["conv_ln_b"])
        qkv = [a.reshape(bsz, t_len, d_attn) for a in (qs[g], k_rows[g], v_rows[g])]
        if k_past is None:
            o = _diff_attention(*qkv, *lam_args)
        else:
            o = _cached_diff_attention(*qkv, k_past, v_past, *lam_args)
        mo = _mem_attention(mqs[g].reshape(bsz, t_len, d_mem), mk, mv)
        conv_acts.append(conv_act.reshape(-1, d_conv))
        new_bufs.append(new_buf)
        os_.append(o.reshape(-1, d_attn))
        mos.append(mo.reshape(-1, d_mem))

    merged = _matmul(conv_acts, w["w_conv_out"], BF16, gates=gates, gate_col0=0)
    merged = _matmul(os_, w["w_attn_out"], BF16, gates=gates, gate_col0=d, adds=merged)
    merged = _matmul(mos, w["w_mem_out"], BF16, gates=gates, gate_col0=2 * d, adds=merged)
    x2s = _matmul(merged, w["w_out"], F32, adds=x2ds)
    hd2 = 2 * HEAD_DIM
    rows4d = lambda r, x: r.reshape(x.shape[0], x.shape[1], N_HEADS, hd2)
    return [(x2s[g], new_bufs[g], rows4d(k_rows[g], x), rows4d(v_rows[g], x)) for g, (x, *_) in enumerate(groups)]


def kernel(x_prompt, x_sample, mem_prompt, cache_conv, cache_diff_k, cache_diff_v, cache_mem_k, cache_mem_v, norm_mix, w_in, w_dw, b_dw, conv_ln_g, conv_ln_b, w_conv_out, lambda_q1, lambda_k1, lambda_q2, lambda_k2, subln_g, w_attn_out, norm_mem, w_mem_k, w_mem_v, w_mem_out, w_out, norm_ffn, w_router_grp, b_router_grp, w_router_exp, b_router_exp, w_exp_gate, w_exp_up, w_exp_down, norm_final):
    depth = w_in.shape[0]
    assert depth == 1, "the final rmsnorm is fused into the last layer's MoE combine"
    bp, tp, d = x_prompt.shape
    bs, ts, _ = x_sample.shape
    n_mem = mem_prompt.shape[1]
    d_conv = w_dw.shape[2]
    d_mem = w_mem_k.shape[2]
    l = 0
    lam_init = 0.8 - 0.6 * math.exp(-0.3 * l)
    w = dict(norm_mix=norm_mix[l], w_in=w_in[l], w_dw=w_dw[l], b_dw=b_dw[l], conv_ln_g=conv_ln_g[l],
             conv_ln_b=conv_ln_b[l], w_conv_out=w_conv_out[l], lambda_q1=lambda_q1[l],
             lambda_k1=lambda_k1[l], lambda_q2=lambda_q2[l], lambda_k2=lambda_k2[l], subln_g=subln_g[l],
             w_attn_out=w_attn_out[l], w_mem_out=w_mem_out[l], w_out=w_out[l])

    hm = _rmsnorm(mem_prompt.reshape(bp * n_mem, d), norm_mem[l], BF16)
    mk = _matmul([hm], w_mem_k[l], F32)[0].reshape(bp, n_mem, d_mem)
    mv = _matmul([hm], w_mem_v[l], F32)[0].reshape(bp, n_mem, d_mem)
    conv0 = jnp.zeros((bp, CONV_W - 1, d_conv), F32)
    prompt = (x_prompt, conv0, None, None, mk, mv)
    sample = (x_sample, cache_conv.reshape(cache_conv.shape[1:]), cache_diff_k.reshape(cache_diff_k.shape[1:]),
              cache_diff_v.reshape(cache_diff_v.shape[1:]), cache_mem_k.reshape(bs, n_mem, d_mem),
              cache_mem_v.reshape(bs, n_mem, d_mem))
    (x2_p, cb_p, k_p, v_p), (x2_s, cb_s, k_s, v_s) = _mixer([prompt, sample], lam_init, w)

    y_p, y_s = _hier_moe_final([x2_p, x2_s], norm_ffn[l], w_router_grp[l], b_router_grp[l], w_router_exp[l],
                               b_router_exp[l], w_exp_gate[l], w_exp_up[l], w_exp_down[l], norm_final)
    mem_shape = (1, bp, n_mem, MEM_HEADS, MEM_HEAD_DIM)
    return (y_p.reshape(bp, tp, d), y_s.reshape(bs, ts, d), cb_p[None], k_p[None], v_p[None],
            mk.reshape(mem_shape), mv.reshape(mem_shape), cb_s[None], k_s[None], v_s[None])
```

```python
import functools
import math

import jax
import jax.numpy as jnp
from jax import lax
from jax.experimental import pallas as pl
from jax.experimental.pallas import tpu as pltpu

F32 = jnp.float32
BF16 = jnp.bfloat16

CHUNK = 64
EPS = 1e-6
SUBLN_EPS = 1e-5
NEG = -1e30
CONV_W = 31
N_HEADS = 16
HEAD_DIM = 64
MEM_HEADS = 4
MEM_HEAD_DIM = 256
N_BRANCH = 3
N_GROUPS = 4
EXPERTS_PER_GROUP = 8
N_EXPERTS = N_GROUPS * EXPERTS_PER_GROUP
TOP_K = 2

LANES = 128
SUBLANES = 8
VMEM_LIMIT_BYTES = 56 * 1024 * 1024
HALO = 32


def _params(*semantics):
    return pltpu.CompilerParams(dimension_semantics=semantics, vmem_limit_bytes=VMEM_LIMIT_BYTES)


def _tile(n, pref):
    if n <= pref:
        return n
    t = pref
    while n % t:
        t //= 2
    return t


def _rmsnorm_kernel(x_ref, g_ref, o_ref, *, eps):
    x = x_ref[...].astype(F32)
    ms = jnp.mean(x * x, axis=-1, keepdims=True)
    o_ref[...] = (x * lax.rsqrt(ms + eps) * g_ref[...]).astype(o_ref.dtype)


def _rmsnorm(x, g, out_dtype, eps=EPS):
    n, d = x.shape
    tr = _tile(n, 512)
    return pl.pallas_call(
        functools.partial(_rmsnorm_kernel, eps=eps),
        grid=(n // tr,),
        in_specs=[pl.BlockSpec((tr, d), lambda i: (i, 0)), pl.BlockSpec((1, d), lambda i: (0, 0))],
        out_specs=pl.BlockSpec((tr, d), lambda i: (i, 0)),
        out_shape=jax.ShapeDtypeStruct((n, d), out_dtype),
        compiler_params=_params("parallel"),
        name="rmsnorm",
    )(x, g.reshape(1, d).astype(F32))


def _mm_kernel(*refs, n_grp, tile_ranges, has_gate, has_add):
    a_refs, b_ref = refs[:n_grp], refs[n_grp]
    pos = n_grp + 1
    gate_refs = refs[pos:pos + n_grp] if has_gate else None
    pos += n_grp * has_gate
    add_refs = refs[pos:pos + n_grp] if has_add else None
    o_refs, b_bf16 = refs[-1 - n_grp:-1], refs[-1]
    i = pl.program_id(1)

    @pl.when(i == 0)
    def _():
        b_bf16[...] = b_ref[...].astype(BF16)

    for g, (t0, t1) in enumerate(tile_ranges):
        @pl.when((i >= t0) & (i < t1))
        def _():
            acc = jnp.dot(a_refs[g][...], b_bf16[...], preferred_element_type=F32)
            if has_gate:
                acc = acc * jax.nn.sigmoid(gate_refs[g][...].astype(F32))
            if has_add:
                acc = acc + add_refs[g][...].astype(F32)
            o_refs[g][...] = acc.astype(o_refs[g].dtype)


def _matmul(a_groups, b, out_dtype, *, b_col0=0, n_out=None, gates=None, gate_col0=0, adds=None):
    k = a_groups[0].shape[1]
    assert all(a.dtype == BF16 for a in a_groups) and b.dtype == F32
    n_out = b.shape[1] if n_out is None else n_out
    tn = _tile(n_out, 1024)
    assert b_col0 % tn == 0 and gate_col0 % tn == 0
    bj0, gj0 = b_col0 // tn, gate_col0 // tn
    tms = [_tile(a.shape[0], 1024) for a in a_groups]
    tiles = [a.shape[0] // tm for a, tm in zip(a_groups, tms)]
    starts = [sum(tiles[:g]) for g in range(len(tiles))]
    tile_ranges = tuple((t0, t0 + nt) for t0, nt in zip(starts, tiles))

    def rows(g, col0=None):
        t0, nt = starts[g], tiles[g]
        if col0 is None:
            return lambda j, i: (jnp.clip(i - t0, 0, nt - 1), 0)
        return lambda j, i: (jnp.clip(i - t0, 0, nt - 1), col0 + j)

    n_grp = len(a_groups)
    in_specs = [pl.BlockSpec((tms[g], k), rows(g)) for g in range(n_grp)]
    size = lambda dt: jnp.dtype(dt).itemsize
    per_row = k * size(BF16) + tn * (size(out_dtype) + (size(gates[0].dtype) if gates is not None else 0)
                                     + (size(adds[0].dtype) if adds is not None else 0))
    estimate = 2 * sum(tms) * per_row + k * tn * (2 * size(F32) + size(BF16)) + max(tms) * tn * size(F32)
    weight_buffers = 2 if estimate <= VMEM_LIMIT_BYTES else 1
    in_specs.append(pl.BlockSpec((k, tn), lambda j, i: (0, bj0 + j), pipeline_mode=pl.Buffered(weight_buffers)))
    args = list(a_groups) + [b]
    if gates is not None:
        in_specs += [pl.BlockSpec((tms[g], tn), rows(g, gj0)) for g in range(n_grp)]
        args += list(gates)
    if adds is not None:
        in_specs += [pl.BlockSpec((tms[g], tn), rows(g, 0)) for g in range(n_grp)]
        args += list(adds)
    return pl.pallas_call(
        functools.partial(_mm_kernel, n_grp=n_grp, tile_ranges=tile_ranges, has_gate=gates is not None,
                          has_add=adds is not None),
        grid=(n_out // tn, sum(tiles)),
        in_specs=in_specs,
        out_specs=[pl.BlockSpec((tms[g], tn), rows(g, 0)) for g in range(n_grp)],
        out_shape=[jax.ShapeDtypeStruct((a.shape[0], n_out), out_dtype) for a in a_groups],
        scratch_shapes=[pltpu.VMEM((k, tn), BF16)],
        compiler_params=_params("parallel", "arbitrary"),
        name="matmul",
    )(*args)


def _conv_kernel(u_ref, buf_ref, w_ref, bdw_ref, g_ref, b_ref, act_ref, newbuf_ref, ext_ref, shift_ref, *, tt, dc):
    t = pl.program_id(1)
    hist = CONV_W - 1

    @pl.when(t == 0)
    def _():
        ext_ref[pl.ds(HALO - hist, hist), :] = buf_ref[0].astype(F32)

    @pl.when(t > 0)
    def _():
        ext_ref[pl.ds(0, HALO), :] = ext_ref[pl.ds(tt, HALO), :]

    a = u_ref[0, :, :dc].astype(F32)
    gt = u_ref[0, :, dc:].astype(F32)
    ext_ref[pl.ds(HALO, tt), :] = a * jax.nn.sigmoid(gt)

    n_ext = HALO + tt
    ext = ext_ref[...]
    for b in range(1, SUBLANES):
        shift_ref[b - 1] = pltpu.roll(ext, n_ext - b, axis=0)
    y = jnp.zeros((tt, dc), F32) + bdw_ref[...]
    for j in range(CONV_W):
        a, b = divmod(HALO - hist + j, SUBLANES)
        src = ext_ref if b == 0 else shift_ref.at[b - 1]
        y = y + src[pl.ds(a * SUBLANES, tt), :] * w_ref[pl.ds(j, 1), :]
    mu = jnp.mean(y, axis=-1, keepdims=True)
    var = jnp.mean(jnp.square(y - mu), axis=-1, keepdims=True)
    z = (y - mu) * lax.rsqrt(var + EPS) * g_ref[...] + b_ref[...]
    act_ref[0] = (z * jax.nn.sigmoid(z)).astype(act_ref.dtype)
    newbuf_ref[0] = ext_ref[pl.ds(HALO + tt - hist, hist), :]


def _conv_branch(u_in, conv_buf, w_dw, b_dw, ln_g, ln_b):
    bsz, t_len, two_dc = u_in.shape
    dc = two_dc // 2
    tt = _tile(t_len, 256)
    assert tt >= HALO and tt % SUBLANES == 0
    vec = lambda v: v.reshape(1, dc).astype(F32)
    return pl.pallas_call(
        functools.partial(_conv_kernel, tt=tt, dc=dc),
        grid=(bsz, t_len // tt),
        in_specs=[pl.BlockSpec((1, tt, two_dc), lambda b, t: (b, t, 0)),
                  pl.BlockSpec((1, CONV_W - 1, dc), lambda b, t: (b, 0, 0)),
                  pl.BlockSpec((CONV_W, dc), lambda b, t: (0, 0)),
                  pl.BlockSpec((1, dc), lambda b, t: (0, 0)),
                  pl.BlockSpec((1, dc), lambda b, t: (0, 0)),
                  pl.BlockSpec((1, dc), lambda b, t: (0, 0))],
        out_specs=[pl.BlockSpec((1, tt, dc), lambda b, t: (b, t, 0)),
                   pl.BlockSpec((1, CONV_W - 1, dc), lambda b, t: (b, 0, 0))],
        out_shape=[jax.ShapeDtypeStruct((bsz, t_len, dc), BF16),
                   jax.ShapeDtypeStruct((bsz, CONV_W - 1, dc), F32)],
        scratch_shapes=[pltpu.VMEM((HALO + tt, dc), F32), pltpu.VMEM((SUBLANES - 1, HALO + tt, dc), F32)],
        compiler_params=_params("parallel", "arbitrary"),
        name="conv_branch",
    )(u_in, conv_buf, w_dw.astype(F32), vec(b_dw), vec(ln_g), vec(ln_b))


ONES_ROWS = 16
ATTN_TQ = 1024
ATTN_LANE_GROUP = 512
ATTN_TK_CACHED = 512
ATTN_HEAD_GROUP = 4
LOG2E = 1.4426950408889634


def _stacked_qt(q):
    tq = q.shape[0]
    qt = (q * (HEAD_DIM ** -0.5 * LOG2E)).T
    z = jnp.zeros((HEAD_DIM, tq), F32)
    return jnp.concatenate([jnp.concatenate([qt[:HEAD_DIM], z], axis=0),
                            jnp.concatenate([z, qt[HEAD_DIM:]], axis=0)], axis=1).astype(BF16)


def _vt_ones(v):
    vt = v.astype(F32).T.astype(BF16)
    return jnp.concatenate([vt, jnp.ones((ONES_ROWS, v.shape[0]), BF16)], axis=0)


def _scores(k_blk, qqt):
    return jnp.dot(k_blk, qqt, preferred_element_type=F32)


def _softmax_step(m_prev, acc_prev, s, vt_blk):
    m_new = jnp.maximum(m_prev, jnp.max(s, axis=0, keepdims=True))
    alpha = jnp.exp2(m_prev - m_new)
    p = jnp.exp2(s - m_new).astype(BF16)
    return m_new, alpha * acc_prev + jnp.dot(vt_blk, p, preferred_element_type=F32)


def _mask_aligned_block(s, lane0, tq):
    q_chunk = lax.rem(lax.broadcasted_iota(jnp.int32, (1, s.shape[1]), 1) + lane0, tq) // CHUNK
    bands = [jnp.where(q_chunk >= c, s[c * CHUNK:(c + 1) * CHUNK, :], NEG) for c in range(s.shape[0] // CHUNK)]
    return jnp.concatenate(bands, axis=0)


def _diff_finalize(acc, lam, g_col, tq, lam_init):
    hd2 = 2 * HEAD_DIM
    inv_l = 1.0 / acc[hd2:hd2 + 1, :]
    ot = acc[:hd2, :tq] * inv_l[:, :tq] - lam * (acc[:hd2, tq:] * inv_l[:, tq:])
    ot = ot * lax.rsqrt(jnp.mean(ot * ot, axis=0, keepdims=True) + SUBLN_EPS) * g_col
    return (ot * (1.0 - lam_init)).T


def _lambda(lq1_ref, lk1_ref, lq2_ref, lk2_ref, lam_init):
    return (jnp.exp(jnp.sum(lq1_ref[...] * lk1_ref[...], axis=1, keepdims=True))
            - jnp.exp(jnp.sum(lq2_ref[...] * lk2_ref[...], axis=1, keepdims=True)) + lam_init)


def _diff_attn_kernel(q_ref, k_ref, v_ref, lq1_ref, lk1_ref, lq2_ref, lk2_ref, g_ref, o_ref, k_scr, vt_scr, qq_scr,
                      acc_ref, s_ref, *, n_blk, tq, lam_init):
    g = pl.program_id(2)

    @pl.when(g == 0)
    def _():
        def body(i, carry):
            off = pl.multiple_of(i * tq, tq)
            k_scr[i] = k_ref[0, pl.ds(off, tq), :].astype(BF16)
            vt_scr[i] = _vt_ones(v_ref[0, pl.ds(off, tq), :])
            return carry
        lax.fori_loop(0, n_blk, body, 0)

    for r in range(2):
        qq_scr[r] = _stacked_qt(q_ref[0, r * tq:(r + 1) * tq, :].astype(F32))
    acc_ref[...] = jnp.zeros(acc_ref.shape, F32)
    lg = min(ATTN_LANE_GROUP, tq)
    n_lg = 2 * tq // lg
    group = lambda c: slice(c * lg, (c + 1) * lg)

    def keys_needed(c, masked):
        return (c % (tq // lg) + 1) * lg if masked else tq

    def issue(kb, r, c, masked=False):
        nk = keys_needed(c, masked)
        s_ref[:nk, group(c)] = _scores(k_scr[kb, :nk], qq_scr[r, :, group(c)])

    def block(kb, r, m_prev, masked, then):
        m_new = []
        for c in range(n_lg):
            if c + 1 < n_lg:
                issue(kb, r, c + 1, masked)
            elif then is not None:
                issue(then[0], then[1], 0)
            nk = keys_needed(c, masked)
            s = s_ref[:nk, group(c)]
            if masked:
                s = _mask_aligned_block(s, c * lg, tq)
            m_c, acc = _softmax_step(m_prev[c], acc_ref[r, :, group(c)], s, vt_scr[kb, :, :nk])
            acc_ref[r, :, group(c)] = acc
            m_new.append(m_c)
        return tuple(m_new)

    def query_block(r, n_full_pairs, odd_full, then):
        def two_blocks(i2, m_prev):
            m_mid = block(2 * i2, r, m_prev, False, (2 * i2 + 1, r))
            return block(2 * i2 + 1, r, m_mid, False, (2 * i2 + 2, r))
        m = lax.fori_loop(0, n_full_pairs, two_blocks, tuple(jnp.full((1, lg), NEG, F32) for _ in range(n_lg)))
        if odd_full:
            m = block(2 * n_full_pairs, r, m, False, (2 * n_full_pairs + 1, r))
        block(2 * n_full_pairs + odd_full, r, m, True, then)
        lam = _lambda(lq1_ref, lk1_ref, lq2_ref, lk2_ref, lam_init)
        o_ref[0, r * tq:(r + 1) * tq, :] = _diff_finalize(acc_ref[r], lam, g_ref[...], tq, lam_init).astype(o_ref.dtype)

    issue(0, 0, 0)
    query_block(0, g, 0, (0, 1))
    query_block(1, g, 1, None)


def _cached_attn_kernel(q_ref, kc_ref, vc_ref, kp_ref, vp_ref, lq1_ref, lk1_ref, lq2_ref, lk2_ref, g_ref, o_ref,
                        qq_scr, m_scr, acc_scr, *, n_heads, tq, tkp, lam_init):
    j = pl.program_id(1)
    hd2 = 2 * HEAD_DIM
    head = lambda h: slice(h * hd2, (h + 1) * hd2)

    @pl.when(j == 0)
    def _():
        for h in range(n_heads):
            qq_scr[h] = _stacked_qt(q_ref[0, :, head(h)].astype(F32))
        m_scr[...] = jnp.full(m_scr.shape, NEG, F32)
        acc_scr[...] = jnp.zeros(acc_scr.shape, F32)

    def update(heads, keys, values, masked):
        ss = [_scores(keys(h).astype(BF16), qq_scr[h]) for h in heads]
        if masked:
            ss = [_mask_aligned_block(s, 0, tq) for s in ss]
        vts = [_vt_ones(values(h)) for h in heads]
        res = [_softmax_step(m_scr[h], acc_scr[h], s, vt) for h, s, vt in zip(heads, ss, vts)]
        for h, (m_new, acc) in zip(heads, res):
            m_scr[h] = m_new
            acc_scr[h] = acc

    head_groups = [range(h0, min(h0 + ATTN_HEAD_GROUP, n_heads)) for h0 in range(0, n_heads, ATTN_HEAD_GROUP)]
    for heads in head_groups:
        update(heads, lambda h: kp_ref[0, pl.ds(h, tkp, stride=n_heads), :],
               lambda h: vp_ref[0, pl.ds(h, tkp, stride=n_heads), :], False)

    @pl.when(j == pl.num_programs(1) - 1)
    def _():
        lam = _lambda(lq1_ref, lk1_ref, lq2_ref, lk2_ref, lam_init)
        for heads in head_groups:
            update(heads, lambda h: kc_ref[0, :, head(h)], lambda h: vc_ref[0, :, head(h)], True)
            for h in heads:
                o_ref[0, :, head(h)] = _diff_finalize(acc_scr[h], lam, g_ref[...], tq, lam_init).astype(o_ref.dtype)


def _lambda_args(lq1, lk1, lq2, lk2, subln_g):
    vec = lambda v: v.reshape(1, -1).astype(F32)
    return [vec(lq1), vec(lk1), vec(lq2), vec(lk2), subln_g.reshape(-1, 1).astype(F32)]


def _diff_attention(q, k_cur, v_cur, lq1, lk1, lq2, lk2, subln_g, lam_init):
    bsz, t_len, d_attn = q.shape
    hd2 = 2 * HEAD_DIM
    tq = _tile(t_len // 2, ATTN_TQ)
    n_blk = t_len // tq
    assert tq % CHUNK == 0 and n_blk % 2 == 0 and tq % min(ATTN_LANE_GROUP, tq) == 0
    const = lambda shape: pl.BlockSpec(shape, lambda b, h, i: (0, 0))
    return pl.pallas_call(
        functools.partial(_diff_attn_kernel, n_blk=n_blk, tq=tq, lam_init=lam_init),
        grid=(bsz, d_attn // hd2, n_blk // 2),
        in_specs=[pl.BlockSpec((1, 2 * tq, hd2), lambda b, h, i: (b, i, h)),
                  pl.BlockSpec((1, t_len, hd2), lambda b, h, i: (b, 0, h)),
                  pl.BlockSpec((1, t_len, hd2), lambda b, h, i: (b, 0, h))]
        + [const((1, HEAD_DIM))] * 4 + [const((hd2, 1))],
        out_specs=pl.BlockSpec((1, 2 * tq, hd2), lambda b, h, i: (b, i, h)),
        out_shape=jax.ShapeDtypeStruct((bsz, t_len, d_attn), BF16),
        scratch_shapes=[pltpu.VMEM((n_blk, tq, hd2), BF16), pltpu.VMEM((n_blk, hd2 + ONES_ROWS, tq), BF16),
                        pltpu.VMEM((2, hd2, 2 * tq), BF16), pltpu.VMEM((2, hd2 + ONES_ROWS, 2 * tq), F32),
                        pltpu.VMEM((tq, 2 * tq), F32)],
        compiler_params=_params("parallel", "parallel", "arbitrary"),
        name="diff_attention",
    )(q, k_cur, v_cur, *_lambda_args(lq1, lk1, lq2, lk2, subln_g))


def _cached_diff_attention(q, k_cur, v_cur, k_past, v_past, lq1, lk1, lq2, lk2, subln_g, lam_init):
    bsz, t_len, d_attn = q.shape
    hd2 = 2 * HEAD_DIM
    past_len, n_heads = k_past.shape[1], k_past.shape[2]
    tq = t_len
    tkp = _tile(past_len, ATTN_TK_CACHED)
    assert tq % CHUNK == 0 and past_len % CHUNK == 0 and tq <= ATTN_TQ and n_heads * hd2 == d_attn
    assert n_heads % SUBLANES == 0
    flat = lambda c: c.reshape(bsz, past_len * n_heads, hd2)
    const = lambda shape: pl.BlockSpec(shape, lambda b, j: (0, 0))
    rows = pl.BlockSpec((1, t_len, d_attn), lambda b, j: (b, 0, 0))
    hist = pl.BlockSpec((1, tkp * n_heads, hd2), lambda b, j: (b, j, 0))
    return pl.pallas_call(
        functools.partial(_cached_attn_kernel, n_heads=n_heads, tq=tq, tkp=tkp, lam_init=lam_init),
        grid=(bsz, past_len // tkp),
        in_specs=[rows, rows, rows, hist, hist] + [const((1, HEAD_DIM))] * 4 + [const((hd2, 1))],
        out_specs=rows,
        out_shape=jax.ShapeDtypeStruct((bsz, t_len, d_attn), BF16),
        scratch_shapes=[pltpu.VMEM((n_heads, hd2, 2 * tq), BF16), pltpu.VMEM((n_heads, 1, 2 * tq), F32),
                        pltpu.VMEM((n_heads, hd2 + ONES_ROWS, 2 * tq), F32)],
        compiler_params=_params("parallel", "arbitrary"),
        name="cached_diff_attention",
    )(q, k_cur, v_cur, flat(k_past), flat(v_past), *_lambda_args(lq1, lk1, lq2, lk2, subln_g))


def _mem_attn_kernel(q_ref, k_ref, v_ref, o_ref):
    scale = jnp.asarray(MEM_HEAD_DIM ** -0.5, q_ref.dtype)
    for h in range(MEM_HEADS):
        sl = slice(h * MEM_HEAD_DIM, (h + 1) * MEM_HEAD_DIM)
        q = q_ref[0, :, sl] * scale
        s = lax.dot_general(q, k_ref[0, :, sl].astype(BF16), (((1,), (1,)), ((), ())), preferred_element_type=F32)
        p = jnp.exp(s - jnp.max(s, axis=1, keepdims=True))
        p = p / jnp.sum(p, axis=1, keepdims=True)
        o = jnp.dot(p.astype(BF16), v_ref[0, :, sl].astype(BF16), preferred_element_type=F32)
        o_ref[0, :, sl] = o.astype(o_ref.dtype)


def _mem_attention(mq, mk, mv):
    bsz, t_len, d_mem = mq.shape
    n_mem = mk.shape[1]
    tt = _tile(t_len, 512)
    return pl.pallas_call(
        _mem_attn_kernel,
        grid=(bsz, t_len // tt),
        in_specs=[pl.BlockSpec((1, tt, d_mem), lambda b, t: (b, t, 0)),
                  pl.BlockSpec((1, n_mem, d_mem), lambda b, t: (b, 0, 0)),
                  pl.BlockSpec((1, n_mem, d_mem), lambda b, t: (b, 0, 0))],
        out_specs=pl.BlockSpec((1, tt, d_mem), lambda b, t: (b, t, 0)),
        out_shape=jax.ShapeDtypeStruct((bsz, t_len, d_mem), BF16),
        compiler_params=_params("parallel", "parallel"),
        name="mem_attention",
    )(mq, mk, mv)


SLAB_DTYPE = F32
ROW_COPY_UNROLL = 4


def _to_slabs(dst_ref, val, rows, n_slab):
    for c in range(n_slab):
        dst_ref[pl.ds(c, rows, stride=n_slab), :] = val[:, c * LANES:(c + 1) * LANES].astype(dst_ref.dtype)


def _slab_chunk(src_ref, c, rows, n_slab, row0=0):
    return src_ref[pl.ds(row0 * n_slab + c, rows, stride=n_slab), :]


def _router_kernel(*refs, tr, n_slab, group_tiles):
    n_grp = len(group_tiles)
    x_refs = refs[:n_grp]
    g_ref, whi_ref, wlo_ref, b_ref, hn_ref, eid_ref, wt_ref, rank_ref, cnt_ref, run_ref = refs[n_grp:]
    t = pl.program_id(0)
    x = x_refs[-1][...]
    first_tile = sum(group_tiles[:-1])
    for gi in range(n_grp - 2, -1, -1):
        x = jnp.where(t < first_tile, x_refs[gi][...], x)
        first_tile -= group_tiles[gi]
    hn = x * lax.rsqrt(jnp.mean(x * x, axis=-1, keepdims=True) + EPS) * g_ref[...]
    _to_slabs(hn_ref, hn, tr, n_slab)
    h_hi = hn.astype(BF16)
    h_lo = (hn - h_hi.astype(F32)).astype(BF16)
    logits = (jnp.dot(h_hi, whi_ref[...], preferred_element_type=F32)
              + jnp.dot(h_lo, whi_ref[...], preferred_element_type=F32)
              + jnp.dot(h_hi, wlo_ref[...], preferred_element_type=F32)) + b_ref[...]
    lane = lax.broadcasted_iota(jnp.int32, logits.shape, 1)
    first = lambda hit: jnp.min(jnp.where(hit, lane, LANES), axis=1, keepdims=True)

    gl = jnp.where(lane < N_GROUPS, logits, -jnp.inf)
    gmax = jnp.max(gl, axis=1, keepdims=True)
    g_idx = first(gl == gmax)
    p_g = 1.0 / jnp.sum(jnp.exp(gl - gmax), axis=1, keepdims=True)

    lo = N_GROUPS + EXPERTS_PER_GROUP * g_idx
    in_grp = (lane >= lo) & (lane < lo + EXPERTS_PER_GROUP)
    el = jnp.where(in_grp, logits, -jnp.inf)
    pe = jnp.exp(el - jnp.max(el, axis=1, keepdims=True))
    probs = jnp.where(in_grp, pe / jnp.sum(pe, axis=1, keepdims=True), -1.0)
    v1 = jnp.max(probs, axis=1, keepdims=True)
    i1 = first(probs == v1)
    probs2 = jnp.where(lane == i1, -1.0, probs)
    v2 = jnp.max(probs2, axis=1, keepdims=True)
    i2 = first(probs2 == v2)
    tot = v1 + v2
    eid_ref[:, 0:1] = i1 - N_GROUPS
    eid_ref[:, 1:2] = i2 - N_GROUPS
    wt_ref[:, 0:1] = p_g * (v1 / tot)
    wt_ref[:, 1:2] = p_g * (v2 / tot)

    @pl.when(t == 0)
    def _():
        run_ref[...] = jnp.zeros(run_ref.shape, F32)

    sel1 = lane == i1
    sel2 = lane == i2
    oh1 = jnp.where(sel1, 1.0, 0.0).astype(BF16)
    oh2 = jnp.where(sel2, 1.0, 0.0).astype(BF16)
    earlier = (lax.broadcasted_iota(jnp.int32, (tr, tr), 1) < lax.broadcasted_iota(jnp.int32, (tr, tr), 0))
    tri = jnp.where(earlier, 1.0, 0.0).astype(BF16)
    run = run_ref[...]
    cnt1 = jnp.sum(oh1.astype(F32), axis=0, keepdims=True)
    cnt2 = jnp.sum(oh2.astype(F32), axis=0, keepdims=True)
    before1 = jnp.dot(tri, oh1, preferred_element_type=F32) + run
    before2 = jnp.dot(tri, oh2, preferred_element_type=F32) + (run + cnt1)
    rank_ref[:, 0:1] = jnp.sum(jnp.where(sel1, before1, 0.0), axis=1, keepdims=True).astype(jnp.int32)
    rank_ref[:, 1:2] = jnp.sum(jnp.where(sel2, before2, 0.0), axis=1, keepdims=True).astype(jnp.int32)
    run = run + cnt1 + cnt2
    run_ref[...] = run
    cnt_ref[...] = run.astype(jnp.int32)


def _router(xs, g, w_rg, b_rg, w_re, b_re):
    d = xs[0].shape[1]
    n = sum(x.shape[0] for x in xs)
    n_slab = d // LANES
    tr = 256
    assert all(x.shape[0] % tr == 0 for x in xs)
    group_tiles = tuple(x.shape[0] // tr for x in xs)
    x_specs = []
    for gi, nt in enumerate(group_tiles):
        t0 = sum(group_tiles[:gi])
        x_specs.append(pl.BlockSpec((tr, d), lambda i, t0=t0, nt=nt: (jnp.clip(i - t0, 0, nt - 1), 0)))
    w = jnp.zeros((d, LANES), F32).at[:, :N_GROUPS].set(w_rg).at[:, N_GROUPS:N_GROUPS + N_EXPERTS].set(w_re)
    b = jnp.zeros((1, LANES), F32).at[0, :N_GROUPS].set(b_rg).at[0, N_GROUPS:N_GROUPS + N_EXPERTS].set(b_re)
    w_hi = w.astype(BF16)
    w_lo = (w - w_hi.astype(F32)).astype(BF16)
    small = lambda dt: jax.ShapeDtypeStruct((n, TOP_K), dt)
    small_spec = pl.BlockSpec((tr, TOP_K), lambda i: (i, 0))
    return pl.pallas_call(
        functools.partial(_router_kernel, tr=tr, n_slab=n_slab, group_tiles=group_tiles),
        grid=(n // tr,),
        in_specs=x_specs + [
                  pl.BlockSpec((1, d), lambda i: (0, 0)),
                  pl.BlockSpec((d, LANES), lambda i: (0, 0)),
                  pl.BlockSpec((d, LANES), lambda i: (0, 0)),
                  pl.BlockSpec((1, LANES), lambda i: (0, 0))],
        out_specs=[pl.BlockSpec((tr * n_slab, LANES), lambda i: (i, 0)), small_spec, small_spec, small_spec,
                   pl.BlockSpec((1, LANES), lambda i: (0, 0))],
        out_shape=[jax.ShapeDtypeStruct((n * n_slab, LANES), SLAB_DTYPE), small(jnp.int32), small(F32), small(jnp.int32),
                   jax.ShapeDtypeStruct((1, LANES), jnp.int32)],
        scratch_shapes=[pltpu.VMEM((1, LANES), F32)],
        compiler_params=_params("arbitrary"),
        name="moe_router",
    )(*xs, g.reshape(1, d).astype(F32), w_hi, w_lo, b)


def _dispatch_kernel(pos_ref, zstart_ref, zvalid_ref, n_active_ref, hn_ref, xs_out, zbuf, sem, zsem,
                     *, td, tm, n_slab, n_tiles):
    t = pl.program_id(0)
    zrows = tm * n_slab

    def zero_tile(row0):
        return pltpu.make_async_copy(zbuf, xs_out.at[pl.ds(pl.multiple_of(row0 * n_slab, n_slab), zrows), :], zsem)

    @pl.when(t == 0)
    def _():
        zbuf[...] = jnp.zeros(zbuf.shape, zbuf.dtype)

        def for_each_zero_tile(act):
            for e in range(N_EXPERTS):
                @pl.when(zvalid_ref[e] > 0)
                def _():
                    act(zero_tile(zstart_ref[e]))

            def unused(i, carry):
                act(zero_tile(i * tm))
                return carry
            lax.fori_loop(n_active_ref[0], n_tiles, unused, 0)

        for_each_zero_tile(lambda copy: copy.start())
        for_each_zero_tile(lambda copy: copy.wait())

    def body(r, carry):
        src = hn_ref.at[pl.ds(pl.multiple_of(r * n_slab, n_slab), n_slab), :]
        for kk in range(TOP_K):
            dst_row = pl.multiple_of(pos_ref[(t * td + r) * TOP_K + kk] * n_slab, n_slab)
            pltpu.make_async_copy(src, xs_out.at[pl.ds(dst_row, n_slab), :], sem).start()
        return carry
    lax.fori_loop(0, td, body, 0, unroll=ROW_COPY_UNROLL)
    for kk in range(TOP_K):
        pltpu.make_async_copy(hn_ref, xs_out.at[pl.ds(0, td * n_slab), :], sem).wait()


def _expert_kernel(tile_expert_ref, n_active_ref, x_ref, wg_ref, wu_ref, wd_ref, ys_ref, xb_ref, wgb_ref, wub_ref,
                   wdb_ref, *, tm, n_slab):
    t = pl.program_id(0)
    active = t < n_active_ref[0]

    @pl.when(jnp.logical_not(active))
    def _():
        ys_ref[...] = jnp.zeros(ys_ref.shape, ys_ref.dtype)

    @pl.when(active & ((t == 0) | (tile_expert_ref[t] != tile_expert_ref[jnp.maximum(t - 1, 0)])))
    def _():
        wgb_ref[...] = wg_ref[0].astype(BF16)
        wub_ref[...] = wu_ref[0].astype(BF16)
        wdb_ref[...] = wd_ref[0].astype(BF16)

    @pl.when(active)
    def _():
        for c in range(n_slab):
            xb_ref[:, c * LANES:(c + 1) * LANES] = _slab_chunk(x_ref, c, tm, n_slab).astype(BF16)
        x = xb_ref[...]
        g = jnp.dot(x, wgb_ref[...], preferred_element_type=F32)
        u = jnp.dot(x, wub_ref[...], preferred_element_type=F32)
        act = (g * jax.nn.sigmoid(g) * u).astype(BF16)
        _to_slabs(ys_ref, jnp.dot(act, wdb_ref[...], preferred_element_type=F32), tm, n_slab)


def _start_slab_gather(pos_ref, tok0, ys_hbm, ybuf, sem, slot, tc, n_slab):
    def body(r, carry):
        for kk in range(TOP_K):
            src_row = pl.multiple_of(pos_ref[(tok0 + r) * TOP_K + kk] * n_slab, n_slab)
            dst_row = pl.multiple_of((kk * tc + r) * n_slab, n_slab)
            pltpu.make_async_copy(ys_hbm.at[pl.ds(src_row, n_slab), :],
                                  ybuf.at[slot, pl.ds(dst_row, n_slab), :], sem.at[slot]).start()
        return carry
    lax.fori_loop(0, tc, body, 0, unroll=ROW_COPY_UNROLL)


def _combine_kernel(pos_ref, x_ref, w_ref, g_ref, ys_hbm, o_ref, ybuf, sem, *, tc, tok_base, n_slab):
    t = pl.program_id(0)
    nt = pl.num_programs(0)
    slot = lax.rem(t, 2)

    @pl.when(t == 0)
    def _():
        _start_slab_gather(pos_ref, tok_base, ys_hbm, ybuf, sem, 0, tc, n_slab)

    @pl.when(t + 1 < nt)
    def _():
        _start_slab_gather(pos_ref, tok_base + (t + 1) * tc, ys_hbm, ybuf, sem, 1 - slot, tc, n_slab)

    pltpu.make_async_copy(ys_hbm.at[pl.ds(0, TOP_K * tc * n_slab), :], ybuf.at[slot], sem.at[slot]).wait()
    yb = ybuf.at[slot]
    wk = [jnp.broadcast_to(w_ref[:, kk:kk + 1], (tc, LANES)) for kk in range(TOP_K)]
    sq = jnp.zeros((tc, LANES), F32)
    for c in range(n_slab):
        moe = wk[0] * _slab_chunk(yb, c, tc, n_slab)
        for kk in range(1, TOP_K):
            moe = moe + wk[kk] * _slab_chunk(yb, c, tc, n_slab, row0=kk * tc)
        y = x_ref[:, c * LANES:(c + 1) * LANES] + moe
        o_ref[:, c * LANES:(c + 1) * LANES] = y
        sq = sq + y * y
    d = x_ref.shape[1]
    scale = lax.rsqrt(jnp.sum(sq, axis=1, keepdims=True) / d + EPS)
    o_ref[...] = o_ref[...] * scale * g_ref[...]


def _hier_moe_final(xs, norm_ffn, w_rg, b_rg, w_re, b_re, w_g, w_u, w_d, norm_final):
    d = xs[0].shape[1]
    n = sum(x.shape[0] for x in xs)
    n_slab = d // LANES
    d_exp = w_g.shape[2]
    hn, eid, wts, rank, cnt = _router(xs, norm_ffn, w_rg, b_rg, w_re, b_re)

    tm = 256
    na = n * TOP_K
    n_tiles = (na + N_EXPERTS * (tm - 1)) // tm + 1
    n_rows = n_tiles * tm
    counts = cnt[0, N_GROUPS:N_GROUPS + N_EXPERTS]
    padded = ((counts + tm - 1) // tm) * tm
    pend = jnp.cumsum(padded)
    pstart = pend - padded
    experts = jnp.arange(N_EXPERTS, dtype=jnp.int32)
    pos = (jnp.sum(jnp.where(eid[:, :, None] == experts, pstart, 0), axis=-1) + rank).reshape(na)
    tile_start = jnp.arange(n_tiles, dtype=jnp.int32) * tm
    tile_expert = jnp.minimum(jnp.sum((pend[None, :] <= tile_start[:, None]).astype(jnp.int32), axis=1), N_EXPERTS - 1)
    n_active = (pend[-1] // tm).astype(jnp.int32).reshape(1)
    zstart = jnp.maximum(pend - tm, 0).astype(jnp.int32)
    zvalid = (counts > 0).astype(jnp.int32)

    td = _tile(n, 256)
    x_sorted = pl.pallas_call(
        functools.partial(_dispatch_kernel, td=td, tm=tm, n_slab=n_slab, n_tiles=n_tiles),
        grid_spec=pltpu.PrefetchScalarGridSpec(
            num_scalar_prefetch=4,
            grid=(n // td,),
            in_specs=[pl.BlockSpec((td * n_slab, LANES), lambda t, p, zs, zv, na_: (t, 0))],
            out_specs=pl.BlockSpec(memory_space=pl.ANY),
            scratch_shapes=[pltpu.VMEM((tm * n_slab, LANES), SLAB_DTYPE), pltpu.SemaphoreType.DMA(()),
                            pltpu.SemaphoreType.DMA(())]),
        out_shape=jax.ShapeDtypeStruct((n_rows * n_slab, LANES), SLAB_DTYPE),
        compiler_params=_params("arbitrary"),
        name="moe_dispatch",
    )(pos, zstart, zvalid, n_active, hn)

    active = lambda t, te, na_: (jnp.minimum(t, na_[0] - 1), 0)
    ys = pl.pallas_call(
        functools.partial(_expert_kernel, tm=tm, n_slab=n_slab),
        grid_spec=pltpu.PrefetchScalarGridSpec(
            num_scalar_prefetch=2,
            grid=(n_tiles,),
            in_specs=[pl.BlockSpec((tm * n_slab, LANES), active),
                      pl.BlockSpec((1, d, d_exp), lambda t, te, na_: (te[t], 0, 0)),
                      pl.BlockSpec((1, d, d_exp), lambda t, te, na_: (te[t], 0, 0)),
                      pl.BlockSpec((1, d_exp, d), lambda t, te, na_: (te[t], 0, 0))],
            out_specs=pl.BlockSpec((tm * n_slab, LANES), lambda t, te, na_: (t, 0)),
            scratch_shapes=[pltpu.VMEM((tm, d), BF16), pltpu.VMEM((d, d_exp), BF16), pltpu.VMEM((d, d_exp), BF16),
                            pltpu.VMEM((d_exp, d), BF16)]),
        out_shape=jax.ShapeDtypeStruct((n_rows * n_slab, LANES), SLAB_DTYPE),
        compiler_params=_params("arbitrary"),
        name="moe_experts",
    )(tile_expert, n_active, x_sorted, w_g, w_u, w_d)

    outs = []
    row0 = 0
    for x in xs:
        rows = x.shape[0]
        tc = _tile(rows, 256)
        assert row0 % tc == 0
        blk0 = row0 // tc
        outs.append(pl.pallas_call(
            functools.partial(_combine_kernel, tc=tc, tok_base=row0, n_slab=n_slab),
            grid_spec=pltpu.PrefetchScalarGridSpec(
                num_scalar_prefetch=1,
                grid=(rows // tc,),
                in_specs=[pl.BlockSpec((tc, d), lambda t, p: (t, 0)),
                          pl.BlockSpec((tc, TOP_K), lambda t, p, blk0=blk0: (blk0 + t, 0)),
                          pl.BlockSpec((1, d), lambda t, p: (0, 0)),
                          pl.BlockSpec(memory_space=pl.ANY)],
                out_specs=pl.BlockSpec((tc, d), lambda t, p: (t, 0)),
                scratch_shapes=[pltpu.VMEM((2, TOP_K * tc * n_slab, LANES), SLAB_DTYPE),
                                pltpu.SemaphoreType.DMA((2,))]),
            out_shape=jax.ShapeDtypeStruct((rows, d), F32),
            compiler_params=_params("arbitrary"),
            name="moe_combine",
        )(pos, x, wts, norm_final.reshape(1, d).astype(F32), ys))
        row0 += rows
    return outs


def _mixer(groups, lam_init, w):
    d = groups[0][0].shape[2]
    d_conv = w["w_dw"].shape[1]
    d_attn = w["w_attn_out"].shape[0]
    d_mem = w["w_mem_out"].shape[0]
    c0 = 2 * d_conv
    c1, c2, c3 = c0 + d_attn, c0 + 2 * d_attn, c0 + 3 * d_attn
    c4 = c3 + d_mem
    x2ds = [x.reshape(-1, d) for x, *_ in groups]

    hs = [_rmsnorm(x2d, w["norm_mix"], BF16) for x2d in x2ds]
    w_in = w["w_in"]
    u_ins = _matmul(hs, w_in, F32, b_col0=0, n_out=c0)
    qs = _matmul(hs, w_in, F32, b_col0=c0, n_out=d_attn)
    k_rows = _matmul(hs, w_in, F32, b_col0=c1, n_out=d_attn)
    v_rows = _matmul(hs, w_in, F32, b_col0=c2, n_out=d_attn)
    mqs = _matmul(hs, w_in, BF16, b_col0=c3, n_out=d_mem)
    gates = _matmul(hs, w_in, BF16, b_col0=c4, n_out=N_BRANCH * d)

    lam_args = (w["lambda_q1"], w["lambda_k1"], w["lambda_q2"], w["lambda_k2"], w["subln_g"], lam_init)
    conv_acts, new_bufs, os_, mos = [], [], [], []
    for g, (x, conv_buf, k_past, v_past, mk, mv) in enumerate(groups):
        bsz, t_len, _ = x.shape
        conv_act, new_buf = _conv_branch(u_ins[g].reshape(bsz, t_len, c0), conv_buf, w["w_dw"], w["b_dw"],
                                         w["conv_ln_g"], w["conv_ln_b"])
        qkv = [a.reshape(bsz, t_len, d_attn) for a in (qs[g], k_rows[g], v_rows[g])]
        if k_past is None:
            o = _diff_attention(*qkv, *lam_args)
        else:
            o = _cached_diff_attention(*qkv, k_past, v_past, *lam_args)
        mo = _mem_attention(mqs[g].reshape(bsz, t_len, d_mem), mk, mv)
        conv_acts.append(conv_act.reshape(-1, d_conv))
        new_bufs.append(new_buf)
        os_.append(o.reshape(-1, d_attn))
        mos.append(mo.reshape(-1, d_mem))

    merged = _matmul(conv_acts, w["w_conv_out"], BF16, gates=gates, gate_col0=0)
    merged = _matmul(os_, w["w_attn_out"], BF16, gates=gates, gate_col0=d, adds=merged)
    merged = _matmul(mos, w["w_mem_out"], BF16, gates=gates, gate_col0=2 * d, adds=merged)
    x2s = _matmul(merged, w["w_out"], F32, adds=x2ds)
    hd2 = 2 * HEAD_DIM
    rows4d = lambda r, x: r.reshape(x.shape[0], x.shape[1], N_HEADS, hd2)
    return [(x2s[g], new_bufs[g], rows4d(k_rows[g], x), rows4d(v_rows[g], x)) for g, (x, *_) in enumerate(groups)]


def kernel(x_prompt, x_sample, mem_prompt, cache_conv, cache_diff_k, cache_diff_v, cache_mem_k, cache_mem_v, norm_mix, w_in, w_dw, b_dw, conv_ln_g, conv_ln_b, w_conv_out, lambda_q1, lambda_k1, lambda_q2, lambda_k2, subln_g, w_attn_out, norm_mem, w_mem_k, w_mem_v, w_mem_out, w_out, norm_ffn, w_router_grp, b_router_grp, w_router_exp, b_router_exp, w_exp_gate, w_exp_up, w_exp_down, norm_final):
    depth = w_in.shape[0]
    assert depth == 1, "the final rmsnorm is fused into the last layer's MoE combine"
    bp, tp, d = x_prompt.shape
    bs, ts, _ = x_sample.shape
    n_mem = mem_prompt.shape[1]
    d_conv = w_dw.shape[2]
    d_mem = w_mem_k.shape[2]
    l = 0
    lam_init = 0.8 - 0.6 * math.exp(-0.3 * l)
    w = dict(norm_mix=norm_mix[l], w_in=w_in[l], w_dw=w_dw[l], b_dw=b_dw[l], conv_ln_g=conv_ln_g[l],
             conv_ln_b=conv_ln_b[l], w_conv_out=w_conv_out[l], lambda_q1=lambda_q1[l],
             lambda_k1=lambda_k1[l], lambda_q2=lambda_q2[l], lambda_k2=lambda_k2[l], subln_g=subln_g[l],
             w_attn_out=w_attn_out[l], w_mem_out=w_mem_out[l], w_out=w_out[l])

    hm = _rmsnorm(mem_prompt.reshape(bp * n_mem, d), norm_mem[l], BF16)
    mk = _matmul([hm], w_mem_k[l], F32)[0].reshape(bp, n_mem, d_mem)
    mv = _matmul([hm], w_mem_v[l], F32)[0].reshape(bp, n_mem, d_mem)
    conv0 = jnp.zeros((bp, CONV_W - 1, d_conv), F32)
    prompt = (x_prompt, conv0, None, None, mk, mv)
    sample = (x_sample, cache_conv.reshape(cache_conv.shape[1:]), cache_diff_k.reshape(cache_diff_k.shape[1:]),
              cache_diff_v.reshape(cache_diff_v.shape[1:]), cache_mem_k.reshape(bs, n_mem, d_mem),
              cache_mem_v.reshape(bs, n_mem, d_mem))
    (x2_p, cb_p, k_p, v_p), (x2_s, cb_s, k_s, v_s) = _mixer([prompt, sample], lam_init, w)

    y_p, y_s = _hier_moe_final([x2_p, x2_s], norm_ffn[l], w_router_grp[l], b_router_grp[l], w_router_exp[l],
                               b_router_exp[l], w_exp_gate[l], w_exp_up[l], w_exp_down[l], norm_final)
    mem_shape = (1, bp, n_mem, MEM_HEADS, MEM_HEAD_DIM)
    return (y_p.reshape(bp, tp, d), y_s.reshape(bs, ts, d), cb_p[None], k_p[None], v_p[None],
            mk.reshape(mem_shape), mv.reshape(mem_shape), cb_s[None], k_s[None], v_s[None])
```

```python
import functools
import math

import jax
import jax.numpy as jnp
from jax import lax
from jax.experimental import pallas as pl
from jax.experimental.pallas import tpu as pltpu

F32 = jnp.float32
BF16 = jnp.bfloat16

CHUNK = 64
EPS = 1e-6
SUBLN_EPS = 1e-5
NEG = -1e30
CONV_W = 31
N_HEADS = 16
HEAD_DIM = 64
MEM_HEADS = 4
MEM_HEAD_DIM = 256
N_BRANCH = 3
N_GROUPS = 4
EXPERTS_PER_GROUP = 8
N_EXPERTS = N_GROUPS * EXPERTS_PER_GROUP
TOP_K = 2

LANES = 128
SUBLANES = 8
VMEM_LIMIT_BYTES = 56 * 1024 * 1024
HALO = 32


def _params(*semantics):
    return pltpu.CompilerParams(dimension_semantics=semantics, vmem_limit_bytes=VMEM_LIMIT_BYTES)


def _tile(n, pref):
    if n <= pref:
        return n
    t = pref
    while n % t:
        t //= 2
    return t


def _rmsnorm_kernel(x_ref, g_ref, o_ref, *, eps):
    x = x_ref[...].astype(F32)
    ms = jnp.mean(x * x, axis=-1, keepdims=True)
    o_ref[...] = (x * lax.rsqrt(ms + eps) * g_ref[...]).astype(o_ref.dtype)


def _rmsnorm(x, g, out_dtype, eps=EPS):
    n, d = x.shape
    tr = _tile(n, 512)
    return pl.pallas_call(
        functools.partial(_rmsnorm_kernel, eps=eps),
        grid=(n // tr,),
        in_specs=[pl.BlockSpec((tr, d), lambda i: (i, 0)), pl.BlockSpec((1, d), lambda i: (0, 0))],
        out_specs=pl.BlockSpec((tr, d), lambda i: (i, 0)),
        out_shape=jax.ShapeDtypeStruct((n, d), out_dtype),
        compiler_params=_params("parallel"),
        name="rmsnorm",
    )(x, g.reshape(1, d).astype(F32))


def _mm_kernel(*refs, n_grp, tile_ranges, has_gate, has_add):
    a_refs, b_ref = refs[:n_grp], refs[n_grp]
    pos = n_grp + 1
    gate_refs = refs[pos:pos + n_grp] if has_gate else None
    pos += n_grp * has_gate
    add_refs = refs[pos:pos + n_grp] if has_add else None
    o_refs, b_bf16 = refs[-1 - n_grp:-1], refs[-1]
    i = pl.program_id(1)

    @pl.when(i == 0)
    def _():
        b_bf16[...] = b_ref[...].astype(BF16)

    for g, (t0, t1) in enumerate(tile_ranges):
        @pl.when((i >= t0) & (i < t1))
        def _():
            acc = jnp.dot(a_refs[g][...], b_bf16[...], preferred_element_type=F32)
            if has_gate:
                acc = acc * jax.nn.sigmoid(gate_refs[g][...].astype(F32))
            if has_add:
                acc = acc + add_refs[g][...].astype(F32)
            o_refs[g][...] = acc.astype(o_refs[g].dtype)


def _matmul(a_groups, b, out_dtype, *, b_col0=0, n_out=None, gates=None, gate_col0=0, adds=None):
    k = a_groups[0].shape[1]
    assert all(a.dtype == BF16 for a in a_groups) and b.dtype == F32
    n_out = b.shape[1] if n_out is None else n_out
    tn = _tile(n_out, 1024)
    assert b_col0 % tn == 0 and gate_col0 % tn == 0
    bj0, gj0 = b_col0 // tn, gate_col0 // tn
    tms = [_tile(a.shape[0], 1024) for a in a_groups]
    tiles = [a.shape[0] // tm for a, tm in zip(a_groups, tms)]
    starts = [sum(tiles[:g]) for g in range(len(tiles))]
    tile_ranges = tuple((t0, t0 + nt) for t0, nt in zip(starts, tiles))

    def rows(g, col0=None):
        t0, nt = starts[g], tiles[g]
        if col0 is None:
            return lambda j, i: (jnp.clip(i - t0, 0, nt - 1), 0)
        return lambda j, i: (jnp.clip(i - t0, 0, nt - 1), col0 + j)

    n_grp = len(a_groups)
    in_specs = [pl.BlockSpec((tms[g], k), rows(g)) for g in range(n_grp)]
    size = lambda dt: jnp.dtype(dt).itemsize
    per_row = k * size(BF16) + tn * (size(out_dtype) + (size(gates[0].dtype) if gates is not None else 0)
                                     + (size(adds[0].dtype) if adds is not None else 0))
    estimate = 2 * sum(tms) * per_row + k * tn * (2 * size(F32) + size(BF16)) + max(tms) * tn * size(F32)
    weight_buffers = 2 if estimate <= VMEM_LIMIT_BYTES else 1
    in_specs.append(pl.BlockSpec((k, tn), lambda j, i: (0, bj0 + j), pipeline_mode=pl.Buffered(weight_buffers)))
    args = list(a_groups) + [b]
    if gates is not None:
        in_specs += [pl.BlockSpec((tms[g], tn), rows(g, gj0)) for g in range(n_grp)]
        args += list(gates)
    if adds is not None:
        in_specs += [pl.BlockSpec((tms[g], tn), rows(g, 0)) for g in range(n_grp)]
        args += list(adds)
    return pl.pallas_call(
        functools.partial(_mm_kernel, n_grp=n_grp, tile_ranges=tile_ranges, has_gate=gates is not None,
                          has_add=adds is not None),
        grid=(n_out // tn, sum(tiles)),
        in_specs=in_specs,
        out_specs=[pl.BlockSpec((tms[g], tn), rows(g, 0)) for g in range(n_grp)],
        out_shape=[jax.ShapeDtypeStruct((a.shape[0], n_out), out_dtype) for a in a_groups],
        scratch_shapes=[pltpu.VMEM((k, tn), BF16)],
        compiler_params=_params("parallel", "arbitrary"),
        name="matmul",
    )(*args)


def _conv_kernel(u_ref, buf_ref, w_ref, bdw_ref, g_ref, b_ref, act_ref, newbuf_ref, ext_ref, shift_ref, *, tt, dc):
    t = pl.program_id(1)
    hist = CONV_W - 1

    @pl.when(t == 0)
    def _():
        ext_ref[pl.ds(HALO - hist, hist), :] = buf_ref[0].astype(F32)

    @pl.when(t > 0)
    def _():
        ext_ref[pl.ds(0, HALO), :] = ext_ref[pl.ds(tt, HALO), :]

    a = u_ref[0, :, :dc].astype(F32)
    gt = u_ref[0, :, dc:].astype(F32)
    ext_ref[pl.ds(HALO, tt), :] = a * jax.nn.sigmoid(gt)

    n_ext = HALO + tt
    ext = ext_ref[...]
    for b in range(1, SUBLANES):
        shift_ref[b - 1] = pltpu.roll(ext, n_ext - b, axis=0)
    y = jnp.zeros((tt, dc), F32) + bdw_ref[...]
    for j in range(CONV_W):
        a, b = divmod(HALO - hist + j, SUBLANES)
        src = ext_ref if b == 0 else shift_ref.at[b - 1]
        y = y + src[pl.ds(a * SUBLANES, tt), :] * w_ref[pl.ds(j, 1), :]
    mu = jnp.mean(y, axis=-1, keepdims=True)
    var = jnp.mean(jnp.square(y - mu), axis=-1, keepdims=True)
    z = (y - mu) * lax.rsqrt(var + EPS) * g_ref[...] + b_ref[...]
    act_ref[0] = (z * jax.nn.sigmoid(z)).astype(act_ref.dtype)
    newbuf_ref[0] = ext_ref[pl.ds(HALO + tt - hist, hist), :]


def _conv_branch(u_in, conv_buf, w_dw, b_dw, ln_g, ln_b):
    bsz, t_len, two_dc = u_in.shape
    dc = two_dc // 2
    tt = _tile(t_len, 256)
    assert tt >= HALO and tt % SUBLANES == 0
    vec = lambda v: v.reshape(1, dc).astype(F32)
    return pl.pallas_call(
        functools.partial(_conv_kernel, tt=tt, dc=dc),
        grid=(bsz, t_len // tt),
        in_specs=[pl.BlockSpec((1, tt, two_dc), lambda b, t: (b, t, 0)),
                  pl.BlockSpec((1, CONV_W - 1, dc), lambda b, t: (b, 0, 0)),
                  pl.BlockSpec((CONV_W, dc), lambda b, t: (0, 0)),
                  pl.BlockSpec((1, dc), lambda b, t: (0, 0)),
                  pl.BlockSpec((1, dc), lambda b, t: (0, 0)),
                  pl.BlockSpec((1, dc), lambda b, t: (0, 0))],
        out_specs=[pl.BlockSpec((1, tt, dc), lambda b, t: (b, t, 0)),
                   pl.BlockSpec((1, CONV_W - 1, dc), lambda b, t: (b, 0, 0))],
        out_shape=[jax.ShapeDtypeStruct((bsz, t_len, dc), BF16),
                   jax.ShapeDtypeStruct((bsz, CONV_W - 1, dc), F32)],
        scratch_shapes=[pltpu.VMEM((HALO + tt, dc), F32), pltpu.VMEM((SUBLANES - 1, HALO + tt, dc), F32)],
        compiler_params=_params("parallel", "arbitrary"),
        name="conv_branch",
    )(u_in, conv_buf, w_dw.astype(F32), vec(b_dw), vec(ln_g), vec(ln_b))


ONES_ROWS = 16
ATTN_TQ = 1024
ATTN_LANE_GROUP = 512
ATTN_TK_CACHED = 512
ATTN_HEAD_GROUP = 4
LOG2E = 1.4426950408889634


def _stacked_qt(q):
    tq = q.shape[0]
    qt = (q * (HEAD_DIM ** -0.5 * LOG2E)).T
    z = jnp.zeros((HEAD_DIM, tq), F32)
    return jnp.concatenate([jnp.concatenate([qt[:HEAD_DIM], z], axis=0),
                            jnp.concatenate([z, qt[HEAD_DIM:]], axis=0)], axis=1).astype(BF16)


def _vt_ones(v):
    vt = v.astype(F32).T.astype(BF16)
    return jnp.concatenate([vt, jnp.ones((ONES_ROWS, v.shape[0]), BF16)], axis=0)


def _scores(k_blk, qqt):
    return jnp.dot(k_blk, qqt, preferred_element_type=F32)


def _softmax_step(m_prev, acc_prev, s, vt_blk):
    m_new = jnp.maximum(m_prev, jnp.max(s, axis=0, keepdims=True))
    alpha = jnp.exp2(m_prev - m_new)
    p = jnp.exp2(s - m_new).astype(BF16)
    return m_new, alpha * acc_prev + jnp.dot(vt_blk, p, preferred_element_type=F32)


def _mask_aligned_block(s, lane0, tq):
    q_chunk = lax.rem(lax.broadcasted_iota(jnp.int32, (1, s.shape[1]), 1) + lane0, tq) // CHUNK
    bands = [jnp.where(q_chunk >= c, s[c * CHUNK:(c + 1) * CHUNK, :], NEG) for c in range(s.shape[0] // CHUNK)]
    return jnp.concatenate(bands, axis=0)


def _diff_finalize(acc, lam, g_col, tq, lam_init):
    hd2 = 2 * HEAD_DIM
    inv_l = 1.0 / acc[hd2:hd2 + 1, :]
    ot = acc[:hd2, :tq] * inv_l[:, :tq] - lam * (acc[:hd2, tq:] * inv_l[:, tq:])
    ot = ot * lax.rsqrt(jnp.mean(ot * ot, axis=0, keepdims=True) + SUBLN_EPS) * g_col
    return (ot * (1.0 - lam_init)).T


def _lambda(lq1_ref, lk1_ref, lq2_ref, lk2_ref, lam_init):
    return (jnp.exp(jnp.sum(lq1_ref[...] * lk1_ref[...], axis=1, keepdims=True))
            - jnp.exp(jnp.sum(lq2_ref[...] * lk2_ref[...], axis=1, keepdims=True)) + lam_init)


def _diff_attn_kernel(q_ref, k_ref, v_ref, lq1_ref, lk1_ref, lq2_ref, lk2_ref, g_ref, o_ref, k_scr, vt_scr, qq_scr,
                      acc_ref, s_ref, *, n_blk, tq, lam_init):
    g = pl.program_id(2)

    @pl.when(g == 0)
    def _():
        def body(i, carry):
            off = pl.multiple_of(i * tq, tq)
            k_scr[i] = k_ref[0, pl.ds(off, tq), :].astype(BF16)
            vt_scr[i] = _vt_ones(v_ref[0, pl.ds(off, tq), :])
            return carry
        lax.fori_loop(0, n_blk, body, 0)

    for r in range(2):
        qq_scr[r] = _stacked_qt(q_ref[0, r * tq:(r + 1) * tq, :].astype(F32))
    acc_ref[...] = jnp.zeros(acc_ref.shape, F32)
    lg = min(ATTN_LANE_GROUP, tq)
    n_lg = 2 * tq // lg
    group = lambda c: slice(c * lg, (c + 1) * lg)

    def keys_needed(c, masked):
        return (c % (tq // lg) + 1) * lg if masked else tq

    def issue(kb, r, c, masked=False):
        nk = keys_needed(c, masked)
        s_ref[:nk, group(c)] = _scores(k_scr[kb, :nk], qq_scr[r, :, group(c)])

    def block(kb, r, m_prev, masked, then):
        m_new = []
        for c in range(n_lg):
            if c + 1 < n_lg:
                issue(kb, r, c + 1, masked)
            elif then is not None:
                issue(then[0], then[1], 0)
            nk = keys_needed(c, masked)
            s = s_ref[:nk, group(c)]
            if masked:
                s = _mask_aligned_block(s, c * lg, tq)
            m_c, acc = _softmax_step(m_prev[c], acc_ref[r, :, group(c)], s, vt_scr[kb, :, :nk])
            acc_ref[r, :, group(c)] = acc
            m_new.append(m_c)
        return tuple(m_new)

    def query_block(r, n_full_pairs, odd_full, then):
        def two_blocks(i2, m_prev):
            m_mid = block(2 * i2, r, m_prev, False, (2 * i2 + 1, r))
            return block(2 * i2 + 1, r, m_mid, False, (2 * i2 + 2, r))
        m = lax.fori_loop(0, n_full_pairs, two_blocks, tuple(jnp.full((1, lg), NEG, F32) for _ in range(n_lg)))
        if odd_full:
            m = block(2 * n_full_pairs, r, m, False, (2 * n_full_pairs + 1, r))
        block(2 * n_full_pairs + odd_full, r, m, True, then)
        lam = _lambda(lq1_ref, lk1_ref, lq2_ref, lk2_ref, lam_init)
        o_ref[0, r * tq:(r + 1) * tq, :] = _diff_finalize(acc_ref[r], lam, g_ref[...], tq, lam_init).astype(o_ref.dtype)

    issue(0, 0, 0)
    query_block(0, g, 0, (0, 1))
    query_block(1, g, 1, None)


def _cached_attn_kernel(q_ref, kc_ref, vc_ref, kp_ref, vp_ref, lq1_ref, lk1_ref, lq2_ref, lk2_ref, g_ref, o_ref,
                        qq_scr, m_scr, acc_scr, *, n_heads, tq, tkp, lam_init):
    j = pl.program_id(1)
    hd2 = 2 * HEAD_DIM
    head = lambda h: slice(h * hd2, (h + 1) * hd2)

    @pl.when(j == 0)
    def _():
        for h in range(n_heads):
            qq_scr[h] = _stacked_qt(q_ref[0, :, head(h)].astype(F32))
        m_scr[...] = jnp.full(m_scr.shape, NEG, F32)
        acc_scr[...] = jnp.zeros(acc_scr.shape, F32)

    def update(heads, keys, values, masked):
        ss = [_scores(keys(h).astype(BF16), qq_scr[h]) for h in heads]
        if masked:
            ss = [_mask_aligned_block(s, 0, tq) for s in ss]
        vts = [_vt_ones(values(h)) for h in heads]
        res = [_softmax_step(m_scr[h], acc_scr[h], s, vt) for h, s, vt in zip(heads, ss, vts)]
        for h, (m_new, acc) in zip(heads, res):
            m_scr[h] = m_new
            acc_scr[h] = acc

    head_groups = [range(h0, min(h0 + ATTN_HEAD_GROUP, n_heads)) for h0 in range(0, n_heads, ATTN_HEAD_GROUP)]
    for heads in head_groups:
        update(heads, lambda h: kp_ref[0, pl.ds(h, tkp, stride=n_heads), :],
               lambda h: vp_ref[0, pl.ds(h, tkp, stride=n_heads), :], False)

    @pl.when(j == pl.num_programs(1) - 1)
    def _():
        lam = _lambda(lq1_ref, lk1_ref, lq2_ref, lk2_ref, lam_init)
        for heads in head_groups:
            update(heads, lambda h: kc_ref[0, :, head(h)], lambda h: vc_ref[0, :, head(h)], True)
            for h in heads:
                o_ref[0, :, head(h)] = _diff_finalize(acc_scr[h], lam, g_ref[...], tq, lam_init).astype(o_ref.dtype)


def _lambda_args(lq1, lk1, lq2, lk2, subln_g):
    vec = lambda v: v.reshape(1, -1).astype(F32)
    return [vec(lq1), vec(lk1), vec(lq2), vec(lk2), subln_g.reshape(-1, 1).astype(F32)]


def _diff_attention(q, k_cur, v_cur, lq1, lk1, lq2, lk2, subln_g, lam_init):
    bsz, t_len, d_attn = q.shape
    hd2 = 2 * HEAD_DIM
    tq = _tile(t_len // 2, ATTN_TQ)
    n_blk = t_len // tq
    assert tq % CHUNK == 0 and n_blk % 2 == 0 and tq % min(ATTN_LANE_GROUP, tq) == 0
    const = lambda shape: pl.BlockSpec(shape, lambda b, h, i: (0, 0))
    return pl.pallas_call(
        functools.partial(_diff_attn_kernel, n_blk=n_blk, tq=tq, lam_init=lam_init),
        grid=(bsz, d_attn // hd2, n_blk // 2),
        in_specs=[pl.BlockSpec((1, 2 * tq, hd2), lambda b, h, i: (b, i, h)),
                  pl.BlockSpec((1, t_len, hd2), lambda b, h, i: (b, 0, h)),
                  pl.BlockSpec((1, t_len, hd2), lambda b, h, i: (b, 0, h))]
        + [const((1, HEAD_DIM))] * 4 + [const((hd2, 1))],
        out_specs=pl.BlockSpec((1, 2 * tq, hd2), lambda b, h, i: (b, i, h)),
        out_shape=jax.ShapeDtypeStruct((bsz, t_len, d_attn), BF16),
        scratch_shapes=[pltpu.VMEM((n_blk, tq, hd2), BF16), pltpu.VMEM((n_blk, hd2 + ONES_ROWS, tq), BF16),
                        pltpu.VMEM((2, hd2, 2 * tq), BF16), pltpu.VMEM((2, hd2 + ONES_ROWS, 2 * tq), F32),
                        pltpu.VMEM((tq, 2 * tq), F32)],
        compiler_params=_params("parallel", "parallel", "arbitrary"),
        name="diff_attention",
    )(q, k_cur, v_cur, *_lambda_args(lq1, lk1, lq2, lk2, subln_g))


def _cached_diff_attention(q, k_cur, v_cur, k_past, v_past, lq1, lk1, lq2, lk2, subln_g, lam_init):
    bsz, t_len, d_attn = q.shape
    hd2 = 2 * HEAD_DIM
    past_len, n_heads = k_past.shape[1], k_past.shape[2]
    tq = t_len
    tkp = _tile(past_len, ATTN_TK_CACHED)
    assert tq % CHUNK == 0 and past_len % CHUNK == 0 and tq <= ATTN_TQ and n_heads * hd2 == d_attn
    assert n_heads % SUBLANES == 0
    flat = lambda c: c.reshape(bsz, past_len * n_heads, hd2)
    const = lambda shape: pl.BlockSpec(shape, lambda b, j: (0, 0))
    rows = pl.BlockSpec((1, t_len, d_attn), lambda b, j: (b, 0, 0))
    hist = pl.BlockSpec((1, tkp * n_heads, hd2), lambda b, j: (b, j, 0))
    return pl.pallas_call(
        functools.partial(_cached_attn_kernel, n_heads=n_heads, tq=tq, tkp=tkp, lam_init=lam_init),
        grid=(bsz, past_len // tkp),
        in_specs=[rows, rows, rows, hist, hist] + [const((1, HEAD_DIM))] * 4 + [const((hd2, 1))],
        out_specs=rows,
        out_shape=jax.ShapeDtypeStruct((bsz, t_len, d_attn), BF16),
        scratch_shapes=[pltpu.VMEM((n_heads, hd2, 2 * tq), BF16), pltpu.VMEM((n_heads, 1, 2 * tq), F32),
                        pltpu.VMEM((n_heads, hd2 + ONES_ROWS, 2 * tq), F32)],
        compiler_params=_params("parallel", "arbitrary"),
        name="cached_diff_attention",
    )(q, k_cur, v_cur, flat(k_past), flat(v_past), *_lambda_args(lq1, lk1, lq2, lk2, subln_g))


def _mem_attn_kernel(q_ref, k_ref, v_ref, o_ref):
    scale = jnp.asarray(MEM_HEAD_DIM ** -0.5, q_ref.dtype)
    for h in range(MEM_HEADS):
        sl = slice(h * MEM_HEAD_DIM, (h + 1) * MEM_HEAD_DIM)
        q = q_ref[0, :, sl] * scale
        s = lax.dot_general(q, k_ref[0, :, sl].astype(BF16), (((1,), (1,)), ((), ())), preferred_element_type=F32)
        p = jnp.exp(s - jnp.max(s, axis=1, keepdims=True))
        p = p / jnp.sum(p, axis=1, keepdims=True)
        o = jnp.dot(p.astype(BF16), v_ref[0, :, sl].astype(BF16), preferred_element_type=F32)
        o_ref[0, :, sl] = o.astype(o_ref.dtype)


def _mem_attention(mq, mk, mv):
    bsz, t_len, d_mem = mq.shape
    n_mem = mk.shape[1]
    tt = _tile(t_len, 512)
    return pl.pallas_call(
        _mem_attn_kernel,
        grid=(bsz, t_len // tt),
        in_specs=[pl.BlockSpec((1, tt, d_mem), lambda b, t: (b, t, 0)),
                  pl.BlockSpec((1, n_mem, d_mem), lambda b, t: (b, 0, 0)),
                  pl.BlockSpec((1, n_mem, d_mem), lambda b, t: (b, 0, 0))],
        out_specs=pl.BlockSpec((1, tt, d_mem), lambda b, t: (b, t, 0)),
        out_shape=jax.ShapeDtypeStruct((bsz, t_len, d_mem), BF16),
        compiler_params=_params("parallel", "parallel"),
        name="mem_attention",
    )(mq, mk, mv)


SLAB_DTYPE = F32
ROW_COPY_UNROLL = 4


def _to_slabs(dst_ref, val, rows, n_slab):
    for c in range(n_slab):
        dst_ref[pl.ds(c, rows, stride=n_slab), :] = val[:, c * LANES:(c + 1) * LANES].astype(dst_ref.dtype)


def _slab_chunk(src_ref, c, rows, n_slab, row0=0):
    return src_ref[pl.ds(row0 * n_slab + c, rows, stride=n_slab), :]


def _router_kernel(*refs, tr, n_slab, group_tiles):
    n_grp = len(group_tiles)
    x_refs = refs[:n_grp]
    g_ref, whi_ref, wlo_ref, b_ref, hn_ref, eid_ref, wt_ref, rank_ref, cnt_ref, run_ref = refs[n_grp:]
    t = pl.program_id(0)
    x = x_refs[-1][...]
    first_tile = sum(group_tiles[:-1])
    for gi in range(n_grp - 2, -1, -1):
        x = jnp.where(t < first_tile, x_refs[gi][...], x)
        first_tile -= group_tiles[gi]
    hn = x * lax.rsqrt(jnp.mean(x * x, axis=-1, keepdims=True) + EPS) * g_ref[...]
    _to_slabs(hn_ref, hn, tr, n_slab)
    h_hi = hn.astype(BF16)
    h_lo = (hn - h_hi.astype(F32)).astype(BF16)
    logits = (jnp.dot(h_hi, whi_ref[...], preferred_element_type=F32)
              + jnp.dot(h_lo, whi_ref[...], preferred_element_type=F32)
              + jnp.dot(h_hi, wlo_ref[...], preferred_element_type=F32)) + b_ref[...]
    lane = lax.broadcasted_iota(jnp.int32, logits.shape, 1)
    first = lambda hit: jnp.min(jnp.where(hit, lane, LANES), axis=1, keepdims=True)

    gl = jnp.where(lane < N_GROUPS, logits, -jnp.inf)
    gmax = jnp.max(gl, axis=1, keepdims=True)
    g_idx = first(gl == gmax)
    p_g = 1.0 / jnp.sum(jnp.exp(gl - gmax), axis=1, keepdims=True)

    lo = N_GROUPS + EXPERTS_PER_GROUP * g_idx
    in_grp = (lane >= lo) & (lane < lo + EXPERTS_PER_GROUP)
    el = jnp.where(in_grp, logits, -jnp.inf)
    pe = jnp.exp(el - jnp.max(el, axis=1, keepdims=True))
    probs = jnp.where(in_grp, pe / jnp.sum(pe, axis=1, keepdims=True), -1.0)
    v1 = jnp.max(probs, axis=1, keepdims=True)
    i1 = first(probs == v1)
    probs2 = jnp.where(lane == i1, -1.0, probs)
    v2 = jnp.max(probs2, axis=1, keepdims=True)
    i2 = first(probs2 == v2)
    tot = v1 + v2
    eid_ref[:, 0:1] = i1 - N_GROUPS
    eid_ref[:, 1:2] = i2 - N_GROUPS
    wt_ref[:, 0:1] = p_g * (v1 / tot)
    wt_ref[:, 1:2] = p_g * (v2 / tot)

    @pl.when(t == 0)
    def _():
        run_ref[...] = jnp.zeros(run_ref.shape, F32)

    sel1 = lane == i1
    sel2 = lane == i2
    oh1 = jnp.where(sel1, 1.0, 0.0).astype(BF16)
    oh2 = jnp.where(sel2, 1.0, 0.0).astype(BF16)
    earlier = (lax.broadcasted_iota(jnp.int32, (tr, tr), 1) < lax.broadcasted_iota(jnp.int32, (tr, tr), 0))
    tri = jnp.where(earlier, 1.0, 0.0).astype(BF16)
    run = run_ref[...]
    cnt1 = jnp.sum(oh1.astype(F32), axis=0, keepdims=True)
    cnt2 = jnp.sum(oh2.astype(F32), axis=0, keepdims=True)
    before1 = jnp.dot(tri, oh1, preferred_element_type=F32) + run
    before2 = jnp.dot(tri, oh2, preferred_element_type=F32) + (run + cnt1)
    rank_ref[:, 0:1] = jnp.sum(jnp.where(sel1, before1, 0.0), axis=1, keepdims=True).astype(jnp.int32)
    rank_ref[:, 1:2] = jnp.sum(jnp.where(sel2, before2, 0.0), axis=1, keepdims=True).astype(jnp.int32)
    run = run + cnt1 + cnt2
    run_ref[...] = run
    cnt_ref[...] = run.astype(jnp.int32)


def _router(xs, g, w_rg, b_rg, w_re, b_re):
    d = xs[0].shape[1]
    n = sum(x.shape[0] for x in xs)
    n_slab = d // LANES
    tr = 256
    assert all(x.shape[0] % tr == 0 for x in xs)
    group_tiles = tuple(x.shape[0] // tr for x in xs)
    x_specs = []
    for gi, nt in enumerate(group_tiles):
        t0 = sum(group_tiles[:gi])
        x_specs.append(pl.BlockSpec((tr, d), lambda i, t0=t0, nt=nt: (jnp.clip(i - t0, 0, nt - 1), 0)))
    w = jnp.zeros((d, LANES), F32).at[:, :N_GROUPS].set(w_rg).at[:, N_GROUPS:N_GROUPS + N_EXPERTS].set(w_re)
    b = jnp.zeros((1, LANES), F32).at[0, :N_GROUPS].set(b_rg).at[0, N_GROUPS:N_GROUPS + N_EXPERTS].set(b_re)
    w_hi = w.astype(BF16)
    w_lo = (w - w_hi.astype(F32)).astype(BF16)
    small = lambda dt: jax.ShapeDtypeStruct((n, TOP_K), dt)
    small_spec = pl.BlockSpec((tr, TOP_K), lambda i: (i, 0))
    return pl.pallas_call(
        functools.partial(_router_kernel, tr=tr, n_slab=n_slab, group_tiles=group_tiles),
        grid=(n // tr,),
        in_specs=x_specs + [
                  pl.BlockSpec((1, d), lambda i: (0, 0)),
                  pl.BlockSpec((d, LANES), lambda i: (0, 0)),
                  pl.BlockSpec((d, LANES), lambda i: (0, 0)),
                  pl.BlockSpec((1, LANES), lambda i: (0, 0))],
        out_specs=[pl.BlockSpec((tr * n_slab, LANES), lambda i: (i, 0)), small_spec, small_spec, small_spec,
                   pl.BlockSpec((1, LANES), lambda i: (0, 0))],
        out_shape=[jax.ShapeDtypeStruct((n * n_slab, LANES), SLAB_DTYPE), small(jnp.int32), small(F32), small(jnp.int32),
                   jax.ShapeDtypeStruct((1, LANES), jnp.int32)],
        scratch_shapes=[pltpu.VMEM((1, LANES), F32)],
        compiler_params=_params("arbitrary"),
        name="moe_router",
    )(*xs, g.reshape(1, d).astype(F32), w_hi, w_lo, b)


def _dispatch_kernel(pos_ref, zstart_ref, zvalid_ref, n_active_ref, hn_ref, xs_out, zbuf, sem, zsem,
                     *, td, tm, n_slab, n_tiles):
    t = pl.program_id(0)
    zrows = tm * n_slab

    def zero_tile(row0):
        return pltpu.make_async_copy(zbuf, xs_out.at[pl.ds(pl.multiple_of(row0 * n_slab, n_slab), zrows), :], zsem)

    @pl.when(t == 0)
    def _():
        zbuf[...] = jnp.zeros(zbuf.shape, zbuf.dtype)

        def for_each_zero_tile(act):
            for e in range(N_EXPERTS):
                @pl.when(zvalid_ref[e] > 0)
                def _():
                    act(zero_tile(zstart_ref[e]))

            def unused(i, carry):
                act(zero_tile(i * tm))
                return carry
            lax.fori_loop(n_active_ref[0], n_tiles, unused, 0)

        for_each_zero_tile(lambda copy: copy.start())
        for_each_zero_tile(lambda copy: copy.wait())

    def body(r, carry):
        src = hn_ref.at[pl.ds(pl.multiple_of(r * n_slab, n_slab), n_slab), :]
        for kk in range(TOP_K):
            dst_row = pl.multiple_of(pos_ref[(t * td + r) * TOP_K + kk] * n_slab, n_slab)
            pltpu.make_async_copy(src, xs_out.at[pl.ds(dst_row, n_slab), :], sem).start()
        return carry
    lax.fori_loop(0, td, body, 0, unroll=ROW_COPY_UNROLL)
    for kk in range(TOP_K):
        pltpu.make_async_copy(hn_ref, xs_out.at[pl.ds(0, td * n_slab), :], sem).wait()


def _expert_kernel(tile_expert_ref, n_active_ref, x_ref, wg_ref, wu_ref, wd_ref, ys_ref, xb_ref, wgb_ref, wub_ref,
                   wdb_ref, *, tm, n_slab):
    t = pl.program_id(0)
    active = t < n_active_ref[0]

    @pl.when(jnp.logical_not(active))
    def _():
        ys_ref[...] = jnp.zeros(ys_ref.shape, ys_ref.dtype)

    @pl.when(active & ((t == 0) | (tile_expert_ref[t] != tile_expert_ref[jnp.maximum(t - 1, 0)])))
    def _():
        wgb_ref[...] = wg_ref[0].astype(BF16)
        wub_ref[...] = wu_ref[0].astype(BF16)
        wdb_ref[...] = wd_ref[0].astype(BF16)

    @pl.when(active)
    def _():
        for c in range(n_slab):
            xb_ref[:, c * LANES:(c + 1) * LANES] = _slab_chunk(x_ref, c, tm, n_slab).astype(BF16)
        x = xb_ref[...]
        g = jnp.dot(x, wgb_ref[...], preferred_element_type=F32)
        u = jnp.dot(x, wub_ref[...], preferred_element_type=F32)
        act = (g * jax.nn.sigmoid(g) * u).astype(BF16)
        _to_slabs(ys_ref, jnp.dot(act, wdb_ref[...], preferred_element_type=F32), tm, n_slab)


def _start_slab_gather(pos_ref, tok0, ys_hbm, ybuf, sem, slot, tc, n_slab):
    def body(r, carry):
        for kk in range(TOP_K):
            src_row = pl.multiple_of(pos_ref[(tok0 + r) * TOP_K + kk] * n_slab, n_slab)
            dst_row = pl.multiple_of((kk * tc + r) * n_slab, n_slab)
            pltpu.make_async_copy(ys_hbm.at[pl.ds(src_row, n_slab), :],
                                  ybuf.at[slot, pl.ds(dst_row, n_slab), :], sem.at[slot]).start()
        return carry
    lax.fori_loop(0, tc, body, 0, unroll=ROW_COPY_UNROLL)


def _combine_kernel(pos_ref, x_ref, w_ref, g_ref, ys_hbm, o_ref, ybuf, sem, *, tc, tok_base, n_slab):
    t = pl.program_id(0)
    nt = pl.num_programs(0)
    slot = lax.rem(t, 2)

    @pl.when(t == 0)
    def _():
        _start_slab_gather(pos_ref, tok_base, ys_hbm, ybuf, sem, 0, tc, n_slab)

    @pl.when(t + 1 < nt)
    def _():
        _start_slab_gather(pos_ref, tok_base + (t + 1) * tc, ys_hbm, ybuf, sem, 1 - slot, tc, n_slab)

    pltpu.make_async_copy(ys_hbm.at[pl.ds(0, TOP_K * tc * n_slab), :], ybuf.at[slot], sem.at[slot]).wait()
    yb = ybuf.at[slot]
    wk = [jnp.broadcast_to(w_ref[:, kk:kk + 1], (tc, LANES)) for kk in range(TOP_K)]
    sq = jnp.zeros((tc, LANES), F32)
    for c in range(n_slab):
        moe = wk[0] * _slab_chunk(yb, c, tc, n_slab)
        for kk in range(1, TOP_K):
            moe = moe + wk[kk] * _slab_chunk(yb, c, tc, n_slab, row0=kk * tc)
        y = x_ref[:, c * LANES:(c + 1) * LANES] + moe
        o_ref[:, c * LANES:(c + 1) * LANES] = y
        sq = sq + y * y
    d = x_ref.shape[1]
    scale = lax.rsqrt(jnp.sum(sq, axis=1, keepdims=True) / d + EPS)
    o_ref[...] = o_ref[...] * scale * g_ref[...]


def _hier_moe_final(xs, norm_ffn, w_rg, b_rg, w_re, b_re, w_g, w_u, w_d, norm_final):
    d = xs[0].shape[1]
    n = sum(x.shape[0] for x in xs)
    n_slab = d // LANES
    d_exp = w_g.shape[2]
    hn, eid, wts, rank, cnt = _router(xs, norm_ffn, w_rg, b_rg, w_re, b_re)

    tm = 256
    na = n * TOP_K
    n_tiles = (na + N_EXPERTS * (tm - 1)) // tm + 1
    n_rows = n_tiles * tm
    counts = cnt[0, N_GROUPS:N_GROUPS + N_EXPERTS]
    padded = ((counts + tm - 1) // tm) * tm
    pend = jnp.cumsum(padded)
    pstart = pend - padded
    experts = jnp.arange(N_EXPERTS, dtype=jnp.int32)
    pos = (jnp.sum(jnp.where(eid[:, :, None] == experts, pstart, 0), axis=-1) + rank).reshape(na)
    tile_start = jnp.arange(n_tiles, dtype=jnp.int32) * tm
    tile_expert = jnp.minimum(jnp.sum((pend[None, :] <= tile_start[:, None]).astype(jnp.int32), axis=1), N_EXPERTS - 1)
    n_active = (pend[-1] // tm).astype(jnp.int32).reshape(1)
    zstart = jnp.maximum(pend - tm, 0).astype(jnp.int32)
    zvalid = (counts > 0).astype(jnp.int32)

    td = _tile(n, 256)
    x_sorted = pl.pallas_call(
        functools.partial(_dispatch_kernel, td=td, tm=tm, n_slab=n_slab, n_tiles=n_tiles),
        grid_spec=pltpu.PrefetchScalarGridSpec(
            num_scalar_prefetch=4,
            grid=(n // td,),
            in_specs=[pl.BlockSpec((td * n_slab, LANES), lambda t, p, zs, zv, na_: (t, 0))],
            out_specs=pl.BlockSpec(memory_space=pl.ANY),
            scratch_shapes=[pltpu.VMEM((tm * n_slab, LANES), SLAB_DTYPE), pltpu.SemaphoreType.DMA(()),
                            pltpu.SemaphoreType.DMA(())]),
        out_shape=jax.ShapeDtypeStruct((n_rows * n_slab, LANES), SLAB_DTYPE),
        compiler_params=_params("arbitrary"),
        name="moe_dispatch",
    )(pos, zstart, zvalid, n_active, hn)

    active = lambda t, te, na_: (jnp.minimum(t, na_[0] - 1), 0)
    ys = pl.pallas_call(
        functools.partial(_expert_kernel, tm=tm, n_slab=n_slab),
        grid_spec=pltpu.PrefetchScalarGridSpec(
            num_scalar_prefetch=2,
            grid=(n_tiles,),
            in_specs=[pl.BlockSpec((tm * n_slab, LANES), active),
                      pl.BlockSpec((1, d, d_exp), lambda t, te, na_: (te[t], 0, 0)),
                      pl.BlockSpec((1, d, d_exp), lambda t, te, na_: (te[t], 0, 0)),
                      pl.BlockSpec((1, d_exp, d), lambda t, te, na_: (te[t], 0, 0))],
            out_specs=pl.BlockSpec((tm * n_slab, LANES), lambda t, te, na_: (t, 0)),
            scratch_shapes=[pltpu.VMEM((tm, d), BF16), pltpu.VMEM((d, d_exp), BF16), pltpu.VMEM((d, d_exp), BF16),
                            pltpu.VMEM((d_exp, d), BF16)]),
        out_shape=jax.ShapeDtypeStruct((n_rows * n_slab, LANES), SLAB_DTYPE),
        compiler_params=_params("arbitrary"),
        name="moe_experts",
    )(tile_expert, n_active, x_sorted, w_g, w_u, w_d)

    outs = []
    row0 = 0
    for x in xs:
        rows = x.shape[0]
        tc = _tile(rows, 256)
        assert row0 % tc == 0
        blk0 = row0 // tc
        outs.append(pl.pallas_call(
            functools.partial(_combine_kernel, tc=tc, tok_base=row0, n_slab=n_slab),
            grid_spec=pltpu.PrefetchScalarGridSpec(
                num_scalar_prefetch=1,
                grid=(rows // tc,),
                in_specs=[pl.BlockSpec((tc, d), lambda t, p: (t, 0)),
                          pl.BlockSpec((tc, TOP_K), lambda t, p, blk0=blk0: (blk0 + t, 0)),
                          pl.BlockSpec((1, d), lambda t, p: (0, 0)),
                          pl.BlockSpec(memory_space=pl.ANY)],
                out_specs=pl.BlockSpec((tc, d), lambda t, p: (t, 0)),
                scratch_shapes=[pltpu.VMEM((2, TOP_K * tc * n_slab, LANES), SLAB_DTYPE),
                                pltpu.SemaphoreType.DMA((2,))]),
            out_shape=jax.ShapeDtypeStruct((rows, d), F32),
            compiler_params=_params("arbitrary"),
            name="moe_combine",
        )(pos, x, wts, norm_final.reshape(1, d).astype(F32), ys))
        row0 += rows
    return outs


def _mixer(groups, lam_init, w):
    d = groups[0][0].shape[2]
    d_conv = w["w_dw"].shape[1]
    d_attn = w["w_attn_out"].shape[0]
    d_mem = w["w_mem_out"].shape[0]
    c0 = 2 * d_conv
    c1, c2, c3 = c0 + d_attn, c0 + 2 * d_attn, c0 + 3 * d_attn
    c4 = c3 + d_mem
    x2ds = [x.reshape(-1, d) for x, *_ in groups]

    hs = [_rmsnorm(x2d, w["norm_mix"], BF16) for x2d in x2ds]
    w_in = w["w_in"]
    u_ins = _matmul(hs, w_in, BF16, b_col0=0, n_out=c0)
    qs = _matmul(hs, w_in, F32, b_col0=c0, n_out=d_attn)
    k_rows = _matmul(hs, w_in, F32, b_col0=c1, n_out=d_attn)
    v_rows = _matmul(hs, w_in, F32, b_col0=c2, n_out=d_attn)
    mqs = _matmul(hs, w_in, BF16, b_col0=c3, n_out=d_mem)
    gates = _matmul(hs, w_in, BF16, b_col0=c4, n_out=N_BRANCH * d)

    lam_args = (w["lambda_q1"], w["lambda_k1"], w["lambda_q2"], w["lambda_k2"], w["subln_g"], lam_init)
    conv_acts, new_bufs, os_, mos = [], [], [], []
    for g, (x, conv_buf, k_past, v_past, mk, mv) in enumerate(groups):
        bsz, t_len, _ = x.shape
        conv_act, new_buf = _conv_branch(u_ins[g].reshape(bsz, t_len, c0), conv_buf, w["w_dw"], w["b_dw"],
                                         w["conv_ln_g"], w["conv_ln_b"])
        qkv = [a.reshape(bsz, t_len, d_attn) for a in (qs[g], k_rows[g], v_rows[g])]
        if k_past is None:
            o = _diff_attention(*qkv, *lam_args)
        else:
            o = _cached_diff_attention(*qkv, k_past, v_past, *lam_args)
        mo = _mem_attention(mqs[g].reshape(bsz, t_len, d_mem), mk, mv)
        conv_acts.append(conv_act.reshape(-1, d_conv))
        new_bufs.append(new_buf)
        os_.append(o.reshape(-1, d_attn))
        mos.append(mo.reshape(-1, d_mem))

    merged = _matmul(conv_acts, w["w_conv_out"], BF16, gates=gates, gate_col0=0)
    merged = _matmul(os_, w["w_attn_out"], BF16, gates=gates, gate_col0=d, adds=merged)
    merged = _matmul(mos, w["w_mem_out"], BF16, gates=gates, gate_col0=2 * d, adds=merged)
    x2s = _matmul(merged, w["w_out"], F32, adds=x2ds)
    hd2 = 2 * HEAD_DIM
    rows4d = lambda r, x: r.reshape(x.shape[0], x.shape[1], N_HEADS, hd2)
    return [(x2s[g], new_bufs[g], rows4d(k_rows[g], x), rows4d(v_rows[g], x)) for g, (x, *_) in enumerate(groups)]


def kernel(x_prompt, x_sample, mem_prompt, cache_conv, cache_diff_k, cache_diff_v, cache_mem_k, cache_mem_v, norm_mix, w_in, w_dw, b_dw, conv_ln_g, conv_ln_b, w_conv_out, lambda_q1, lambda_k1, lambda_q2, lambda_k2, subln_g, w_attn_out, norm_mem, w_mem_k, w_mem_v, w_mem_out, w_out, norm_ffn, w_router_grp, b_router_grp, w_router_exp, b_router_exp, w_exp_gate, w_exp_up, w_exp_down, norm_final):
    depth = w_in.shape[0]
    assert depth == 1, "the final rmsnorm is fused into the last layer's MoE combine"
    bp, tp, d = x_prompt.shape
    bs, ts, _ = x_sample.shape
    n_mem = mem_prompt.shape[1]
    d_conv = w_dw.shape[2]
    d_mem = w_mem_k.shape[2]
    l = 0
    lam_init = 0.8 - 0.6 * math.exp(-0.3 * l)
    w = dict(norm_mix=norm_mix[l], w_in=w_in[l], w_dw=w_dw[l], b_dw=b_dw[l], conv_ln_g=conv_ln_g[l],
             conv_ln_b=conv_ln_b[l], w_conv_out=w_conv_out[l], lambda_q1=lambda_q1[l],
             lambda_k1=lambda_k1[l], lambda_q2=lambda_q2[l], lambda_k2=lambda_k2[l], subln_g=subln_g[l],
             w_attn_out=w_attn_out[l], w_mem_out=w_mem_out[l], w_out=w_out[l])

    hm = _rmsnorm(mem_prompt.reshape(bp * n_mem, d), norm_mem[l], BF16)
    mk = _matmul([hm], w_mem_k[l], F32)[0].reshape(bp, n_mem, d_mem)
    mv = _matmul([hm], w_mem_v[l], F32)[0].reshape(bp, n_mem, d_mem)
    conv0 = jnp.zeros((bp, CONV_W - 1, d_conv), F32)
    prompt = (x_prompt, conv0, None, None, mk, mv)
    sample = (x_sample, cache_conv.reshape(cache_conv.shape[1:]), cache_diff_k.reshape(cache_diff_k.shape[1:]),
              cache_diff_v.reshape(cache_diff_v.shape[1:]), cache_mem_k.reshape(bs, n_mem, d_mem),
              cache_mem_v.reshape(bs, n_mem, d_mem))
    (x2_p, cb_p, k_p, v_p), (x2_s, cb_s, k_s, v_s) = _mixer([prompt, sample], lam_init, w)

    y_p, y_s = _hier_moe_final([x2_p, x2_s], norm_ffn[l], w_router_grp[l], b_router_grp[l], w_router_exp[l],
                               b_router_exp[l], w_exp_gate[l], w_exp_up[l], w_exp_down[l], norm_final)
    mem_shape = (1, bp, n_mem, MEM_HEADS, MEM_HEAD_DIM)
    return (y_p.reshape(bp, tp, d), y_s.reshape(bs, ts, d), cb_p[None], k_p[None], v_p[None],
            mk.reshape(mem_shape), mv.reshape(mem_shape), cb_s[None], k_s[None], v_s[None])
```
